```python
import math
import jax
import jax.numpy as jnp
from jax import lax
import numpy as np

D_MODEL = 1024
BATCH = 2
SEQ = 8192
DEPTH = 2
DEC_BATCH = 32
DEC_SEQ = 4
PAST_LEN = 16384
PAGE_SIZE = 128

D_HEAD = 64
FOX_HEADS = 8
FOX_WIDTH = FOX_HEADS * D_HEAD
POOL_WINDOWS = (2, 4, 8, 16)
POOL_WIDTH = D_MODEL - FOX_WIDTH
POOL_GROUP_W = POOL_WIDTH // len(POOL_WINDOWS)
POOL_BUF = max(POOL_WINDOWS) - 1
EVEN_IN = 3 * FOX_WIDTH + FOX_HEADS + POOL_WIDTH
DIL_CONFIGS = ((128, 1), (512, 4), (2048, 16))
DIL_GROUPS = len(DIL_CONFIGS)
DIL_HEADS = 8
DIL_GROUP_W = DIL_HEADS * D_HEAD
ODD_IN = DIL_GROUPS * 3 * DIL_GROUP_W
REL_BUCKETS = 32
REL_MAX_DIST = 2048
D_FF = 2816
D_PLE = 256
N_EVEN = (DEPTH + 1) // 2
N_ODD = DEPTH // 2
Q_BLOCK = 128
ALPHA = (2 * DEPTH) ** 0.25
BETA = (8 * DEPTH) ** -0.25
LN_EPS = 1e-5
NEG_INF = -1e30

kernel_name = 'fox_pool_dilated_hybrid_step'


def layer_norm(x, g, b):
    xf = x.astype(jnp.float32)
    mu = jnp.mean(xf, -1, keepdims=True)
    var = jnp.mean(jnp.square(xf - mu), -1, keepdims=True)
    return ((xf - mu) * lax.rsqrt(var + LN_EPS) * g + b).astype(x.dtype)


def post_norm(x, sub, g, b):
    return layer_norm(ALPHA * x + sub, g, b)


def swiglu(x, wg, wu, wd):
    return (jax.nn.silu(x @ wg) * (x @ wu)) @ wd


def half_ffn(x, wg, wu, wd, g, b):
    return post_norm(x, 0.5 * swiglu(x, wg, wu, wd), g, b)


def per_layer_embed(x, p, wg, bg, wp):
    return jax.nn.sigmoid(x @ wg + bg) * (p @ wp)


def split_even(z, w_in, b_f):
    B, T, _ = z.shape
    proj = z @ w_in
    hw = FOX_WIDTH
    q = proj[..., :hw].reshape(B, T, FOX_HEADS, D_HEAD)
    k = proj[..., hw:2 * hw].reshape(B, T, FOX_HEADS, D_HEAD)
    v = proj[..., 2 * hw:3 * hw].reshape(B, T, FOX_HEADS, D_HEAD)
    logf = jax.nn.log_sigmoid((proj[..., 3 * hw:3 * hw + FOX_HEADS] + b_f).astype(jnp.float32))
    u = proj[..., 3 * hw + FOX_HEADS:]
    return q, k, v, logf, u


def fox_attend(q, k, v, cq, ck, q_pos, k_pos):
    s = jnp.einsum('bqhd,bkhd->bhqk', q, k, preferred_element_type=jnp.float32) * (D_HEAD ** -0.5)
    s = s + jnp.transpose(cq, (0, 2, 1))[..., :, None] - jnp.transpose(ck, (0, 2, 1))[..., None, :]
    s = jnp.where((k_pos[None, :] <= q_pos[:, None])[None, None], s, NEG_INF)
    p = jax.nn.softmax(s, axis=-1)
    return jnp.einsum('bhqk,bkhd->bqhd', p.astype(v.dtype), v)


def fox_prompt(q, k, v, logf):
    B, T = q.shape[:2]
    c = jnp.cumsum(logf, axis=1)
    k_pos = jnp.arange(T)

    def block(t0):
        qb = lax.dynamic_slice_in_dim(q, t0, Q_BLOCK, 1)
        cb = lax.dynamic_slice_in_dim(c, t0, Q_BLOCK, 1)
        return fox_attend(qb, k, v, cb, c, t0 + jnp.arange(Q_BLOCK), k_pos)

    o = lax.map(block, jnp.arange(T // Q_BLOCK) * Q_BLOCK)
    return jnp.moveaxis(o, 0, 1).reshape(B, T, FOX_WIDTH)


def fox_sample(q, k, v, logf, pool_k, pool_v, pool_logf, page_table):
    Bd, Tn = q.shape[:2]
    P = page_table.shape[1] * PAGE_SIZE
    kp = pool_k[page_table].reshape(Bd, P, FOX_HEADS, D_HEAD)
    vp = pool_v[page_table].reshape(Bd, P, FOX_HEADS, D_HEAD)
    lp = pool_logf[page_table].reshape(Bd, P, FOX_HEADS).astype(jnp.float32)
    k_all = jnp.concatenate([kp, k.astype(kp.dtype)], 1)
    v_all = jnp.concatenate([vp, v.astype(vp.dtype)], 1)
    c_all = jnp.cumsum(jnp.concatenate([lp, logf], 1), axis=1)
    o = fox_attend(q, k_all, v_all, c_all[:, P:], c_all, P + jnp.arange(Tn), jnp.arange(P + Tn))
    return o.reshape(Bd, Tn, FOX_WIDTH)


def pool_mix(ctx, pos0, w_pool, scale):
    P = POOL_BUF
    T = ctx.shape[1] - P
    cs = jnp.cumsum(ctx.astype(jnp.float32), axis=1)
    cs = jnp.concatenate([jnp.zeros_like(cs[:, :1]), cs], axis=1)
    n_avail = pos0 + jnp.arange(T) + 1
    outs = []
    for g, w in enumerate(POOL_WINDOWS):
        sl = slice(g * POOL_GROUP_W, (g + 1) * POOL_GROUP_W)
        win_sum = cs[:, P + 1:, sl] - cs[:, P + 1 - w:P + 1 - w + T, sl]
        mean = win_sum / jnp.minimum(w, n_avail).astype(jnp.float32)[None, :, None]
        outs.append(jnp.einsum('btc,cd->btd', mean - ctx[:, P:, sl].astype(jnp.float32), w_pool[g].astype(jnp.float32)))
    return (jnp.concatenate(outs, -1) * scale).astype(ctx.dtype)


def rel_bucket(dist):
    exact = REL_BUCKETS // 2
    d = jnp.maximum(dist, 1).astype(jnp.float32)
    large = exact + (jnp.log(d / exact) / math.log(REL_MAX_DIST / exact) * (REL_BUCKETS - exact)).astype(jnp.int32)
    large = jnp.minimum(large, REL_BUCKETS - 1)
    return jnp.where(dist < exact, dist, large)


def split_odd(z, w_in):
    B, T, _ = z.shape
    proj = (z @ w_in).reshape(B, T, DIL_GROUPS, 3, DIL_HEADS, D_HEAD)
    return proj[:, :, :, 0], proj[:, :, :, 1], proj[:, :, :, 2]


def dilated_branch(q, kc, vc, q_loc, ctx_abs0, window, dil, bias_tab):
    n_keys = window // dil + 1
    dist = jnp.arange(n_keys) * dil
    idx = q_loc[:, None] - dist[None, :]
    valid = (idx >= 0) & (idx + ctx_abs0 >= 0)
    idx = jnp.maximum(idx, 0)
    kg = jnp.take(kc, idx, axis=1)
    vg = jnp.take(vc, idx, axis=1)
    bias = bias_tab[rel_bucket(dist)].T.astype(jnp.float32)
    s = jnp.einsum('bqhd,bqnhd->bhqn', q, kg, preferred_element_type=jnp.float32) * (D_HEAD ** -0.5)
    s = jnp.where(valid[None, None], s + bias[None, :, None, :], NEG_INF)
    m = jnp.max(s, -1, keepdims=True)
    e = jnp.exp(s - m)
    den = jnp.sum(e, -1)
    o = jnp.einsum('bhqn,bqnhd->bqhd', e, vg.astype(jnp.float32)) / jnp.transpose(den, (0, 2, 1))[..., None]
    return o, m[..., 0] + jnp.log(den)


def dilated_mix(q, kcs, vcs, q_locs, ctx_abs0s, rel_bias):
    B, Tq = q.shape[:2]
    outs, lses = [], []
    for g, (w, d) in enumerate(DIL_CONFIGS):
        o, lse = dilated_branch(q[:, :, g], kcs[g], vcs[g], q_locs[g], ctx_abs0s[g], w, d,
                                rel_bias[:, g * DIL_HEADS:(g + 1) * DIL_HEADS])
        outs.append(o)
        lses.append(lse)
    wts = jax.nn.softmax(jnp.stack(lses), axis=0)
    o = jnp.einsum('gbhq,gbqhd->bqhd', wts, jnp.stack(outs))
    return o.reshape(B, Tq, DIL_GROUP_W).astype(q.dtype)


def dilated_prompt(q, k, v, rel_bias):
    B, T = q.shape[:2]
    kpad = [jnp.pad(k[:, :, g], ((0, 0), (w, 0), (0, 0), (0, 0))) for g, (w, _) in enumerate(DIL_CONFIGS)]
    vpad = [jnp.pad(v[:, :, g], ((0, 0), (w, 0), (0, 0), (0, 0))) for g, (w, _) in enumerate(DIL_CONFIGS)]

    def block(t0):
        qb = lax.dynamic_slice_in_dim(q, t0, Q_BLOCK, 1)
        kcs = [lax.dynamic_slice_in_dim(kp, t0, w + Q_BLOCK, 1) for kp, (w, _) in zip(kpad, DIL_CONFIGS)]
        vcs = [lax.dynamic_slice_in_dim(vp, t0, w + Q_BLOCK, 1) for vp, (w, _) in zip(vpad, DIL_CONFIGS)]
        locs = [w + jnp.arange(Q_BLOCK) for w, _ in DIL_CONFIGS]
        abs0 = [t0 - w for w, _ in DIL_CONFIGS]
        return dilated_mix(qb, kcs, vcs, locs, abs0, rel_bias)

    o = lax.map(block, jnp.arange(T // Q_BLOCK) * Q_BLOCK)
    return jnp.moveaxis(o, 0, 1).reshape(B, T, DIL_GROUP_W)


def dilated_sample(q, k, v, bufs_k, bufs_v, rel_bias):
    Tn = q.shape[1]
    kcs = [jnp.concatenate([bk, k[:, :, g].astype(bk.dtype)], 1) for g, bk in enumerate(bufs_k)]
    vcs = [jnp.concatenate([bv, v[:, :, g].astype(bv.dtype)], 1) for g, bv in enumerate(bufs_v)]
    locs = [bk.shape[1] + jnp.arange(Tn) for bk in bufs_k]
    abs0 = [PAST_LEN - bk.shape[1] for bk in bufs_k]
    o = dilated_mix(q, kcs, vcs, locs, abs0, rel_bias)
    new_k = [c[:, -bk.shape[1]:] for c, bk in zip(kcs, bufs_k)]
    new_v = [c[:, -bv.shape[1]:] for c, bv in zip(vcs, bufs_v)]
    return o, new_k, new_v


def setup_inputs(seed: int = 0) -> dict:
    key = jax.random.key(seed)
    keys = iter(jax.random.split(key, 48))

    def nrm(shape, scale=1.0):
        return scale * jax.random.normal(next(keys), shape, jnp.float32)

    n_pages = PAST_LEN // PAGE_SIZE
    n_used = DEC_BATCH * n_pages
    n_phys = n_used + n_used // 4
    page_table = jax.random.permutation(next(keys), n_phys)[:n_used].reshape(DEC_BATCH, n_pages).astype(jnp.int32)
    b0, b1, b2 = [min(w, PAST_LEN) for w, _ in DIL_CONFIGS]
    return {
        'x_prompt': nrm((BATCH, SEQ, D_MODEL)),
        'x_sample': nrm((DEC_BATCH, DEC_SEQ, D_MODEL)),
        'cache_fox_k': nrm((N_EVEN, n_phys, PAGE_SIZE, FOX_HEADS, D_HEAD)),
        'cache_fox_v': nrm((N_EVEN, n_phys, PAGE_SIZE, FOX_HEADS, D_HEAD)),
        'cache_fox_logf': jax.nn.log_sigmoid(nrm((N_EVEN, n_phys, PAGE_SIZE, FOX_HEADS)) + 2.0),
        'state_pool': nrm((N_EVEN, DEC_BATCH, POOL_BUF, POOL_WIDTH)),
        'cache_dil0_k': nrm((N_ODD, DEC_BATCH, b0, DIL_HEADS, D_HEAD)),
        'cache_dil0_v': nrm((N_ODD, DEC_BATCH, b0, DIL_HEADS, D_HEAD)),
        'cache_dil1_k': nrm((N_ODD, DEC_BATCH, b1, DIL_HEADS, D_HEAD)),
        'cache_dil1_v': nrm((N_ODD, DEC_BATCH, b1, DIL_HEADS, D_HEAD)),
        'cache_dil2_k': nrm((N_ODD, DEC_BATCH, b2, DIL_HEADS, D_HEAD)),
        'cache_dil2_v': nrm((N_ODD, DEC_BATCH, b2, DIL_HEADS, D_HEAD)),
        'page_table': page_table,
        'p_prompt': nrm((DEPTH, BATCH, SEQ, D_PLE)),
        'p_sample': nrm((DEPTH, DEC_BATCH, DEC_SEQ, D_PLE)),
        'w_in_even': nrm((N_EVEN, D_MODEL, EVEN_IN), D_MODEL ** -0.5),
        'b_fgate': 2.0 + nrm((N_EVEN, FOX_HEADS), 0.5),
        'pool_w': nrm((N_EVEN, len(POOL_WINDOWS), POOL_GROUP_W, POOL_GROUP_W), POOL_GROUP_W ** -0.5),
        'pool_scale': 1.0 + nrm((N_EVEN, POOL_WIDTH), 0.1),
        'w_out_even': nrm((N_EVEN, FOX_WIDTH + POOL_WIDTH, D_MODEL), BETA * (FOX_WIDTH + POOL_WIDTH) ** -0.5),
        'w_in_odd': nrm((N_ODD, D_MODEL, ODD_IN), D_MODEL ** -0.5),
        'w_out_odd': nrm((N_ODD, DIL_GROUP_W, D_MODEL), BETA * DIL_GROUP_W ** -0.5),
        'rel_bias': nrm((REL_BUCKETS, DIL_GROUPS * DIL_HEADS), 0.5),
        'ffn1_wg': nrm((DEPTH, D_MODEL, D_FF), D_MODEL ** -0.5),
        'ffn1_wu': nrm((DEPTH, D_MODEL, D_FF), D_MODEL ** -0.5),
        'ffn1_wd': nrm((DEPTH, D_FF, D_MODEL), BETA * D_FF ** -0.5),
        'ffn2_wg': nrm((DEPTH, D_MODEL, D_FF), D_MODEL ** -0.5),
        'ffn2_wu': nrm((DEPTH, D_MODEL, D_FF), D_MODEL ** -0.5),
        'ffn2_wd': nrm((DEPTH, D_FF, D_MODEL), BETA * D_FF ** -0.5),
        'ln_g': 1.0 + nrm((DEPTH, 3, D_MODEL), 0.02),
        'ln_b': nrm((DEPTH, 3, D_MODEL), 0.02),
        'ple_wg': nrm((DEPTH, D_MODEL, D_MODEL), D_MODEL ** -0.5),
        'ple_bg': nrm((DEPTH, D_MODEL), 0.02),
        'ple_wp': nrm((DEPTH, D_PLE, D_MODEL), D_PLE ** -0.5),
    }


def reference(x_prompt, x_sample, cache_fox_k, cache_fox_v, cache_fox_logf, state_pool,
              cache_dil0_k, cache_dil0_v, cache_dil1_k, cache_dil1_v, cache_dil2_k, cache_dil2_v,
              page_table, p_prompt, p_sample, w_in_even, b_fgate, pool_w, pool_scale, w_out_even,
              w_in_odd, w_out_odd, rel_bias, ffn1_wg, ffn1_wu, ffn1_wd, ffn2_wg, ffn2_wu, ffn2_wd,
              ln_g, ln_b, ple_wg, ple_bg, ple_wp):
    dil_bufs_k = (cache_dil0_k, cache_dil1_k, cache_dil2_k)
    dil_bufs_v = (cache_dil0_v, cache_dil1_v, cache_dil2_v)
    yp, ys = x_prompt, x_sample
    fkp, fvp, flp, fks, fvs, fls, plp, pls = [], [], [], [], [], [], [], []
    dkp = [[] for _ in DIL_CONFIGS]
    dvp = [[] for _ in DIL_CONFIGS]
    dks = [[] for _ in DIL_CONFIGS]
    dvs = [[] for _ in DIL_CONFIGS]
    for i in range(DEPTH):
        yp = half_ffn(yp, ffn1_wg[i], ffn1_wu[i], ffn1_wd[i], ln_g[i, 0], ln_b[i, 0])
        ys = half_ffn(ys, ffn1_wg[i], ffn1_wu[i], ffn1_wd[i], ln_g[i, 0], ln_b[i, 0])
        if i % 2 == 0:
            e = i // 2
            qp, kp, vp, lp, up = split_even(yp, w_in_even[e], b_fgate[e])
            qs, ks, vs, ls, us = split_even(ys, w_in_even[e], b_fgate[e])
            att_p = fox_prompt(qp, kp, vp, lp)
            att_s = fox_sample(qs, ks, vs, ls, cache_fox_k[e], cache_fox_v[e], cache_fox_logf[e], page_table)
            ctx_p = jnp.concatenate([jnp.zeros((up.shape[0], POOL_BUF, POOL_WIDTH), up.dtype), up], 1)
            ctx_s = jnp.concatenate([state_pool[e].astype(us.dtype), us], 1)
            pool_p = pool_mix(ctx_p, 0, pool_w[e], pool_scale[e])
            pool_s = pool_mix(ctx_s, PAST_LEN, pool_w[e], pool_scale[e])
            mix_p = jnp.concatenate([att_p, pool_p.astype(att_p.dtype)], -1) @ w_out_even[e]
            mix_s = jnp.concatenate([att_s, pool_s.astype(att_s.dtype)], -1) @ w_out_even[e]
            fkp.append(kp); fvp.append(vp); flp.append(lp)
            fks.append(ks); fvs.append(vs); fls.append(ls)
            plp.append(ctx_p[:, -POOL_BUF:]); pls.append(ctx_s[:, -POOL_BUF:])
        else:
            o = i // 2
            qp, kp, vp = split_odd(yp, w_in_odd[o])
            qs, ks, vs = split_odd(ys, w_in_odd[o])
            mix_p = dilated_prompt(qp, kp, vp, rel_bias) @ w_out_odd[o]
            att_s, nk, nv = dilated_sample(qs, ks, vs, [b[o] for b in dil_bufs_k], [b[o] for b in dil_bufs_v], rel_bias)
            mix_s = att_s @ w_out_odd[o]
            for g, (w, _) in enumerate(DIL_CONFIGS):
                keep = min(w, kp.shape[1])
                dkp[g].append(kp[:, -keep:, g]); dvp[g].append(vp[:, -keep:, g])
                dks[g].append(nk[g]); dvs[g].append(nv[g])
        yp = post_norm(yp, mix_p, ln_g[i, 1], ln_b[i, 1])
        ys = post_norm(ys, mix_s, ln_g[i, 1], ln_b[i, 1])
        yp = half_ffn(yp, ffn2_wg[i], ffn2_wu[i], ffn2_wd[i], ln_g[i, 2], ln_b[i, 2])
        ys = half_ffn(ys, ffn2_wg[i], ffn2_wu[i], ffn2_wd[i], ln_g[i, 2], ln_b[i, 2])
        yp = yp + per_layer_embed(yp, p_prompt[i], ple_wg[i], ple_bg[i], ple_wp[i])
        ys = ys + per_layer_embed(ys, p_sample[i], ple_wg[i], ple_bg[i], ple_wp[i])
    fox_k_prompt = jnp.stack(fkp)
    fox_v_prompt = jnp.stack(fvp)
    fox_logf_prompt = jnp.stack(flp)
    fox_k_sample = jnp.stack(fks)
    fox_v_sample = jnp.stack(fvs)
    fox_logf_sample = jnp.stack(fls)
    pool_prompt = jnp.stack(plp)
    pool_sample = jnp.stack(pls)
    dil0_k_prompt, dil1_k_prompt, dil2_k_prompt = [jnp.stack(a) for a in dkp]
    dil0_v_prompt, dil1_v_prompt, dil2_v_prompt = [jnp.stack(a) for a in dvp]
    dil0_k_sample, dil1_k_sample, dil2_k_sample = [jnp.stack(a) for a in dks]
    dil0_v_sample, dil1_v_sample, dil2_v_sample = [jnp.stack(a) for a in dvs]
    return (yp, ys, fox_k_prompt, fox_v_prompt, fox_logf_prompt, fox_k_sample, fox_v_sample, fox_logf_sample,
            pool_prompt, pool_sample,
            dil0_k_prompt, dil0_v_prompt, dil1_k_prompt, dil1_v_prompt, dil2_k_prompt, dil2_v_prompt,
            dil0_k_sample, dil0_v_sample, dil1_k_sample, dil1_v_sample, dil2_k_sample, dil2_v_sample)
```

```python
import functools
import math

import jax
import jax.numpy as jnp
import numpy as np
from jax import lax
from jax.experimental import pallas as pl
from jax.experimental.pallas import tpu as pltpu

F32 = jnp.float32
BF16 = jnp.bfloat16

D_HEAD = 64
POOL_WINDOWS = (2, 4, 8, 16)
POOL_BUF = max(POOL_WINDOWS) - 1
DIL_CONFIGS = ((128, 1), (512, 4), (2048, 16))
REL_BUCKETS = 32
REL_MAX_DIST = 2048
LN_EPS = 1e-5
NEG_INF = -1e30
PAGE_SIZE = 128

LANES = 128
SUBLANES = 8
VMEM_LIMIT_BYTES = 56 * 1024 * 1024

HEADS_PER_LANE_GROUP = LANES // D_HEAD
HALO_ROWS = 16


def _params(*semantics):
    return pltpu.CompilerParams(dimension_semantics=semantics, vmem_limit_bytes=VMEM_LIMIT_BYTES)


def _dot(a, b):
    return jnp.dot(a, b, preferred_element_type=F32)


def _dot_nt(a, b):
    return lax.dot_general(a, b, (((1,), (1,)), ((), ())), preferred_element_type=F32)


def _layer_norm(z, g, b):
    mu = jnp.mean(z, axis=-1, keepdims=True)
    zc = z - mu
    var = jnp.mean(zc * zc, axis=-1, keepdims=True)
    return zc * lax.rsqrt(var + LN_EPS) * g + b


def _log_sigmoid(x):
    return jnp.minimum(x, 0.0) - jnp.log1p(jnp.exp(-jnp.abs(x)))


def _split3(x):
    hi = x.astype(BF16)
    r1 = x - hi.astype(F32)
    mid = r1.astype(BF16)
    lo = (r1 - mid.astype(F32)).astype(BF16)
    return hi, mid, lo


def _lane_is_even_head(shape):
    return lax.broadcasted_iota(jnp.int32, shape, len(shape) - 1) % LANES < D_HEAD


def _const_spec(shape):
    zeros = (0,) * len(shape)
    return pl.BlockSpec(shape, lambda *_: zeros)


def _ffn_kernel(*refs, alpha, ff_chunk, with_ple):
    if with_ple:
        (x_ref, wg_ref, wu_ref, wd_ref, g_ref, b_ref, p_ref, pwg_ref, pbg_ref, pwp_ref,
         o_ref, acc_ref) = refs
    else:
        x_ref, wg_ref, wu_ref, wd_ref, g_ref, b_ref, o_ref, acc_ref = refs
    x = x_ref[...]
    xb = x.astype(BF16)
    d_ff = wg_ref.shape[1]
    for c in range(d_ff // ff_chunk):
        cols = slice(c * ff_chunk, (c + 1) * ff_chunk)
        gate = _dot(xb, wg_ref[:, cols])
        up = _dot(xb, wu_ref[:, cols])
        h = (gate * jax.nn.sigmoid(gate) * up).astype(BF16)
        part = _dot(h, wd_ref[cols, :])
        if c == 0:
            acc_ref[...] = part
        else:
            acc_ref[...] += part
    y = _layer_norm(alpha * x + 0.5 * acc_ref[...], g_ref[...], b_ref[...])
    if with_ple:
        gate = jax.nn.sigmoid(_dot(y.astype(BF16), pwg_ref[...]) + pbg_ref[...])
        y = y + gate * _dot(p_ref[...].astype(BF16), pwp_ref[...])
    o_ref[...] = y


def _ffn(x, wg, wu, wd, g, b, *, alpha, tm, ple=None):
    rows, d = x.shape
    d_ff = wg.shape[1]
    ff_chunk = 256
    assert rows % tm == 0 and d_ff % ff_chunk == 0
    row_spec = pl.BlockSpec((tm, d), lambda i: (i, 0))
    in_specs = [row_spec, _const_spec((d, d_ff)), _const_spec((d, d_ff)), _const_spec((d_ff, d)),
                _const_spec((1, d)), _const_spec((1, d))]
    args = [x, wg, wu, wd, g, b]
    if ple is not None:
        p, pwg, pbg, pwp = ple
        in_specs += [pl.BlockSpec((tm, p.shape[1]), lambda i: (i, 0)), _const_spec(pwg.shape),
                     _const_spec((1, d)), _const_spec(pwp.shape)]
        args += [p, pwg, pbg, pwp]
    return pl.pallas_call(
        functools.partial(_ffn_kernel, alpha=alpha, ff_chunk=ff_chunk, with_ple=ple is not None),
        out_shape=jax.ShapeDtypeStruct((rows, d), F32),
        grid=(rows // tm,),
        in_specs=in_specs,
        out_specs=row_spec,
        scratch_shapes=[pltpu.VMEM((tm, d), F32)],
        compiler_params=_params("parallel"),
        name="ffn_ple" if ple is not None else "ffn",
    )(*args)


def _even_proj_kernel(y_ref, wqkv_ref, wf_ref, bf_ref, wu_ref,
                      q_ref, k_ref, v_ref, kb_ref, vb_ref, logf_ref, logft_ref, u_ref):
    yb = y_ref[...].astype(BF16)
    hw = q_ref.shape[1]
    heads = logf_ref.shape[1]
    q = _dot(yb, wqkv_ref[:, 0:hw])
    q_ref[...] = (q * (D_HEAD ** -0.5)).astype(BF16)
    k = _dot(yb, wqkv_ref[:, hw:2 * hw])
    k_ref[...] = k
    kb_ref[...] = k.astype(BF16)
    v = _dot(yb, wqkv_ref[:, 2 * hw:3 * hw])
    v_ref[...] = v
    vb_ref[...] = v.astype(BF16)
    logf = _log_sigmoid(_dot(yb, wf_ref[...]) + bf_ref[...])
    logf_ref[...] = logf[:, :heads]
    logft_ref[...] = logf.T[:heads, :]
    u_ref[...] = _dot(yb, wu_ref[...])


def _even_proj(y, wqkv, wf, bf, wu, *, heads, tm):
    rows, d = y.shape
    hw = wqkv.shape[1] // 3
    pw = wu.shape[1]
    row = lambda w: pl.BlockSpec((tm, w), lambda i: (i, 0))
    return pl.pallas_call(
        _even_proj_kernel,
        out_shape=(jax.ShapeDtypeStruct((rows, hw), BF16),
                   jax.ShapeDtypeStruct((rows, hw), F32), jax.ShapeDtypeStruct((rows, hw), F32),
                   jax.ShapeDtypeStruct((rows, hw), BF16), jax.ShapeDtypeStruct((rows, hw), BF16),
                   jax.ShapeDtypeStruct((rows, heads), F32), jax.ShapeDtypeStruct((heads, rows), F32),
                   jax.ShapeDtypeStruct((rows, pw), F32)),
        grid=(rows // tm,),
        in_specs=[row(d), _const_spec(wqkv.shape), _const_spec(wf.shape), _const_spec(bf.shape),
                  _const_spec(wu.shape)],
        out_specs=(row(hw), row(hw), row(hw), row(hw), row(hw), row(heads),
                   pl.BlockSpec((heads, tm), lambda i: (0, i)), row(pw)),
        compiler_params=_params("parallel"),
        name="even_proj",
    )(y, wqkv, wf, bf, wu)


def _cumsum_kernel(x_ref, o_ref, carry_ref, *, blocks_per_seq):
    @pl.when(pl.program_id(0) % blocks_per_seq == 0)
    def _():
        carry_ref[...] = jnp.zeros_like(carry_ref)

    x = x_ref[...]
    tc = x.shape[1]
    src = lax.broadcasted_iota(jnp.int32, (tc, tc), 0)
    dst = lax.broadcasted_iota(jnp.int32, (tc, tc), 1)
    tri = jnp.where(src <= dst, 1.0, 0.0).astype(BF16)
    hi, mid, lo = _split3(x)
    y = _dot(hi, tri) + _dot(mid, tri) + _dot(lo, tri) + carry_ref[:, 0:1]
    o_ref[...] = y
    carry_ref[...] = jnp.broadcast_to(y[:, tc - 1:tc], carry_ref.shape)


def _cumsum_lanes(x, *, seq_len, tc):
    heads, n = x.shape
    assert seq_len % tc == 0
    spec = pl.BlockSpec((heads, tc), lambda i: (0, i))
    return pl.pallas_call(
        functools.partial(_cumsum_kernel, blocks_per_seq=seq_len // tc),
        out_shape=jax.ShapeDtypeStruct((heads, n), F32),
        grid=(n // tc,),
        in_specs=[spec],
        out_specs=spec,
        scratch_shapes=[pltpu.VMEM((heads, LANES), F32)],
        compiler_params=_params("arbitrary"),
        name="logf_cumsum",
    )(x)


def _fox_attn_kernel(qi_ref, ki_ref, q_ref, k_ref, v_ref, cq_ref, ck_ref, o_ref,
                     m_ref, l_ref, acc_ref, *, heads):
    pair = pl.program_id(1)
    qi = qi_ref[pair]
    ki = ki_ref[pair]
    tq = q_ref.shape[0]
    tk = k_ref.shape[0]

    @pl.when(ki == 0)
    def _():
        m_ref[...] = jnp.full_like(m_ref, NEG_INF)
        l_ref[...] = jnp.zeros_like(l_ref)
        acc_ref[...] = jnp.zeros_like(acc_ref)

    even_lane = _lane_is_even_head((tq, LANES))

    def step(masked):
        if masked:
            causal = (lax.broadcasted_iota(jnp.int32, (tq, tk), 1)
                      <= lax.broadcasted_iota(jnp.int32, (tq, tk), 0))
        for g in range(heads // HEADS_PER_LANE_GROUP):
            lanes = slice(g * LANES, (g + 1) * LANES)
            q2 = q_ref[:, lanes]
            k2 = k_ref[:, lanes]
            v2 = v_ref[:, lanes]
            scale_parts, sum_parts, pv_parts = [], [], []
            for sub in range(HEADS_PER_LANE_GROUP):
                h = g * HEADS_PER_LANE_GROUP + sub
                qh = jnp.where(even_lane if sub == 0 else ~even_lane, q2, jnp.zeros_like(q2))
                s = _dot_nt(qh, k2) + cq_ref[:, h:h + 1] - ck_ref[h:h + 1, :]
                if masked:
                    s = jnp.where(causal, s, NEG_INF)
                m_prev = m_ref[h]
                m_new = jnp.maximum(m_prev, jnp.max(s, axis=1, keepdims=True))
                p = jnp.exp(s - jnp.tile(m_new, (1, tk // LANES)))
                m_ref[h] = m_new
                scale_parts.append(jnp.exp(m_prev - m_new))
                sum_parts.append(jnp.sum(p, axis=1, keepdims=True))
                pv_parts.append(_dot(p.astype(BF16), v2))
            scale = jnp.where(even_lane, scale_parts[0], scale_parts[1])
            l_ref[:, lanes] = scale * l_ref[:, lanes] + jnp.where(even_lane, sum_parts[0], sum_parts[1])
            acc_ref[:, lanes] = scale * acc_ref[:, lanes] + jnp.where(even_lane, pv_parts[0], pv_parts[1])

    @pl.when(ki < qi)
    def _():
        step(masked=False)

    @pl.when(ki == qi)
    def _():
        step(masked=True)
        o_ref[...] = (acc_ref[...] / l_ref[...]).astype(o_ref.dtype)


def _fox_attention(q, k, v, c, ct, *, batch, seq_len, heads, tq):
    width = q.shape[1]
    nq = seq_len // tq
    qi_tab = np.concatenate([np.full(i + 1, i) for i in range(nq)]).astype(np.int32)
    ki_tab = np.concatenate([np.arange(i + 1) for i in range(nq)]).astype(np.int32)
    q_map = lambda b, p, qi, ki: (b * nq + qi[p], 0)
    k_map = lambda b, p, qi, ki: (b * nq + ki[p], 0)
    return pl.pallas_call(
        functools.partial(_fox_attn_kernel, heads=heads),
        out_shape=jax.ShapeDtypeStruct(q.shape, BF16),
        grid_spec=pltpu.PrefetchScalarGridSpec(
            num_scalar_prefetch=2,
            grid=(batch, len(qi_tab)),
            in_specs=[pl.BlockSpec((tq, width), q_map), pl.BlockSpec((tq, width), k_map),
                      pl.BlockSpec((tq, width), k_map), pl.BlockSpec((tq, heads), q_map),
                      pl.BlockSpec((heads, tq), lambda b, p, qi, ki: (0, b * nq + ki[p]))],
            out_specs=pl.BlockSpec((tq, width), q_map),
            scratch_shapes=[pltpu.VMEM((heads, tq, LANES), F32), pltpu.VMEM((tq, width), F32),
                            pltpu.VMEM((tq, width), F32)],
        ),
        compiler_params=_params("parallel", "arbitrary"),
        name="fox_attention",
    )(jnp.asarray(qi_tab), jnp.asarray(ki_tab), q, k, v, c, ct)


def _pool_groups(ctx_ref, n_avail, w_pool_ref, scale_ref, rows, halo):
    gw = w_pool_ref.shape[1]
    outs = []
    for g, w in enumerate(POOL_WINDOWS):
        lanes = slice(g * gw, (g + 1) * gw)
        cur = ctx_ref[pl.ds(halo, rows), lanes]
        win = cur
        for j in range(1, w):
            win = win + ctx_ref[pl.ds(halo - j, rows), lanes]
        mean = win / jnp.minimum(float(w), n_avail)
        outs.append(_dot((mean - cur).astype(BF16), w_pool_ref[g]))
    return jnp.concatenate(outs, axis=-1) * scale_ref[...]


def _even_mix_kernel(att_ref, u_ref, halo_ref, y_ref, wpool_ref, pscale_ref, wo_ref, g_ref, b_ref,
                     o_ref, ctx_ref, *, alpha, blocks_per_seq):
    tm = u_ref.shape[0]
    fw = att_ref.shape[1]
    blk = pl.program_id(0) % blocks_per_seq
    ctx_ref[0:HALO_ROWS, :] = jnp.where(blk == 0, 0.0, halo_ref[...])
    ctx_ref[HALO_ROWS:HALO_ROWS + tm, :] = u_ref[...]
    pos = blk * tm + lax.broadcasted_iota(jnp.int32, (tm, 1), 0)
    n_avail = (pos + 1).astype(F32)
    pool = _pool_groups(ctx_ref, n_avail, wpool_ref, pscale_ref, tm, HALO_ROWS)
    mix = _dot(att_ref[...], wo_ref[0:fw, :]) + _dot(pool.astype(BF16), wo_ref[fw:, :])
    o_ref[...] = _layer_norm(alpha * y_ref[...] + mix, g_ref[...], b_ref[...])


def _even_mix(att, u, y, wpool, pscale, wo, g, b, *, alpha, seq_len, tm):
    rows, d = y.shape
    fw, pw = att.shape[1], u.shape[1]
    halo_blocks = tm // HALO_ROWS
    row = lambda w: pl.BlockSpec((tm, w), lambda i: (i, 0))
    return pl.pallas_call(
        functools.partial(_even_mix_kernel, alpha=alpha, blocks_per_seq=seq_len // tm),
        out_shape=jax.ShapeDtypeStruct((rows, d), F32),
        grid=(rows // tm,),
        in_specs=[row(fw), row(pw),
                  pl.BlockSpec((HALO_ROWS, pw), lambda i: (jnp.maximum(i * halo_blocks - 1, 0), 0)),
                  row(d), _const_spec(wpool.shape), _const_spec(pscale.shape), _const_spec(wo.shape),
                  _const_spec(g.shape), _const_spec(b.shape)],
        out_specs=row(d),
        scratch_shapes=[pltpu.VMEM((HALO_ROWS + tm, pw), F32)],
        compiler_params=_params("parallel"),
        name="even_mix",
    )(att, u, u, y, wpool, pscale, wo, g, b)


def _pool_sample_kernel(ctx_ref, wpool_ref, pscale_ref, o_ref, *, pos0, n_new):
    gw = wpool_ref.shape[1]
    for i in range(n_new):
        n_avail = float(pos0 + i + 1)
        outs = []
        for g, w in enumerate(POOL_WINDOWS):
            lanes = slice(g * gw, (g + 1) * gw)
            cur = ctx_ref[POOL_BUF + i, :, lanes]
            win = cur
            for j in range(1, w):
                win = win + ctx_ref[POOL_BUF + i - j, :, lanes]
            mean = win / min(float(w), n_avail)
            outs.append(_dot((mean - cur).astype(BF16), wpool_ref[g]))
        o_ref[i] = jnp.concatenate(outs, axis=-1) * pscale_ref[...]


def _pool_sample(ctx_tm, wpool, pscale, *, pos0, n_new):
    steps, nb, c = ctx_tm.shape
    return pl.pallas_call(
        functools.partial(_pool_sample_kernel, pos0=pos0, n_new=n_new),
        out_shape=jax.ShapeDtypeStruct((n_new, nb, c), F32),
        grid=(1,),
        in_specs=[_const_spec(ctx_tm.shape), _const_spec(wpool.shape), _const_spec(pscale.shape)],
        out_specs=_const_spec((n_new, nb, c)),
        compiler_params=_params("arbitrary"),
        name="pool_sample",
    )(ctx_tm, wpool, pscale)


def _pair_mix_kernel(a_ref, p_ref, y_ref, wo_ref, g_ref, b_ref, o_ref, *, alpha):
    fw = a_ref.shape[1]
    mix = _dot(a_ref[...], wo_ref[0:fw, :]) + _dot(p_ref[...].astype(BF16), wo_ref[fw:, :])
    o_ref[...] = _layer_norm(alpha * y_ref[...] + mix, g_ref[...], b_ref[...])


def _pair_mix(a, p, y, wo, g, b, *, alpha, tm):
    rows, d = y.shape
    row = lambda w: pl.BlockSpec((tm, w), lambda i: (i, 0))
    return pl.pallas_call(
        functools.partial(_pair_mix_kernel, alpha=alpha),
        out_shape=jax.ShapeDtypeStruct((rows, d), F32),
        grid=(rows // tm,),
        in_specs=[row(a.shape[1]), row(p.shape[1]), row(d), _const_spec(wo.shape),
                  _const_spec(g.shape), _const_spec(b.shape)],
        out_specs=row(d),
        compiler_params=_params("parallel"),
        name="pair_mix",
    )(a, p, y, wo, g, b)


def _fox_sample_kernel(*refs, pages_per_step, n_new, heads):
    pt_ref = refs[0]
    del pt_ref
    qbd_ref, knew_ref, vnew_ref, lnew_ref = refs[1:5]
    k_refs = refs[5:5 + pages_per_step]
    v_refs = refs[5 + pages_per_step:5 + 2 * pages_per_step]
    lf_refs = refs[5 + 2 * pages_per_step:5 + 3 * pages_per_step]
    o_ref, m_ref, l_ref, acc_ref, carry_ref = refs[5 + 3 * pages_per_step:]
    step = pl.program_id(1)
    n_rows = qbd_ref.shape[0]
    qbd = qbd_ref[...]

    @pl.when(step == 0)
    def _():
        lf = lnew_ref[...]
        lane = lax.broadcasted_iota(jnp.int32, lf.shape, 1)
        tok = lax.broadcasted_iota(jnp.int32, lf.shape, 0) // heads
        upto = jnp.where(lax.broadcasted_iota(jnp.int32, (LANES, LANES), 0)
                         <= lax.broadcasted_iota(jnp.int32, (LANES, LANES), 1), 1.0, 0.0).astype(BF16)
        hi, mid, lo = _split3(lf)
        pre = _dot(hi, upto) + _dot(mid, upto) + _dot(lo, upto)
        n_col = jnp.sum(jnp.where(lane == tok, pre, 0.0), axis=1, keepdims=True)
        kn = knew_ref[...].astype(BF16)
        s = _dot_nt(qbd, kn) + n_col - pre[:, 0:SUBLANES]
        key = lax.broadcasted_iota(jnp.int32, s.shape, 1)
        qtok = lax.broadcasted_iota(jnp.int32, s.shape, 0) // heads
        s = jnp.where(key <= qtok, s, NEG_INF)
        m = jnp.max(s, axis=1, keepdims=True)
        p = jnp.exp(s - m)
        m_ref[...] = jnp.broadcast_to(m, m_ref.shape)
        l_ref[...] = jnp.broadcast_to(jnp.sum(p, axis=1, keepdims=True), l_ref.shape)
        acc_ref[...] = _dot(p.astype(BF16), vnew_ref[...].astype(BF16))
        carry_ref[...] = jnp.broadcast_to(n_col, carry_ref.shape)

    src = lax.broadcasted_iota(jnp.int32, (PAGE_SIZE, PAGE_SIZE), 0)
    dst = lax.broadcasted_iota(jnp.int32, (PAGE_SIZE, PAGE_SIZE), 1)
    after = jnp.where(src > dst, 1.0, 0.0).astype(BF16)
    reps = n_rows // heads
    carry = carry_ref[...]
    s_parts = []
    for t in range(pages_per_step):
        lf = jnp.tile(lf_refs[t][...], (reps, 1))
        hi, mid, lo = _split3(lf)
        within = _dot(hi, after) + _dot(mid, after) + _dot(lo, after)
        bias = carry + within
        s_parts.append(_dot_nt(qbd, k_refs[t][...].astype(BF16)) + bias)
        carry = bias[:, 0:1] + lf[:, 0:1]
        carry = jnp.broadcast_to(carry, carry_ref.shape)
    carry_ref[...] = carry
    s = jnp.concatenate(s_parts, axis=1)
    m_prev = m_ref[...]
    m_new = jnp.maximum(m_prev, jnp.max(s, axis=1, keepdims=True))
    p = jnp.exp(s - jnp.tile(m_new, (1, pages_per_step)))
    scale = jnp.exp(m_prev - m_new)
    m_ref[...] = m_new
    l_ref[...] = scale * l_ref[...] + jnp.sum(p, axis=1, keepdims=True)
    pb = p.astype(BF16)
    pv = _dot(pb[:, 0:PAGE_SIZE], v_refs[0][...].astype(BF16))
    for t in range(1, pages_per_step):
        pv = pv + _dot(pb[:, t * PAGE_SIZE:(t + 1) * PAGE_SIZE], v_refs[t][...].astype(BF16))
    acc_ref[...] = jnp.tile(scale, (1, acc_ref.shape[1] // LANES)) * acc_ref[...] + pv

    @pl.when(step == pl.num_programs(1) - 1)
    def _():
        full = acc_ref[...] / jnp.tile(l_ref[...], (1, acc_ref.shape[1] // LANES))
        row_head = lax.broadcasted_iota(jnp.int32, full.shape, 0) % heads
        lane_head = lax.broadcasted_iota(jnp.int32, full.shape, 1) // D_HEAD
        own = jnp.where(row_head == lane_head, full, 0.0)
        o_ref[...] = jnp.sum(own.reshape(n_new, heads, full.shape[1]), axis=1).astype(o_ref.dtype)


def _fox_sample(qbd, knew, vnew, lnew, cache_k, cache_v, cache_lft, page_table, *, heads, n_new,
                pages_per_step):
    nb, n_rows, width = qbd.shape
    n_pages = page_table.shape[1]
    assert n_pages % pages_per_step == 0
    pt = page_table.reshape(-1)

    def page_map(t):
        def index(b, s, pt_ref):
            logical = n_pages - 1 - (s * pages_per_step + t)
            return (pt_ref[b * n_pages + logical], 0, 0)
        return index

    batch3 = lambda shape: pl.BlockSpec((None,) + shape, lambda b, s, pt_ref: (b, 0, 0))
    in_specs = [batch3((n_rows, width)), batch3((SUBLANES, width)), batch3((SUBLANES, width)),
                batch3((n_rows, LANES))]
    in_specs += [pl.BlockSpec((None, PAGE_SIZE, width), page_map(t)) for t in range(pages_per_step)]
    in_specs += [pl.BlockSpec((None, PAGE_SIZE, width), page_map(t)) for t in range(pages_per_step)]
    in_specs += [pl.BlockSpec((None, heads, PAGE_SIZE), page_map(t)) for t in range(pages_per_step)]
    return pl.pallas_call(
        functools.partial(_fox_sample_kernel, pages_per_step=pages_per_step, n_new=n_new, heads=heads),
        out_shape=jax.ShapeDtypeStruct((nb, n_new, width), BF16),
        grid_spec=pltpu.PrefetchScalarGridSpec(
            num_scalar_prefetch=1,
            grid=(nb, n_pages // pages_per_step),
            in_specs=in_specs,
            out_specs=batch3((n_new, width)),
            scratch_shapes=[pltpu.VMEM((n_rows, LANES), F32), pltpu.VMEM((n_rows, LANES), F32),
                            pltpu.VMEM((n_rows, width), F32), pltpu.VMEM((n_rows, LANES), F32)],
        ),
        compiler_params=_params("parallel", "arbitrary"),
        name="fox_sample",
    )(pt, qbd, knew, vnew, lnew, *([cache_k] * pages_per_step), *([cache_v] * pages_per_step),
      *([cache_lft] * pages_per_step))


def _odd_proj_kernel(*refs, tm):
    n_groups = len(DIL_CONFIGS)
    y_ref, w_ref = refs[0], refs[1]
    de_refs = refs[2:2 + 3 * n_groups]
    tail_refs = refs[2 + 3 * n_groups:2 + 5 * n_groups]
    stage_refs = refs[2 + 5 * n_groups:]
    yb = y_ref[...].astype(BF16)
    n_lane_groups = de_refs[0].shape[1]
    gw = n_lane_groups * LANES
    stage = 0
    for g, (window, dil) in enumerate(DIL_CONFIGS):
        for part in range(3):
            col = (g * 3 + part) * gw
            res = _dot(yb, w_ref[:, col:col + gw])
            if part == 0:
                res = res * (D_HEAD ** -0.5)
            dst = de_refs[g * 3 + part]
            if dil == 1:
                for a in range(n_lane_groups):
                    dst[0, a] = res[:, a * LANES:(a + 1) * LANES].astype(BF16)
            else:
                s_ref = stage_refs[stage]
                stage += 1
                for a in range(n_lane_groups):
                    s_ref[a] = res[:, a * LANES:(a + 1) * LANES]
                for r in range(dil):
                    for a in range(n_lane_groups):
                        dst[r, a] = s_ref[a, pl.ds(r, tm // dil, stride=dil), :].astype(BF16)
            if part > 0:
                keep = min(window, tm)
                tail_refs[g * 2 + part - 1][...] = res[tm - keep:, :]


def _odd_proj(y3, w, *, tm):
    batch, seq_len, d = y3.shape
    gw = w.shape[1] // (3 * len(DIL_CONFIGS))
    n_lane_groups = gw // LANES
    n_tiles = seq_len // tm
    out_shape, out_specs, n_stage = [], [], 0
    for window, dil in DIL_CONFIGS:
        assert tm % (dil * 16) == 0 and (window % tm == 0 or tm % window == 0)
        for _ in range(3):
            out_shape.append(jax.ShapeDtypeStruct((batch, dil, n_lane_groups, seq_len // dil, LANES), BF16))
            out_specs.append(pl.BlockSpec((None, dil, n_lane_groups, tm // dil, LANES),
                                          lambda b, i: (b, 0, 0, i, 0)))
        n_stage += 3 if dil > 1 else 0
    for window, dil in DIL_CONFIGS:
        keep = min(window, tm)
        first = n_tiles - window // keep
        for _ in range(2):
            out_shape.append(jax.ShapeDtypeStruct((batch, window, gw), F32))
            out_specs.append(pl.BlockSpec((None, keep, gw),
                                          lambda b, i, first=first: (b, jnp.maximum(i - first, 0), 0)))
    return pl.pallas_call(
        functools.partial(_odd_proj_kernel, tm=tm),
        out_shape=tuple(out_shape),
        grid=(batch, n_tiles),
        in_specs=[pl.BlockSpec((None, tm, d), lambda b, i: (b, i, 0)),
                  pl.BlockSpec(w.shape, lambda b, i: (0, 0), pipeline_mode=pl.Buffered(1))],
        out_specs=tuple(out_specs),
        scratch_shapes=[pltpu.VMEM((n_lane_groups, tm, LANES), F32)] * n_stage,
        compiler_params=_params("parallel", "arbitrary"),
        name="odd_proj",
    )(y3, w)


def _dil_attn_kernel(q_ref, kp_ref, kc_ref, vp_ref, vc_ref, bp_ref, bc_ref, o_ref, lse_ref, *, dil, heads):
    qb = q_ref.shape[2]
    even_lane = _lane_is_even_head((qb, LANES))

    def one_class(r, carry):
        for g in range(heads // HEADS_PER_LANE_GROUP):
            q2 = q_ref[r, g]
            kp, kc = kp_ref[r, g], kc_ref[r, g]
            vp, vc = vp_ref[r, g], vc_ref[r, g]
            outs, lses = [], []
            for sub in range(HEADS_PER_LANE_GROUP):
                h = g * HEADS_PER_LANE_GROUP + sub
                qh = jnp.where(even_lane if sub == 0 else ~even_lane, q2, jnp.zeros_like(q2))
                sp = _dot_nt(qh, kp) + bp_ref[h]
                sc = _dot_nt(qh, kc) + bc_ref[h]
                m = jnp.maximum(jnp.max(sp, axis=1, keepdims=True), jnp.max(sc, axis=1, keepdims=True))
                pp = jnp.exp(sp - m)
                pc = jnp.exp(sc - m)
                den = jnp.sum(pp, axis=1, keepdims=True) + jnp.sum(pc, axis=1, keepdims=True)
                pv = _dot(pp.astype(BF16), vp) + _dot(pc.astype(BF16), vc)
                outs.append(pv / den)
                lses.append(jnp.broadcast_to(m + jnp.log(den), (qb, LANES)))
            o2 = jnp.where(even_lane, outs[0], outs[1])
            l2 = jnp.where(even_lane, lses[0], lses[1])
            if dil == 1:
                o_ref[g] = o2
                lse_ref[g] = l2
            else:
                o_ref[g, pl.ds(r, qb, stride=dil), :] = o2
                lse_ref[g, pl.ds(r, qb, stride=dil), :] = l2
        return carry

    if dil == 1:
        one_class(0, 0)
    else:
        lax.fori_loop(0, dil, one_class, 0)


def _dil_attention(q, k, v, bias_prev, bias_cur, *, heads, qb):
    batch, dil, n_lane_groups, slots, _ = q.shape
    blk = (None, dil, n_lane_groups, qb, LANES)
    cur = pl.BlockSpec(blk, lambda b, i: (b, 0, 0, i, 0))
    prev = pl.BlockSpec(blk, lambda b, i: (b, 0, 0, jnp.maximum(i - 1, 0), 0))
    nat = pl.BlockSpec((None, n_lane_groups, qb * dil, LANES), lambda b, i: (b, 0, i, 0))
    out_sds = jax.ShapeDtypeStruct((batch, n_lane_groups, slots * dil, LANES), F32)
    return pl.pallas_call(
        functools.partial(_dil_attn_kernel, dil=dil, heads=heads),
        out_shape=(out_sds, out_sds),
        grid=(batch, slots // qb),
        in_specs=[cur, prev, cur, prev, cur,
                  pl.BlockSpec((None, heads, qb, qb), lambda b, i: (jnp.minimum(i, 1), 0, 0, 0)),
                  _const_spec(bias_cur.shape)],
        out_specs=(nat, nat),
        compiler_params=_params("parallel", "arbitrary"),
        name=f"dil_attention_d{dil}",
    )(q, k, k, v, v, bias_prev, bias_cur)


def _odd_mix_kernel(*refs, alpha):
    n = len(DIL_CONFIGS)
    o_refs, lse_refs = refs[:n], refs[n:2 * n]
    y_ref, wo_ref, g_ref, b_ref, out_ref = refs[2 * n:]
    merged = []
    for a in range(o_refs[0].shape[0]):
        lses = [r[a] for r in lse_refs]
        top = functools.reduce(jnp.maximum, lses)
        wts = [jnp.exp(l - top) for l in lses]
        num = functools.reduce(jnp.add, [w * r[a] for w, r in zip(wts, o_refs)])
        merged.append((num / functools.reduce(jnp.add, wts)).astype(BF16))
    mix = _dot(jnp.concatenate(merged, axis=-1), wo_ref[...])
    out_ref[...] = _layer_norm(alpha * y_ref[...] + mix, g_ref[...], b_ref[...])


def _odd_mix(outs, lses, y3, wo, g, b, *, alpha, tm):
    batch, seq_len, d = y3.shape
    n_lane_groups = outs[0].shape[1]
    part = pl.BlockSpec((None, n_lane_groups, tm, LANES), lambda b, i: (b, 0, i, 0))
    row = pl.BlockSpec((None, tm, d), lambda b, i: (b, i, 0))
    return pl.pallas_call(
        functools.partial(_odd_mix_kernel, alpha=alpha),
        out_shape=jax.ShapeDtypeStruct((batch, seq_len, d), F32),
        grid=(batch, seq_len // tm),
        in_specs=[part] * (2 * len(outs)) + [row, _const_spec(wo.shape), _const_spec(g.shape),
                                             _const_spec(b.shape)],
        out_specs=row,
        compiler_params=_params("parallel", "parallel"),
        name="odd_mix",
    )(*outs, *lses, y3, wo, g, b)


def _proj_kernel(x_ref, w_ref, o_ref):
    o_ref[...] = _dot(x_ref[...].astype(BF16), w_ref[...])


def _proj(x, w, *, tn):
    rows, d = x.shape
    n = w.shape[1]
    return pl.pallas_call(
        _proj_kernel,
        out_shape=jax.ShapeDtypeStruct((rows, n), F32),
        grid=(n // tn,),
        in_specs=[_const_spec((rows, d)), pl.BlockSpec((d, tn), lambda j: (0, j))],
        out_specs=pl.BlockSpec((rows, tn), lambda j: (0, j)),
        compiler_params=_params("parallel"),
        name="proj",
    )(x, w)


def _dil_sample_kernel(qbd_ref, kc_ref, vc_ref, kn_ref, vn_ref, tabc_ref, tabn_ref,
                       o_ref, lse_ref, ko_ref, vo_ref, *stage_refs, dil, n_new, keys):
    buf_len, width = kc_ref.shape
    heads = qbd_ref.shape[1]
    n_lane_groups = width // LANES
    kn = kn_ref[...].astype(BF16)
    vn = vn_ref[...].astype(BF16)
    own = (lax.broadcasted_iota(jnp.int32, (heads, width), 0)
           == lax.broadcasted_iota(jnp.int32, (heads, width), 1) // D_HEAD)
    chunk = 256
    if dil > 1:
        for src_ref, dst_ref in zip((kc_ref, vc_ref), stage_refs):
            for c0 in range(0, buf_len, chunk):
                for a in range(n_lane_groups):
                    dst_ref[a, c0:c0 + chunk, :] = src_ref[c0:c0 + chunk, a * LANES:(a + 1) * LANES]

    def window_rows(src_ref, stage_ref, i):
        if dil == 1:
            return src_ref[0:keys, :]
        return jnp.concatenate([stage_ref[a, pl.ds(i, keys, stride=dil), :] for a in range(n_lane_groups)],
                               axis=-1)

    for i in range(n_new):
        kc = window_rows(kc_ref, stage_refs[0] if dil > 1 else None, i).astype(BF16)
        vc = window_rows(vc_ref, stage_refs[1] if dil > 1 else None, i).astype(BF16)
        q8 = qbd_ref[i]
        sc = _dot_nt(q8, kc) + tabc_ref[i]
        sn = _dot_nt(q8, kn) + tabn_ref[i]
        m = jnp.maximum(jnp.max(sc, axis=1, keepdims=True), jnp.max(sn, axis=1, keepdims=True))
        pc = jnp.exp(sc - m)
        pn = jnp.exp(sn - m)
        den = jnp.sum(pc, axis=1, keepdims=True) + jnp.sum(pn, axis=1, keepdims=True)
        o8 = (_dot(pc.astype(BF16), vc) + _dot(pn.astype(BF16), vn)) / den
        lse8 = jnp.broadcast_to(m + jnp.log(den), (heads, width))
        o_ref[i:i + 1, :] = jnp.sum(jnp.where(own, o8, 0.0), axis=0, keepdims=True)
        lse_ref[i:i + 1, :] = jnp.sum(jnp.where(own, lse8, 0.0), axis=0, keepdims=True)
    for src_ref, new_ref, dst_ref in ((kc_ref, kn_ref, ko_ref), (vc_ref, vn_ref, vo_ref)):
        for c0 in range(0, buf_len - n_new, chunk):
            size = min(chunk, buf_len - n_new - c0)
            dst_ref[c0:c0 + size, :] = src_ref[c0 + n_new:c0 + n_new + size, :]
        dst_ref[buf_len - n_new:buf_len, :] = new_ref[0:n_new, :]


def _dil_sample(qbd, kc, vc, kn, vn, tab_c, tab_n, *, dil, n_new):
    nb, buf_len, width = kc.shape
    heads = qbd.shape[2]
    keys = tab_c.shape[-1]
    b3 = lambda *shape: pl.BlockSpec((None,) + shape, lambda b: (b,) + (0,) * len(shape))
    small = jax.ShapeDtypeStruct((nb, n_new, width), F32)
    big = jax.ShapeDtypeStruct((nb, buf_len, width), F32)
    return pl.pallas_call(
        functools.partial(_dil_sample_kernel, dil=dil, n_new=n_new, keys=keys),
        out_shape=(small, small, big, big),
        grid=(nb,),
        in_specs=[b3(n_new, heads, width), b3(buf_len, width), b3(buf_len, width),
                  b3(SUBLANES, width), b3(SUBLANES, width), _const_spec(tab_c.shape), _const_spec(tab_n.shape)],
        out_specs=(b3(n_new, width), b3(n_new, width), b3(buf_len, width), b3(buf_len, width)),
        scratch_shapes=[pltpu.VMEM((width // LANES, buf_len, LANES), F32)] * (2 if dil > 1 else 0),
        compiler_params=_params("parallel"),
        name=f"dil_sample_d{dil}",
    )(qbd, kc, vc, kn, vn, tab_c, tab_n)


def _rel_bucket(dist):
    exact = REL_BUCKETS // 2
    d = jnp.maximum(dist, 1).astype(F32)
    large = exact + (jnp.log(d / exact) / math.log(REL_MAX_DIST / exact) * (REL_BUCKETS - exact)).astype(jnp.int32)
    large = jnp.minimum(large, REL_BUCKETS - 1)
    return jnp.where(dist < exact, dist, large)


def _group_bias(rel_bias, g, window, dil, heads):
    dist = jnp.arange(window // dil + 1) * dil
    return rel_bias[_rel_bucket(dist), g * heads:(g + 1) * heads].T.astype(F32)


def _lookup(bias, idx, valid):
    idx = np.clip(idx, 0, bias.shape[1] - 1)
    return jnp.where(jnp.asarray(valid), jnp.take(bias, jnp.asarray(idx), axis=1), NEG_INF)


def _prompt_bias_tiles(bias, qb):
    qi = np.arange(qb)[:, None]
    kj = np.arange(qb)[None, :]
    n_back = bias.shape[1] - 1
    prev = _lookup(bias, qi + qb - kj, qi + qb - kj <= n_back)
    cur = _lookup(bias, qi - kj, qi - kj >= 0)
    return jnp.stack([jnp.full_like(prev, NEG_INF), prev]), cur


def _sample_bias_tables(bias, dil, n_new):
    n_back = bias.shape[1] - 1
    i = np.arange(n_new)[:, None]
    row = np.arange(n_back)[None, :]
    new = np.arange(SUBLANES)[None, :]
    if dil == 1:
        tab_c = _lookup(bias, n_back + i - row, row >= i)
        tab_n = _lookup(bias, i - new, new <= i)
    else:
        tab_c = _lookup(bias, n_back - row + 0 * i, np.ones((n_new, n_back), bool))
        tab_n = _lookup(bias, 0 * (i - new), new == i)
    return jnp.swapaxes(tab_c, 0, 1), jnp.swapaxes(tab_n, 0, 1)


PROMPT_ROW_TILE = 512
ATTN_BLOCK = 512
DIL_SLOT_BLOCK = 128
CUMSUM_BLOCK = 512
PAGES_PER_STEP = 8


def _pad_axis(x, axis, size):
    pad = [(0, 0)] * x.ndim
    pad[axis] = (0, size - x.shape[axis])
    return jnp.pad(x, pad)


def kernel(x_prompt, x_sample, cache_fox_k, cache_fox_v, cache_fox_logf, state_pool, cache_dil0_k, cache_dil0_v, cache_dil1_k, cache_dil1_v, cache_dil2_k, cache_dil2_v, page_table, p_prompt, p_sample, w_in_even, b_fgate, pool_w, pool_scale, w_out_even, w_in_odd, w_out_odd, rel_bias, ffn1_wg, ffn1_wu, ffn1_wd, ffn2_wg, ffn2_wu, ffn2_wd, ln_g, ln_b, ple_wg, ple_bg, ple_wp):
    depth = ffn1_wg.shape[0]
    alpha = (2 * depth) ** 0.25
    batch, seq_len, d_model = x_prompt.shape
    nb, n_new, _ = x_sample.shape
    past_len = page_table.shape[1] * PAGE_SIZE
    fox_heads = cache_fox_k.shape[-2]
    fox_w = fox_heads * D_HEAD
    dil_heads = cache_dil0_k.shape[-2]
    dil_w = dil_heads * D_HEAD
    dil_caches_k = (cache_dil0_k, cache_dil1_k, cache_dil2_k)
    dil_caches_v = (cache_dil0_v, cache_dil1_v, cache_dil2_v)
    rows_p, rows_s = batch * seq_len, nb * n_new
    tm = PROMPT_ROW_TILE
    bf = lambda w: w.astype(BF16)
    vec = lambda a: a.reshape(1, -1)

    yp = x_prompt.reshape(rows_p, d_model)
    ys = x_sample.reshape(rows_s, d_model)
    outs = {name: [] for name in ("fkp", "fvp", "flp", "fks", "fvs", "fls", "plp", "pls")}
    dkp, dvp, dks, dvs = ([[] for _ in DIL_CONFIGS] for _ in range(4))

    for i in range(depth):
        w1 = (bf(ffn1_wg[i]), bf(ffn1_wu[i]), bf(ffn1_wd[i]), vec(ln_g[i, 0]), vec(ln_b[i, 0]))
        yp = _ffn(yp, *w1, alpha=alpha, tm=tm)
        ys = _ffn(ys, *w1, alpha=alpha, tm=rows_s)
        g1, b1 = vec(ln_g[i, 1]), vec(ln_b[i, 1])
        if i % 2 == 0:
            e = i // 2
            w_in = w_in_even[e]
            wqkv = bf(w_in[:, :3 * fox_w])
            wf = bf(_pad_axis(w_in[:, 3 * fox_w:3 * fox_w + fox_heads], 1, LANES))
            bfg = _pad_axis(vec(b_fgate[e]), 1, LANES)
            wu = bf(w_in[:, 3 * fox_w + fox_heads:])
            wpool, pscale, wo = bf(pool_w[e]), vec(pool_scale[e]), bf(w_out_even[e])
            qb, k, v, kb, vb, lf, lft, u = _even_proj(yp, wqkv, wf, bfg, wu, heads=fox_heads, tm=tm)
            ct = _cumsum_lanes(lft, seq_len=seq_len, tc=CUMSUM_BLOCK)
            att = _fox_attention(qb, kb, vb, ct.T, ct, batch=batch, seq_len=seq_len, heads=fox_heads,
                                 tq=ATTN_BLOCK)
            yp_next = _even_mix(att, u, yp, wpool, pscale, wo, g1, b1, alpha=alpha, seq_len=seq_len, tm=tm)
            outs["fkp"].append(k.reshape(batch, seq_len, fox_heads, D_HEAD))
            outs["fvp"].append(v.reshape(batch, seq_len, fox_heads, D_HEAD))
            outs["flp"].append(lf.reshape(batch, seq_len, fox_heads))
            outs["plp"].append(u.reshape(batch, seq_len, -1)[:, seq_len - POOL_BUF:])
            qs, ks, vs, _, _, lfs, _, us = _even_proj(ys, wqkv, wf, bfg, wu, heads=fox_heads, tm=rows_s)
            head_mask = (jnp.arange(fox_w)[None, :] // D_HEAD == jnp.arange(fox_heads)[:, None]).astype(BF16)
            qbd = (qs.reshape(nb, n_new, 1, fox_w) * head_mask).reshape(nb, n_new * fox_heads, fox_w)
            knew = _pad_axis(ks.reshape(nb, n_new, fox_w), 1, SUBLANES)
            vnew = _pad_axis(vs.reshape(nb, n_new, fox_w), 1, SUBLANES)
            lf3 = jnp.swapaxes(lfs.reshape(nb, n_new, fox_heads), 1, 2)
            lnew = jnp.broadcast_to(lf3[:, None], (nb, n_new, fox_heads, n_new)).reshape(nb, n_new * fox_heads, n_new)
            lnew = _pad_axis(lnew, 2, LANES)
            n_phys = cache_fox_k.shape[1]
            att_s = _fox_sample(qbd, knew, vnew, lnew,
                                cache_fox_k[e].reshape(n_phys, PAGE_SIZE, fox_w),
                                cache_fox_v[e].reshape(n_phys, PAGE_SIZE, fox_w),
                                jnp.swapaxes(cache_fox_logf[e], 1, 2), page_table,
                                heads=fox_heads, n_new=n_new, pages_per_step=PAGES_PER_STEP)
            ctx_s = jnp.concatenate([state_pool[e].astype(F32), us.reshape(nb, n_new, -1)], axis=1)
            pool_s = _pool_sample(jnp.swapaxes(ctx_s, 0, 1), wpool, pscale, pos0=past_len, n_new=n_new)
            pool_s = jnp.swapaxes(pool_s, 0, 1).reshape(rows_s, -1)
            ys_next = _pair_mix(att_s.reshape(rows_s, fox_w), pool_s, ys, wo, g1, b1, alpha=alpha, tm=rows_s)
            outs["fks"].append(ks.reshape(nb, n_new, fox_heads, D_HEAD))
            outs["fvs"].append(vs.reshape(nb, n_new, fox_heads, D_HEAD))
            outs["fls"].append(lfs.reshape(nb, n_new, fox_heads))
            outs["pls"].append(ctx_s[:, -POOL_BUF:])
        else:
            o = i // 2
            w_in, wo = bf(w_in_odd[o]), bf(w_out_odd[o])
            biases = [_group_bias(rel_bias, g, window, dil, dil_heads) for g, (window, dil) in enumerate(DIL_CONFIGS)]
            proj = _odd_proj(yp.reshape(batch, seq_len, d_model), w_in, tm=tm)
            n_groups = len(DIL_CONFIGS)
            o_parts, lse_parts = [], []
            for g, (window, dil) in enumerate(DIL_CONFIGS):
                qd, kd, vd = proj[3 * g:3 * g + 3]
                bias_prev, bias_cur = _prompt_bias_tiles(biases[g], DIL_SLOT_BLOCK)
                og, lg = _dil_attention(qd, kd, vd, bias_prev, bias_cur, heads=dil_heads, qb=DIL_SLOT_BLOCK)
                o_parts.append(og)
                lse_parts.append(lg)
                keep = min(window, seq_len)
                dkp[g].append(proj[3 * n_groups + 2 * g].reshape(batch, keep, dil_heads, D_HEAD))
                dvp[g].append(proj[3 * n_groups + 2 * g + 1].reshape(batch, keep, dil_heads, D_HEAD))
            yp_next = _odd_mix(o_parts, lse_parts, yp.reshape(batch, seq_len, d_model), wo, g1, b1,
                               alpha=alpha, tm=tm).reshape(rows_p, d_model)
            lane_major = lambda a: jnp.swapaxes(a.reshape(rows_s, dil_w // LANES, LANES), 0, 1)[None]
            proj_s = _proj(ys, w_in, tn=3 * dil_w).reshape(nb, n_new, n_groups, 3, dil_w)
            head_mask = (jnp.arange(dil_w)[None, :] // D_HEAD == jnp.arange(dil_heads)[:, None]).astype(F32)
            o_parts, lse_parts = [], []
            for g, (window, dil) in enumerate(DIL_CONFIGS):
                qg = proj_s[:, :, g, 0] * (D_HEAD ** -0.5)
                qbd = (qg[:, :, None, :] * head_mask).astype(BF16)
                kn = _pad_axis(proj_s[:, :, g, 1], 1, SUBLANES)
                vn = _pad_axis(proj_s[:, :, g, 2], 1, SUBLANES)
                buf_len = dil_caches_k[g].shape[2]
                tab_c, tab_n = _sample_bias_tables(biases[g], dil, n_new)
                og, lg, k_roll, v_roll = _dil_sample(
                    qbd, dil_caches_k[g][o].reshape(nb, buf_len, dil_w),
                    dil_caches_v[g][o].reshape(nb, buf_len, dil_w), kn, vn, tab_c, tab_n, dil=dil, n_new=n_new)
                o_parts.append(lane_major(og))
                lse_parts.append(lane_major(lg))
                dks[g].append(k_roll.reshape(nb, buf_len, dil_heads, D_HEAD))
                dvs[g].append(v_roll.reshape(nb, buf_len, dil_heads, D_HEAD))
            ys_next = _odd_mix(o_parts, lse_parts, ys[None], wo, g1, b1, alpha=alpha,
                               tm=rows_s).reshape(rows_s, d_model)
        yp, ys = yp_next, ys_next
        w2 = (bf(ffn2_wg[i]), bf(ffn2_wu[i]), bf(ffn2_wd[i]), vec(ln_g[i, 2]), vec(ln_b[i, 2]))
        ple_w = (bf(ple_wg[i]), vec(ple_bg[i]), bf(ple_wp[i]))
        yp = _ffn(yp, *w2, alpha=alpha, tm=tm, ple=(p_prompt[i].reshape(rows_p, -1),) + ple_w)
        ys = _ffn(ys, *w2, alpha=alpha, tm=rows_s, ple=(p_sample[i].reshape(rows_s, -1),) + ple_w)

    stack = lambda parts: jnp.stack(parts)
    result = [yp.reshape(batch, seq_len, d_model), ys.reshape(nb, n_new, d_model)]
    result += [stack(outs[name]) for name in ("fkp", "fvp", "flp", "fks", "fvs", "fls", "plp", "pls")]
    for g in range(len(DIL_CONFIGS)):
        result += [stack(dkp[g]), stack(dvp[g])]
    for g in range(len(DIL_CONFIGS)):
        result += [stack(dks[g]), stack(dvs[g])]
    return tuple(result)
```

```python
import functools
import math

import jax
import jax.numpy as jnp
import numpy as np
from jax import lax
from jax.experimental import pallas as pl
from jax.experimental.pallas import tpu as pltpu

F32 = jnp.float32
BF16 = jnp.bfloat16

D_HEAD = 64
POOL_WINDOWS = (2, 4, 8, 16)
POOL_BUF = max(POOL_WINDOWS) - 1
DIL_CONFIGS = ((128, 1), (512, 4), (2048, 16))
REL_BUCKETS = 32
REL_MAX_DIST = 2048
LN_EPS = 1e-5
NEG_INF = -1e30
PAGE_SIZE = 128

LANES = 128
SUBLANES = 8
VMEM_LIMIT_BYTES = 56 * 1024 * 1024

HEADS_PER_LANE_GROUP = LANES // D_HEAD
HALO_ROWS = 16


def _params(*semantics):
    return pltpu.CompilerParams(dimension_semantics=semantics, vmem_limit_bytes=VMEM_LIMIT_BYTES)


def _dot(a, b):
    return jnp.dot(a, b, preferred_element_type=F32)


def _dot_nt(a, b):
    return lax.dot_general(a, b, (((1,), (1,)), ((), ())), preferred_element_type=F32)


def _layer_norm(z, g, b):
    mu = jnp.mean(z, axis=-1, keepdims=True)
    zc = z - mu
    var = jnp.mean(zc * zc, axis=-1, keepdims=True)
    return zc * lax.rsqrt(var + LN_EPS) * g + b


def _log_sigmoid(x):
    return jnp.minimum(x, 0.0) - jnp.log1p(jnp.exp(-jnp.abs(x)))


def _split3(x):
    hi = x.astype(BF16)
    r1 = x - hi.astype(F32)
    mid = r1.astype(BF16)
    lo = (r1 - mid.astype(F32)).astype(BF16)
    return hi, mid, lo


def _lane_is_even_head(shape):
    return lax.broadcasted_iota(jnp.int32, shape, len(shape) - 1) % LANES < D_HEAD


def _const_spec(shape):
    zeros = (0,) * len(shape)
    return pl.BlockSpec(shape, lambda *_: zeros)


def _ffn_kernel(*refs, alpha, ff_chunk, with_ple):
    if with_ple:
        (x_ref, wg_ref, wu_ref, wd_ref, g_ref, b_ref, p_ref, pwg_ref, pbg_ref, pwp_ref,
         o_ref, acc_ref) = refs
    else:
        x_ref, wg_ref, wu_ref, wd_ref, g_ref, b_ref, o_ref, acc_ref = refs
    x = x_ref[...]
    xb = x.astype(BF16)
    d_ff = wg_ref.shape[1]
    for c in range(d_ff // ff_chunk):
        cols = slice(c * ff_chunk, (c + 1) * ff_chunk)
        gate = _dot(xb, wg_ref[:, cols])
        up = _dot(xb, wu_ref[:, cols])
        h = (gate * jax.nn.sigmoid(gate) * up).astype(BF16)
        part = _dot(h, wd_ref[cols, :])
        if c == 0:
            acc_ref[...] = part
        else:
            acc_ref[...] += part
    y = _layer_norm(alpha * x + 0.5 * acc_ref[...], g_ref[...], b_ref[...])
    if with_ple:
        gate = jax.nn.sigmoid(_dot(y.astype(BF16), pwg_ref[...]) + pbg_ref[...])
        y = y + gate * _dot(p_ref[...].astype(BF16), pwp_ref[...])
    o_ref[...] = y


def _ffn(x, wg, wu, wd, g, b, *, alpha, tm, ple=None):
    rows, d = x.shape
    d_ff = wg.shape[1]
    ff_chunk = 256
    assert rows % tm == 0 and d_ff % ff_chunk == 0
    row_spec = pl.BlockSpec((tm, d), lambda i: (i, 0))
    in_specs = [row_spec, _const_spec((d, d_ff)), _const_spec((d, d_ff)), _const_spec((d_ff, d)),
                _const_spec((1, d)), _const_spec((1, d))]
    args = [x, wg, wu, wd, g, b]
    if ple is not None:
        p, pwg, pbg, pwp = ple
        in_specs += [pl.BlockSpec((tm, p.shape[1]), lambda i: (i, 0)), _const_spec(pwg.shape),
                     _const_spec((1, d)), _const_spec(pwp.shape)]
        args += [p, pwg, pbg, pwp]
    return pl.pallas_call(
        functools.partial(_ffn_kernel, alpha=alpha, ff_chunk=ff_chunk, with_ple=ple is not None),
        out_shape=jax.ShapeDtypeStruct((rows, d), F32),
        grid=(rows // tm,),
        in_specs=in_specs,
        out_specs=row_spec,
        scratch_shapes=[pltpu.VMEM((tm, d), F32)],
        compiler_params=_params("parallel"),
        name="ffn_ple" if ple is not None else "ffn",
    )(*args)


def _even_proj_kernel(y_ref, wqkv_ref, wf_ref, bf_ref, wu_ref,
                      q_ref, kb_ref, vb_ref, k_ref, v_ref, logft_ref, u_ref, *, kv_transposed):
    yb = y_ref[...].astype(BF16)
    hw = q_ref.shape[1]
    heads = logft_ref.shape[0]
    q = _dot(yb, wqkv_ref[:, 0:hw])
    q_ref[...] = (q * (D_HEAD ** -0.5)).astype(BF16)
    k = _dot(yb, wqkv_ref[:, hw:2 * hw])
    kb_ref[...] = k.astype(BF16)
    k_ref[...] = k.T if kv_transposed else k
    v = _dot(yb, wqkv_ref[:, 2 * hw:3 * hw])
    vb_ref[...] = v.astype(BF16)
    v_ref[...] = v.T if kv_transposed else v
    logf = _log_sigmoid(_dot(yb, wf_ref[...]) + bf_ref[...])
    logft_ref[...] = logf.T[:heads, :]
    u_ref[...] = _dot(yb, wu_ref[...])


def _even_proj(y3, wqkv, wf, bf, wu, *, heads, tm, kv_transposed):
    batch, seq_len, d = y3.shape
    hw = wqkv.shape[1] // 3
    pw = wu.shape[1]
    row = lambda w: pl.BlockSpec((None, tm, w), lambda b, i: (b, i, 0))
    col = lambda h: pl.BlockSpec((None, h, tm), lambda b, i: (b, 0, i))
    nat = lambda w, dt: jax.ShapeDtypeStruct((batch, seq_len, w), dt)
    kv_sds = jax.ShapeDtypeStruct((batch, hw, seq_len), F32) if kv_transposed else nat(hw, F32)
    kv_spec = col(hw) if kv_transposed else row(hw)
    return pl.pallas_call(
        functools.partial(_even_proj_kernel, kv_transposed=kv_transposed),
        out_shape=(nat(hw, BF16), nat(hw, BF16), nat(hw, BF16), kv_sds, kv_sds,
                   jax.ShapeDtypeStruct((batch, heads, seq_len), F32), nat(pw, F32)),
        grid=(batch, seq_len // tm),
        in_specs=[row(d), _const_spec(wqkv.shape), _const_spec(wf.shape), _const_spec(bf.shape),
                  _const_spec(wu.shape)],
        out_specs=(row(hw), row(hw), row(hw), kv_spec, kv_spec, col(heads), row(pw)),
        compiler_params=_params("parallel", "parallel"),
        name="even_proj",
    )(y3, wqkv, wf, bf, wu)


def _cumsum_kernel(x_ref, o_ref, carry_ref):
    @pl.when(pl.program_id(1) == 0)
    def _():
        carry_ref[...] = jnp.zeros_like(carry_ref)

    x = x_ref[...]
    tc = x.shape[1]
    src = lax.broadcasted_iota(jnp.int32, (tc, tc), 0)
    dst = lax.broadcasted_iota(jnp.int32, (tc, tc), 1)
    tri = jnp.where(src <= dst, 1.0, 0.0).astype(BF16)
    hi, mid, lo = _split3(x)
    y = _dot(hi, tri) + _dot(mid, tri) + _dot(lo, tri) + carry_ref[:, 0:1]
    o_ref[...] = y
    carry_ref[...] = jnp.broadcast_to(y[:, tc - 1:tc], carry_ref.shape)


def _cumsum_lanes(x, *, tc):
    batch, heads, seq_len = x.shape
    assert seq_len % tc == 0
    spec = pl.BlockSpec((None, heads, tc), lambda b, i: (b, 0, i))
    return pl.pallas_call(
        _cumsum_kernel,
        out_shape=jax.ShapeDtypeStruct(x.shape, F32),
        grid=(batch, seq_len // tc),
        in_specs=[spec],
        out_specs=spec,
        scratch_shapes=[pltpu.VMEM((heads, LANES), F32)],
        compiler_params=_params("parallel", "arbitrary"),
        name="logf_cumsum",
    )(x)


def _fox_attn_kernel(qi_ref, ki_ref, q_ref, k_ref, v_ref, cq_ref, ck_ref, o_ref,
                     m_ref, l_ref, acc_ref, *, heads):
    pair = pl.program_id(1)
    qi = qi_ref[pair]
    ki = ki_ref[pair]
    tq = q_ref.shape[0]
    tk = k_ref.shape[0]

    @pl.when(ki == 0)
    def _():
        m_ref[...] = jnp.full_like(m_ref, NEG_INF)
        l_ref[...] = jnp.zeros_like(l_ref)
        acc_ref[...] = jnp.zeros_like(acc_ref)

    even_lane = _lane_is_even_head((tq, LANES))

    def step(masked):
        if masked:
            causal = (lax.broadcasted_iota(jnp.int32, (tq, tk), 1)
                      <= lax.broadcasted_iota(jnp.int32, (tq, tk), 0))
        for g in range(heads // HEADS_PER_LANE_GROUP):
            lanes = slice(g * LANES, (g + 1) * LANES)
            q2 = q_ref[:, lanes]
            k2 = k_ref[:, lanes]
            v2 = v_ref[:, lanes]
            scale_parts, sum_parts, pv_parts = [], [], []
            for sub in range(HEADS_PER_LANE_GROUP):
                h = g * HEADS_PER_LANE_GROUP + sub
                qh = jnp.where(even_lane if sub == 0 else ~even_lane, q2, jnp.zeros_like(q2))
                s = _dot_nt(qh, k2) + cq_ref[:, h:h + 1] - ck_ref[h:h + 1, :]
                if masked:
                    s = jnp.where(causal, s, NEG_INF)
                m_prev = m_ref[h]
                m_new = jnp.maximum(m_prev, jnp.max(s, axis=1, keepdims=True))
                p = jnp.exp(s - jnp.tile(m_new, (1, tk // LANES)))
                m_ref[h] = m_new
                scale_parts.append(jnp.exp(m_prev - m_new))
                sum_parts.append(jnp.sum(p, axis=1, keepdims=True))
                pv_parts.append(_dot(p.astype(BF16), v2))
            scale = jnp.where(even_lane, scale_parts[0], scale_parts[1])
            l_ref[:, lanes] = scale * l_ref[:, lanes] + jnp.where(even_lane, sum_parts[0], sum_parts[1])
            acc_ref[:, lanes] = scale * acc_ref[:, lanes] + jnp.where(even_lane, pv_parts[0], pv_parts[1])

    @pl.when(ki < qi)
    def _():
        step(masked=False)

    @pl.when(ki == qi)
    def _():
        step(masked=True)
        o_ref[...] = (acc_ref[...] / l_ref[...]).astype(o_ref.dtype)


def _fox_attention(q, k, v, c, ct, *, tq):
    batch, seq_len, width = q.shape
    heads = ct.shape[1]
    nq = seq_len // tq
    qi_tab = np.concatenate([np.full(i + 1, i) for i in range(nq)]).astype(np.int32)
    ki_tab = np.concatenate([np.arange(i + 1) for i in range(nq)]).astype(np.int32)
    q_map = lambda b, p, qi, ki: (b, qi[p], 0)
    k_map = lambda b, p, qi, ki: (b, ki[p], 0)
    return pl.pallas_call(
        functools.partial(_fox_attn_kernel, heads=heads),
        out_shape=jax.ShapeDtypeStruct(q.shape, BF16),
        grid_spec=pltpu.PrefetchScalarGridSpec(
            num_scalar_prefetch=2,
            grid=(batch, len(qi_tab)),
            in_specs=[pl.BlockSpec((None, tq, width), q_map), pl.BlockSpec((None, tq, width), k_map),
                      pl.BlockSpec((None, tq, width), k_map), pl.BlockSpec((None, tq, heads), q_map),
                      pl.BlockSpec((None, heads, tq), lambda b, p, qi, ki: (b, 0, ki[p]))],
            out_specs=pl.BlockSpec((None, tq, width), q_map),
            scratch_shapes=[pltpu.VMEM((heads, tq, LANES), F32), pltpu.VMEM((tq, width), F32),
                            pltpu.VMEM((tq, width), F32)],
        ),
        compiler_params=_params("parallel", "arbitrary"),
        name="fox_attention",
    )(jnp.asarray(qi_tab), jnp.asarray(ki_tab), q, k, v, c, ct)


def _pool_groups(ctx_ref, n_avail, w_pool_ref, scale_ref, rows, halo):
    gw = w_pool_ref.shape[1]
    outs = []
    for g, w in enumerate(POOL_WINDOWS):
        lanes = slice(g * gw, (g + 1) * gw)
        cur = ctx_ref[pl.ds(halo, rows), lanes]
        win = cur
        for j in range(1, w):
            win = win + ctx_ref[pl.ds(halo - j, rows), lanes]
        mean = win / jnp.minimum(float(w), n_avail)
        outs.append(_dot((mean - cur).astype(BF16), w_pool_ref[g]))
    return jnp.concatenate(outs, axis=-1) * scale_ref[...]


def _even_mix_kernel(att_ref, u_ref, halo_ref, y_ref, wpool_ref, pscale_ref, wo_ref, g_ref, b_ref,
                     o_ref, ctx_ref, *, alpha, blocks_per_seq):
    tm = u_ref.shape[0]
    fw = att_ref.shape[1]
    blk = pl.program_id(0) % blocks_per_seq
    ctx_ref[0:HALO_ROWS, :] = jnp.where(blk == 0, 0.0, halo_ref[...])
    ctx_ref[HALO_ROWS:HALO_ROWS + tm, :] = u_ref[...]
    pos = blk * tm + lax.broadcasted_iota(jnp.int32, (tm, 1), 0)
    n_avail = (pos + 1).astype(F32)
    pool = _pool_groups(ctx_ref, n_avail, wpool_ref, pscale_ref, tm, HALO_ROWS)
    mix = _dot(att_ref[...], wo_ref[0:fw, :]) + _dot(pool.astype(BF16), wo_ref[fw:, :])
    o_ref[...] = _layer_norm(alpha * y_ref[...] + mix, g_ref[...], b_ref[...])


def _even_mix(att, u, y, wpool, pscale, wo, g, b, *, alpha, seq_len, tm):
    rows, d = y.shape
    fw, pw = att.shape[1], u.shape[1]
    halo_blocks = tm // HALO_ROWS
    row = lambda w: pl.BlockSpec((tm, w), lambda i: (i, 0))
    return pl.pallas_call(
        functools.partial(_even_mix_kernel, alpha=alpha, blocks_per_seq=seq_len // tm),
        out_shape=jax.ShapeDtypeStruct((rows, d), F32),
        grid=(rows // tm,),
        in_specs=[row(fw), row(pw),
                  pl.BlockSpec((HALO_ROWS, pw), lambda i: (jnp.maximum(i * halo_blocks - 1, 0), 0)),
                  row(d), _const_spec(wpool.shape), _const_spec(pscale.shape), _const_spec(wo.shape),
                  _const_spec(g.shape), _const_spec(b.shape)],
        out_specs=row(d),
        scratch_shapes=[pltpu.VMEM((HALO_ROWS + tm, pw), F32)],
        compiler_params=_params("parallel"),
        name="even_mix",
    )(att, u, u, y, wpool, pscale, wo, g, b)


def _pool_sample_kernel(ctx_ref, wpool_ref, pscale_ref, o_ref, *, pos0, n_new):
    gw = wpool_ref.shape[1]
    for i in range(n_new):
        n_avail = float(pos0 + i + 1)
        outs = []
        for g, w in enumerate(POOL_WINDOWS):
            lanes = slice(g * gw, (g + 1) * gw)
            cur = ctx_ref[POOL_BUF + i, :, lanes]
            win = cur
            for j in range(1, w):
                win = win + ctx_ref[POOL_BUF + i - j, :, lanes]
            mean = win / min(float(w), n_avail)
            outs.append(_dot((mean - cur).astype(BF16), wpool_ref[g]))
        o_ref[i] = jnp.concatenate(outs, axis=-1) * pscale_ref[...]


def _pool_sample(ctx_tm, wpool, pscale, *, pos0, n_new):
    steps, nb, c = ctx_tm.shape
    return pl.pallas_call(
        functools.partial(_pool_sample_kernel, pos0=pos0, n_new=n_new),
        out_shape=jax.ShapeDtypeStruct((n_new, nb, c), F32),
        grid=(1,),
        in_specs=[_const_spec(ctx_tm.shape), _const_spec(wpool.shape), _const_spec(pscale.shape)],
        out_specs=_const_spec((n_new, nb, c)),
        compiler_params=_params("arbitrary"),
        name="pool_sample",
    )(ctx_tm, wpool, pscale)


def _pair_mix_kernel(a_ref, p_ref, y_ref, wo_ref, g_ref, b_ref, o_ref, *, alpha):
    fw = a_ref.shape[1]
    mix = _dot(a_ref[...], wo_ref[0:fw, :]) + _dot(p_ref[...].astype(BF16), wo_ref[fw:, :])
    o_ref[...] = _layer_norm(alpha * y_ref[...] + mix, g_ref[...], b_ref[...])


def _pair_mix(a, p, y, wo, g, b, *, alpha, tm):
    rows, d = y.shape
    row = lambda w: pl.BlockSpec((tm, w), lambda i: (i, 0))
    return pl.pallas_call(
        functools.partial(_pair_mix_kernel, alpha=alpha),
        out_shape=jax.ShapeDtypeStruct((rows, d), F32),
        grid=(rows // tm,),
        in_specs=[row(a.shape[1]), row(p.shape[1]), row(d), _const_spec(wo.shape),
                  _const_spec(g.shape), _const_spec(b.shape)],
        out_specs=row(d),
        compiler_params=_params("parallel"),
        name="pair_mix",
    )(a, p, y, wo, g, b)


def _fox_sample_kernel(*refs, pages_per_step, n_new, heads):
    pt_ref = refs[0]
    del pt_ref
    qbd_ref, knew_ref, vnew_ref, lnew_ref = refs[1:5]
    k_refs = refs[5:5 + pages_per_step]
    v_refs = refs[5 + pages_per_step:5 + 2 * pages_per_step]
    lf_refs = refs[5 + 2 * pages_per_step:5 + 3 * pages_per_step]
    o_ref, m_ref, l_ref, acc_ref, carry_ref = refs[5 + 3 * pages_per_step:]
    step = pl.program_id(1)
    n_rows = qbd_ref.shape[0]
    qbd = qbd_ref[...]

    @pl.when(step == 0)
    def _():
        lf = lnew_ref[...]
        lane = lax.broadcasted_iota(jnp.int32, lf.shape, 1)
        tok = lax.broadcasted_iota(jnp.int32, lf.shape, 0) // heads
        upto = jnp.where(lax.broadcasted_iota(jnp.int32, (LANES, LANES), 0)
                         <= lax.broadcasted_iota(jnp.int32, (LANES, LANES), 1), 1.0, 0.0).astype(BF16)
        hi, mid, lo = _split3(lf)
        pre = _dot(hi, upto) + _dot(mid, upto) + _dot(lo, upto)
        n_col = jnp.sum(jnp.where(lane == tok, pre, 0.0), axis=1, keepdims=True)
        kn = knew_ref[...].astype(BF16)
        s = _dot_nt(qbd, kn) + n_col - pre[:, 0:SUBLANES]
        key = lax.broadcasted_iota(jnp.int32, s.shape, 1)
        qtok = lax.broadcasted_iota(jnp.int32, s.shape, 0) // heads
        s = jnp.where(key <= qtok, s, NEG_INF)
        m = jnp.max(s, axis=1, keepdims=True)
        p = jnp.exp(s - m)
        m_ref[...] = jnp.broadcast_to(m, m_ref.shape)
        l_ref[...] = jnp.broadcast_to(jnp.sum(p, axis=1, keepdims=True), l_ref.shape)
        acc_ref[...] = _dot(p.astype(BF16), vnew_ref[...].astype(BF16))
        carry_ref[...] = jnp.broadcast_to(n_col, carry_ref.shape)

    src = lax.broadcasted_iota(jnp.int32, (PAGE_SIZE, PAGE_SIZE), 0)
    dst = lax.broadcasted_iota(jnp.int32, (PAGE_SIZE, PAGE_SIZE), 1)
    after = jnp.where(src > dst, 1.0, 0.0).astype(BF16)
    reps = n_rows // heads
    carry = carry_ref[...]
    s_parts = []
    for t in range(pages_per_step):
        lf = jnp.tile(lf_refs[t][...], (reps, 1))
        hi, mid, lo = _split3(lf)
        within = _dot(hi, after) + _dot(mid, after) + _dot(lo, after)
        bias = carry + within
        s_parts.append(_dot(qbd, k_refs[t][...].astype(BF16)) + bias)
        carry = bias[:, 0:1] + lf[:, 0:1]
        carry = jnp.broadcast_to(carry, carry_ref.shape)
    carry_ref[...] = carry
    s = jnp.concatenate(s_parts, axis=1)
    m_prev = m_ref[...]
    m_new = jnp.maximum(m_prev, jnp.max(s, axis=1, keepdims=True))
    p = jnp.exp(s - jnp.tile(m_new, (1, pages_per_step)))
    scale = jnp.exp(m_prev - m_new)
    m_ref[...] = m_new
    l_ref[...] = scale * l_ref[...] + jnp.sum(p, axis=1, keepdims=True)
    pb = p.astype(BF16)
    pv = _dot_nt(pb[:, 0:PAGE_SIZE], v_refs[0][...].astype(BF16))
    for t in range(1, pages_per_step):
        pv = pv + _dot_nt(pb[:, t * PAGE_SIZE:(t + 1) * PAGE_SIZE], v_refs[t][...].astype(BF16))
    acc_ref[...] = jnp.tile(scale, (1, acc_ref.shape[1] // LANES)) * acc_ref[...] + pv

    @pl.when(step == pl.num_programs(1) - 1)
    def _():
        full = acc_ref[...] / jnp.tile(l_ref[...], (1, acc_ref.shape[1] // LANES))
        row_head = lax.broadcasted_iota(jnp.int32, full.shape, 0) % heads
        lane_head = lax.broadcasted_iota(jnp.int32, full.shape, 1) // D_HEAD
        own = jnp.where(row_head == lane_head, full, 0.0)
        o_ref[...] = jnp.sum(own.reshape(n_new, heads, full.shape[1]), axis=1).astype(o_ref.dtype)


def _fox_sample(qbd, knew, vnew, lnew, cache_k, cache_v, cache_lft, page_table, *, heads, n_new,
                pages_per_step):
    nb, n_rows, width = qbd.shape
    n_pages = page_table.shape[1]
    assert n_pages % pages_per_step == 0
    pt = page_table.reshape(-1)

    def page_map(t):
        def index(b, s, pt_ref):
            logical = n_pages - 1 - (s * pages_per_step + t)
            return (pt_ref[b * n_pages + logical], 0, 0)
        return index

    batch3 = lambda shape: pl.BlockSpec((None,) + shape, lambda b, s, pt_ref: (b, 0, 0))
    in_specs = [batch3((n_rows, width)), batch3((SUBLANES, width)), batch3((SUBLANES, width)),
                batch3((n_rows, LANES))]
    in_specs += [pl.BlockSpec((None, width, PAGE_SIZE), page_map(t)) for t in range(pages_per_step)]
    in_specs += [pl.BlockSpec((None, width, PAGE_SIZE), page_map(t)) for t in range(pages_per_step)]
    in_specs += [pl.BlockSpec((None, heads, PAGE_SIZE), page_map(t)) for t in range(pages_per_step)]
    return pl.pallas_call(
        functools.partial(_fox_sample_kernel, pages_per_step=pages_per_step, n_new=n_new, heads=heads),
        out_shape=jax.ShapeDtypeStruct((nb, n_new, width), BF16),
        grid_spec=pltpu.PrefetchScalarGridSpec(
            num_scalar_prefetch=1,
            grid=(nb, n_pages // pages_per_step),
            in_specs=in_specs,
            out_specs=batch3((n_new, width)),
            scratch_shapes=[pltpu.VMEM((n_rows, LANES), F32), pltpu.VMEM((n_rows, LANES), F32),
                            pltpu.VMEM((n_rows, width), F32), pltpu.VMEM((n_rows, LANES), F32)],
        ),
        compiler_params=_params("parallel", "arbitrary"),
        name="fox_sample",
    )(pt, qbd, knew, vnew, lnew, *([cache_k] * pages_per_step), *([cache_v] * pages_per_step),
      *([cache_lft] * pages_per_step))


def _odd_proj_kernel(*refs, tm):
    n_groups = len(DIL_CONFIGS)
    y_ref, w_ref = refs[0], refs[1]
    de_refs = refs[2:2 + 3 * n_groups]
    tail_refs = refs[2 + 3 * n_groups:2 + 5 * n_groups]
    stage_refs = refs[2 + 5 * n_groups:]
    yb = y_ref[...].astype(BF16)
    n_lane_groups = de_refs[0].shape[1]
    gw = n_lane_groups * LANES
    stage = 0
    for g, (window, dil) in enumerate(DIL_CONFIGS):
        for part in range(3):
            col = (g * 3 + part) * gw
            res = _dot(yb, w_ref[:, col:col + gw])
            if part == 0:
                res = res * (D_HEAD ** -0.5)
            dst = de_refs[g * 3 + part]
            if dil == 1:
                for a in range(n_lane_groups):
                    dst[0, a] = res[:, a * LANES:(a + 1) * LANES].astype(BF16)
            else:
                s_ref = stage_refs[stage]
                stage += 1
                for a in range(n_lane_groups):
                    s_ref[a] = res[:, a * LANES:(a + 1) * LANES]
                for r in range(dil):
                    for a in range(n_lane_groups):
                        dst[r, a] = s_ref[a, pl.ds(r, tm // dil, stride=dil), :].astype(BF16)
            if part > 0:
                keep = min(window, tm)
                tail_refs[g * 2 + part - 1][...] = res[tm - keep:, :].T


def _odd_proj(y3, w, *, tm):
    batch, seq_len, d = y3.shape
    gw = w.shape[1] // (3 * len(DIL_CONFIGS))
    n_lane_groups = gw // LANES
    n_tiles = seq_len // tm
    out_shape, out_specs, n_stage = [], [], 0
    for window, dil in DIL_CONFIGS:
        assert tm % (dil * 16) == 0 and (window % tm == 0 or tm % window == 0)
        for _ in range(3):
            out_shape.append(jax.ShapeDtypeStruct((batch, dil, n_lane_groups, seq_len // dil, LANES), BF16))
            out_specs.append(pl.BlockSpec((None, dil, n_lane_groups, tm // dil, LANES),
                                          lambda b, i: (b, 0, 0, i, 0)))
        n_stage += 3 if dil > 1 else 0
    for window, dil in DIL_CONFIGS:
        keep = min(window, tm)
        first = n_tiles - window // keep
        for _ in range(2):
            out_shape.append(jax.ShapeDtypeStruct((batch, gw, window), F32))
            out_specs.append(pl.BlockSpec((None, gw, keep),
                                          lambda b, i, first=first: (b, 0, jnp.maximum(i - first, 0))))
    return pl.pallas_call(
        functools.partial(_odd_proj_kernel, tm=tm),
        out_shape=tuple(out_shape),
        grid=(batch, n_tiles),
        in_specs=[pl.BlockSpec((None, tm, d), lambda b, i: (b, i, 0)),
                  pl.BlockSpec(w.shape, lambda b, i: (0, 0), pipeline_mode=pl.Buffered(1))],
        out_specs=tuple(out_specs),
        scratch_shapes=[pltpu.VMEM((n_lane_groups, tm, LANES), F32)] * n_stage,
        compiler_params=_params("parallel", "arbitrary"),
        name="odd_proj",
    )(y3, w)


def _dil_attn_kernel(q_ref, kp_ref, kc_ref, vp_ref, vc_ref, bp_ref, bc_ref, o_ref, lse_ref, *, dil, heads):
    qb = q_ref.shape[2]
    even_lane = _lane_is_even_head((qb, LANES))

    def one_class(r, carry):
        for g in range(heads // HEADS_PER_LANE_GROUP):
            q2 = q_ref[r, g]
            kp, kc = kp_ref[r, g], kc_ref[r, g]
            vp, vc = vp_ref[r, g], vc_ref[r, g]
            outs, lses = [], []
            for sub in range(HEADS_PER_LANE_GROUP):
                h = g * HEADS_PER_LANE_GROUP + sub
                qh = jnp.where(even_lane if sub == 0 else ~even_lane, q2, jnp.zeros_like(q2))
                sp = _dot_nt(qh, kp) + bp_ref[h]
                sc = _dot_nt(qh, kc) + bc_ref[h]
                m = jnp.maximum(jnp.max(sp, axis=1, keepdims=True), jnp.max(sc, axis=1, keepdims=True))
                pp = jnp.exp(sp - m)
                pc = jnp.exp(sc - m)
                den = jnp.sum(pp, axis=1, keepdims=True) + jnp.sum(pc, axis=1, keepdims=True)
                pv = _dot(pp.astype(BF16), vp) + _dot(pc.astype(BF16), vc)
                outs.append(pv / den)
                lses.append(jnp.broadcast_to(m + jnp.log(den), (qb, LANES)))
            o2 = jnp.where(even_lane, outs[0], outs[1])
            l2 = jnp.where(even_lane, lses[0], lses[1])
            if dil == 1:
                o_ref[g] = o2
                lse_ref[g] = l2
            else:
                o_ref[g, pl.ds(r, qb, stride=dil), :] = o2
                lse_ref[g, pl.ds(r, qb, stride=dil), :] = l2
        return carry

    if dil == 1:
        one_class(0, 0)
    else:
        lax.fori_loop(0, dil, one_class, 0)


def _dil_attention(q, k, v, bias_prev, bias_cur, *, heads, qb):
    batch, dil, n_lane_groups, slots, _ = q.shape
    blk = (None, dil, n_lane_groups, qb, LANES)
    cur = pl.BlockSpec(blk, lambda b, i: (b, 0, 0, i, 0))
    prev = pl.BlockSpec(blk, lambda b, i: (b, 0, 0, jnp.maximum(i - 1, 0), 0))
    nat = pl.BlockSpec((None, n_lane_groups, qb * dil, LANES), lambda b, i: (b, 0, i, 0))
    out_sds = jax.ShapeDtypeStruct((batch, n_lane_groups, slots * dil, LANES), F32)
    return pl.pallas_call(
        functools.partial(_dil_attn_kernel, dil=dil, heads=heads),
        out_shape=(out_sds, out_sds),
        grid=(batch, slots // qb),
        in_specs=[cur, prev, cur, prev, cur,
                  pl.BlockSpec((None, heads, qb, qb), lambda b, i: (jnp.minimum(i, 1), 0, 0, 0)),
                  _const_spec(bias_cur.shape)],
        out_specs=(nat, nat),
        compiler_params=_params("parallel", "arbitrary"),
        name=f"dil_attention_d{dil}",
    )(q, k, k, v, v, bias_prev, bias_cur)


def _odd_mix_kernel(*refs, alpha):
    n = len(DIL_CONFIGS)
    o_refs, lse_refs = refs[:n], refs[n:2 * n]
    y_ref, wo_ref, g_ref, b_ref, out_ref = refs[2 * n:]
    merged = []
    for a in range(o_refs[0].shape[0]):
        lses = [r[a] for r in lse_refs]
        top = functools.reduce(jnp.maximum, lses)
        wts = [jnp.exp(l - top) for l in lses]
        num = functools.reduce(jnp.add, [w * r[a] for w, r in zip(wts, o_refs)])
        merged.append((num / functools.reduce(jnp.add, wts)).astype(BF16))
    mix = _dot(jnp.concatenate(merged, axis=-1), wo_ref[...])
    out_ref[...] = _layer_norm(alpha * y_ref[...] + mix, g_ref[...], b_ref[...])


def _odd_mix(outs, lses, y3, wo, g, b, *, alpha, tm):
    batch, seq_len, d = y3.shape
    n_lane_groups = outs[0].shape[1]
    part = pl.BlockSpec((None, n_lane_groups, tm, LANES), lambda b, i: (b, 0, i, 0))
    row = pl.BlockSpec((None, tm, d), lambda b, i: (b, i, 0))
    return pl.pallas_call(
        functools.partial(_odd_mix_kernel, alpha=alpha),
        out_shape=jax.ShapeDtypeStruct((batch, seq_len, d), F32),
        grid=(batch, seq_len // tm),
        in_specs=[part] * (2 * len(outs)) + [row, _const_spec(wo.shape), _const_spec(g.shape),
                                             _const_spec(b.shape)],
        out_specs=row,
        compiler_params=_params("parallel", "parallel"),
        name="odd_mix",
    )(*outs, *lses, y3, wo, g, b)


def _proj_kernel(x_ref, w_ref, o_ref):
    o_ref[...] = _dot(x_ref[...].astype(BF16), w_ref[...])


def _proj(x, w, *, tn):
    rows, d = x.shape
    n = w.shape[1]
    return pl.pallas_call(
        _proj_kernel,
        out_shape=jax.ShapeDtypeStruct((rows, n), F32),
        grid=(n // tn,),
        in_specs=[_const_spec((rows, d)), pl.BlockSpec((d, tn), lambda j: (0, j))],
        out_specs=pl.BlockSpec((rows, tn), lambda j: (0, j)),
        compiler_params=_params("parallel"),
        name="proj",
    )(x, w)


def _dil_sample_kernel(qbd_ref, kc_ref, vc_ref, kn_ref, vn_ref, ktail_ref, vtail_ref, tabc_ref, tabn_ref,
                       o_ref, lse_ref, ko_ref, vo_ref, *, n_new, heads):
    width, buf_len = kc_ref.shape
    qbd = qbd_ref[...]
    chunk = min(buf_len, 512)
    starts = range(0, buf_len, chunk)
    sc = [_dot(qbd, kc_ref[:, c0:c0 + chunk].astype(BF16)) + tabc_ref[:, c0:c0 + chunk] for c0 in starts]
    sn = _dot_nt(qbd, kn_ref[...].astype(BF16)) + tabn_ref[...]
    m = functools.reduce(jnp.maximum, [jnp.max(s, axis=1, keepdims=True) for s in sc + [sn]])
    pn = jnp.exp(sn - m)
    den = jnp.sum(pn, axis=1, keepdims=True)
    pv = _dot(pn.astype(BF16), vn_ref[...].astype(BF16))
    for s, c0 in zip(sc, starts):
        p = jnp.exp(s - m)
        den = den + jnp.sum(p, axis=1, keepdims=True)
        pv = pv + _dot_nt(p.astype(BF16), vc_ref[:, c0:c0 + chunk].astype(BF16))
    full = pv / den
    lse = jnp.broadcast_to(m + jnp.log(den), full.shape)
    own = (lax.broadcasted_iota(jnp.int32, full.shape, 0) % heads
           == lax.broadcasted_iota(jnp.int32, full.shape, 1) // D_HEAD)
    o_ref[...] = jnp.sum(jnp.where(own, full, 0.0).reshape(n_new, heads, width), axis=1)
    lse_ref[...] = jnp.sum(jnp.where(own, lse, 0.0).reshape(n_new, heads, width), axis=1)
    rows = 64
    is_new = lax.broadcasted_iota(jnp.int32, (rows, LANES), 1) >= LANES - n_new
    for src_ref, tail_ref, dst_ref in ((kc_ref, ktail_ref, ko_ref), (vc_ref, vtail_ref, vo_ref)):
        for r0 in range(0, width, rows):
            rolled = pltpu.roll(src_ref[r0:r0 + rows, :], buf_len - n_new, axis=1)
            if buf_len > LANES:
                dst_ref[r0:r0 + rows, 0:buf_len - LANES] = rolled[:, 0:buf_len - LANES]
            dst_ref[r0:r0 + rows, buf_len - LANES:] = jnp.where(is_new, tail_ref[r0:r0 + rows, :],
                                                                rolled[:, buf_len - LANES:])


def _dil_sample(qbd, kc, vc, kn, vn, ktail, vtail, tab_c, tab_n, *, n_new, heads):
    nb, width, buf_len = kc.shape
    n_rows = qbd.shape[1]
    b3 = lambda *shape: pl.BlockSpec((None,) + shape, lambda b: (b,) + (0,) * len(shape))
    small = jax.ShapeDtypeStruct((nb, n_new, width), F32)
    big = jax.ShapeDtypeStruct((nb, width, buf_len), F32)
    return pl.pallas_call(
        functools.partial(_dil_sample_kernel, n_new=n_new, heads=heads),
        out_shape=(small, small, big, big),
        grid=(nb,),
        in_specs=[b3(n_rows, width), b3(width, buf_len), b3(width, buf_len),
                  b3(SUBLANES, width), b3(SUBLANES, width), b3(width, LANES), b3(width, LANES),
                  _const_spec(tab_c.shape), _const_spec(tab_n.shape)],
        out_specs=(b3(n_new, width), b3(n_new, width), b3(width, buf_len), b3(width, buf_len)),
        compiler_params=_params("parallel"),
        name=f"dil_sample_l{buf_len}",
    )(qbd, kc, vc, kn, vn, ktail, vtail, tab_c, tab_n)


def _rel_bucket(dist):
    exact = REL_BUCKETS // 2
    d = jnp.maximum(dist, 1).astype(F32)
    large = exact + (jnp.log(d / exact) / math.log(REL_MAX_DIST / exact) * (REL_BUCKETS - exact)).astype(jnp.int32)
    large = jnp.minimum(large, REL_BUCKETS - 1)
    return jnp.where(dist < exact, dist, large)


def _group_bias(rel_bias, g, window, dil, heads):
    dist = jnp.arange(window // dil + 1) * dil
    onehot = _rel_bucket(dist)[:, None] == jnp.arange(REL_BUCKETS)[None, :]
    cols = rel_bias[:, g * heads:(g + 1) * heads].astype(F32)
    return jnp.sum(jnp.where(onehot[:, :, None], cols[None], 0.0), axis=1).T


def _toeplitz(w, n):
    heads, period = w.shape
    flat = jnp.tile(w, (1, n))[:, :n * (period - 1)]
    return flat.reshape(heads, n, period - 1)[:, :, :n]


def _prompt_bias_tiles(bias, qb):
    heads, n_keys = bias.shape
    assert n_keys == qb + 1
    neg = jnp.full((heads, qb), NEG_INF, F32)
    cur = _toeplitz(jnp.concatenate([bias[:, 0:qb], neg], axis=1), qb)
    prev = _toeplitz(jnp.concatenate([bias[:, qb:qb + 1], neg, bias[:, 1:qb]], axis=1), qb)
    cur, prev = jnp.swapaxes(cur, 1, 2), jnp.swapaxes(prev, 1, 2)
    return jnp.stack([jnp.full_like(prev, NEG_INF), prev]), cur


def _sample_bias_tables(bias, dil, n_new, buf_len):
    heads, n_keys = bias.shape
    n_back = n_keys - 1
    assert n_back * dil == buf_len
    oldest_first = bias[:, :0:-1]
    gaps = jnp.full((heads, n_back, dil - 1), NEG_INF, F32)
    spread = jnp.concatenate([oldest_first[:, :, None], gaps], axis=2).reshape(heads, buf_len)
    neg_col = jnp.full((heads,), NEG_INF, F32)
    tab_c, tab_n = [], []
    for i in range(n_new):
        tab_c.append(jnp.concatenate([jnp.full((heads, i), NEG_INF, F32), spread[:, :buf_len - i]], axis=1))
        in_set = lambda j: j <= i and (i - j) % dil == 0
        tab_n.append(jnp.stack([bias[:, (i - j) // dil] if in_set(j) else neg_col for j in range(SUBLANES)],
                               axis=1))
    return jnp.concatenate(tab_c, axis=0), jnp.concatenate(tab_n, axis=0)


PROMPT_ROW_TILE = 512
ATTN_BLOCK = 512
DIL_SLOT_BLOCK = 128
CUMSUM_BLOCK = 512
PAGES_PER_STEP = 8


def _pad_axis(x, axis, size):
    pad = [(0, 0)] * x.ndim
    pad[axis] = (0, size - x.shape[axis])
    return jnp.pad(x, pad)


def token_major(xt, lead):
    return jnp.transpose(xt.reshape(lead, -1, D_HEAD, xt.shape[-1]), (0, 3, 1, 2))


def kernel(x_prompt, x_sample, cache_fox_k, cache_fox_v, cache_fox_logf, state_pool, cache_dil0_k, cache_dil0_v, cache_dil1_k, cache_dil1_v, cache_dil2_k, cache_dil2_v, page_table, p_prompt, p_sample, w_in_even, b_fgate, pool_w, pool_scale, w_out_even, w_in_odd, w_out_odd, rel_bias, ffn1_wg, ffn1_wu, ffn1_wd, ffn2_wg, ffn2_wu, ffn2_wd, ln_g, ln_b, ple_wg, ple_bg, ple_wp):
    depth = ffn1_wg.shape[0]
    alpha = (2 * depth) ** 0.25
    batch, seq_len, d_model = x_prompt.shape
    nb, n_new, _ = x_sample.shape
    past_len = page_table.shape[1] * PAGE_SIZE
    fox_heads = cache_fox_k.shape[-2]
    fox_w = fox_heads * D_HEAD
    dil_heads = cache_dil0_k.shape[-2]
    dil_w = dil_heads * D_HEAD
    dil_caches_k = (cache_dil0_k, cache_dil1_k, cache_dil2_k)
    dil_caches_v = (cache_dil0_v, cache_dil1_v, cache_dil2_v)
    rows_p, rows_s = batch * seq_len, nb * n_new
    tm = PROMPT_ROW_TILE
    bf = lambda w: w.astype(BF16)
    vec = lambda a: a.reshape(1, -1)

    yp = x_prompt.reshape(rows_p, d_model)
    ys = x_sample.reshape(rows_s, d_model)
    outs = {name: [] for name in ("fkp", "fvp", "flp", "fks", "fvs", "fls", "plp", "pls")}
    dkp, dvp, dks, dvs = ([[] for _ in DIL_CONFIGS] for _ in range(4))

    for i in range(depth):
        w1 = (bf(ffn1_wg[i]), bf(ffn1_wu[i]), bf(ffn1_wd[i]), vec(ln_g[i, 0]), vec(ln_b[i, 0]))
        yp = _ffn(yp, *w1, alpha=alpha, tm=tm)
        ys = _ffn(ys, *w1, alpha=alpha, tm=rows_s)
        g1, b1 = vec(ln_g[i, 1]), vec(ln_b[i, 1])
        if i % 2 == 0:
            e = i // 2
            w_in = w_in_even[e]
            wqkv = bf(w_in[:, :3 * fox_w])
            wf = bf(_pad_axis(w_in[:, 3 * fox_w:3 * fox_w + fox_heads], 1, LANES))
            bfg = _pad_axis(vec(b_fgate[e]), 1, LANES)
            wu = bf(w_in[:, 3 * fox_w + fox_heads:])
            wpool, pscale, wo = bf(pool_w[e]), vec(pool_scale[e]), bf(w_out_even[e])
            qb, kb, vb, kt, vt, lft, u = _even_proj(yp.reshape(batch, seq_len, d_model), wqkv, wf, bfg, wu,
                                                    heads=fox_heads, tm=tm, kv_transposed=True)
            ct = _cumsum_lanes(lft, tc=CUMSUM_BLOCK)
            att = _fox_attention(qb, kb, vb, jnp.swapaxes(ct, 1, 2), ct, tq=ATTN_BLOCK).reshape(rows_p, fox_w)
            u = u.reshape(rows_p, -1)
            yp_next = _even_mix(att, u, yp, wpool, pscale, wo, g1, b1, alpha=alpha, seq_len=seq_len, tm=tm)
            outs["fkp"].append(token_major(kt, batch))
            outs["fvp"].append(token_major(vt, batch))
            outs["flp"].append(jnp.swapaxes(lft, 1, 2))
            outs["plp"].append(u.reshape(batch, seq_len, -1)[:, seq_len - POOL_BUF:])
            qs, _, _, ks, vs, lfts, us = _even_proj(ys[None], wqkv, wf, bfg, wu, heads=fox_heads, tm=rows_s,
                                                    kv_transposed=False)
            head_mask = (jnp.arange(fox_w)[None, :] // D_HEAD == jnp.arange(fox_heads)[:, None]).astype(BF16)
            qbd = (qs.reshape(nb, n_new, 1, fox_w) * head_mask).reshape(nb, n_new * fox_heads, fox_w)
            knew = _pad_axis(ks.reshape(nb, n_new, fox_w), 1, SUBLANES)
            vnew = _pad_axis(vs.reshape(nb, n_new, fox_w), 1, SUBLANES)
            lf3 = jnp.swapaxes(lfts.reshape(fox_heads, nb, n_new), 0, 1)
            lnew = jnp.broadcast_to(lf3[:, None], (nb, n_new, fox_heads, n_new)).reshape(nb, n_new * fox_heads, n_new)
            lnew = _pad_axis(lnew, 2, LANES)
            n_phys = cache_fox_k.shape[1]
            page_t = lambda c: jnp.transpose(c, (0, 2, 3, 1)).reshape(n_phys, fox_w, PAGE_SIZE)
            att_s = _fox_sample(qbd, knew, vnew, lnew, page_t(cache_fox_k[e]), page_t(cache_fox_v[e]),
                                jnp.swapaxes(cache_fox_logf[e], 1, 2), page_table,
                                heads=fox_heads, n_new=n_new, pages_per_step=PAGES_PER_STEP)
            ctx_s = jnp.concatenate([state_pool[e].astype(F32), us.reshape(nb, n_new, -1)], axis=1)
            pool_s = _pool_sample(jnp.swapaxes(ctx_s, 0, 1), wpool, pscale, pos0=past_len, n_new=n_new)
            pool_s = jnp.swapaxes(pool_s, 0, 1).reshape(rows_s, -1)
            ys_next = _pair_mix(att_s.reshape(rows_s, fox_w), pool_s, ys, wo, g1, b1, alpha=alpha, tm=rows_s)
            outs["fks"].append(ks.reshape(nb, n_new, fox_heads, D_HEAD))
            outs["fvs"].append(vs.reshape(nb, n_new, fox_heads, D_HEAD))
            outs["fls"].append(jnp.swapaxes(lf3, 1, 2))
            outs["pls"].append(ctx_s[:, -POOL_BUF:])
        else:
            o = i // 2
            w_in, wo = bf(w_in_odd[o]), bf(w_out_odd[o])
            biases = [_group_bias(rel_bias, g, window, dil, dil_heads) for g, (window, dil) in enumerate(DIL_CONFIGS)]
            proj = _odd_proj(yp.reshape(batch, seq_len, d_model), w_in, tm=tm)
            n_groups = len(DIL_CONFIGS)
            o_parts, lse_parts = [], []
            for g, (window, dil) in enumerate(DIL_CONFIGS):
                qd, kd, vd = proj[3 * g:3 * g + 3]
                bias_prev, bias_cur = _prompt_bias_tiles(biases[g], DIL_SLOT_BLOCK)
                og, lg = _dil_attention(qd, kd, vd, bias_prev, bias_cur, heads=dil_heads, qb=DIL_SLOT_BLOCK)
                o_parts.append(og)
                lse_parts.append(lg)
                dkp[g].append(token_major(proj[3 * n_groups + 2 * g], batch))
                dvp[g].append(token_major(proj[3 * n_groups + 2 * g + 1], batch))
            yp_next = _odd_mix(o_parts, lse_parts, yp.reshape(batch, seq_len, d_model), wo, g1, b1,
                               alpha=alpha, tm=tm).reshape(rows_p, d_model)
            lane_major = lambda a: jnp.swapaxes(a.reshape(rows_s, dil_w // LANES, LANES), 0, 1)[None]
            proj_s = _proj(ys, w_in, tn=3 * dil_w).reshape(nb, n_new, n_groups, 3, dil_w)
            head_mask = (jnp.arange(dil_w)[None, :] // D_HEAD == jnp.arange(dil_heads)[:, None]).astype(F32)
            o_parts, lse_parts = [], []
            for g, (window, dil) in enumerate(DIL_CONFIGS):
                qg = proj_s[:, :, g, 0] * (D_HEAD ** -0.5)
                qbd = (qg[:, :, None, :] * head_mask).astype(BF16).reshape(nb, n_new * dil_heads, dil_w)
                k_new, v_new = proj_s[:, :, g, 1], proj_s[:, :, g, 2]
                buf_len = dil_caches_k[g].shape[2]
                pos_minor = lambda c: jnp.transpose(c, (0, 2, 3, 1)).reshape(nb, dil_w, buf_len)
                last_lanes = lambda x: jnp.pad(jnp.swapaxes(x, 1, 2), ((0, 0), (0, 0), (LANES - n_new, 0)))
                tab_c, tab_n = _sample_bias_tables(biases[g], dil, n_new, buf_len)
                og, lg, k_roll, v_roll = _dil_sample(
                    qbd, pos_minor(dil_caches_k[g][o]), pos_minor(dil_caches_v[g][o]),
                    _pad_axis(k_new, 1, SUBLANES), _pad_axis(v_new, 1, SUBLANES),
                    last_lanes(k_new), last_lanes(v_new), tab_c, tab_n, n_new=n_new, heads=dil_heads)
                o_parts.append(lane_major(og))
                lse_parts.append(lane_major(lg))
                dks[g].append(token_major(k_roll, nb))
                dvs[g].append(token_major(v_roll, nb))
            ys_next = _odd_mix(o_parts, lse_parts, ys[None], wo, g1, b1, alpha=alpha,
                               tm=rows_s).reshape(rows_s, d_model)
        yp, ys = yp_next, ys_next
        w2 = (bf(ffn2_wg[i]), bf(ffn2_wu[i]), bf(ffn2_wd[i]), vec(ln_g[i, 2]), vec(ln_b[i, 2]))
        ple_w = (bf(ple_wg[i]), vec(ple_bg[i]), bf(ple_wp[i]))
        yp = _ffn(yp, *w2, alpha=alpha, tm=tm, ple=(p_prompt[i].reshape(rows_p, -1),) + ple_w)
        ys = _ffn(ys, *w2, alpha=alpha, tm=rows_s, ple=(p_sample[i].reshape(rows_s, -1),) + ple_w)

    stack = lambda parts: jnp.stack(parts)
    result = [yp.reshape(batch, seq_len, d_model), ys.reshape(nb, n_new, d_model)]
    result += [stack(outs[name]) for name in ("fkp", "fvp", "flp", "fks", "fvs", "fls", "plp", "pls")]
    for g in range(len(DIL_CONFIGS)):
        result += [stack(dkp[g]), stack(dvp[g])]
    for g in range(len(DIL_CONFIGS)):
        result += [stack(dks[g]), stack(dvs[g])]
    return tuple(result)
```

```python
import functools
import math

import jax
import jax.numpy as jnp
import numpy as np
from jax import lax
from jax.experimental import pallas as pl
from jax.experimental.pallas import tpu as pltpu

F32 = jnp.float32
BF16 = jnp.bfloat16

D_HEAD = 64
POOL_WINDOWS = (2, 4, 8, 16)
POOL_BUF = max(POOL_WINDOWS) - 1
DIL_CONFIGS = ((128, 1), (512, 4), (2048, 16))
REL_BUCKETS = 32
REL_MAX_DIST = 2048
LN_EPS = 1e-5
NEG_INF = -1e30
PAGE_SIZE = 128

LANES = 128
SUBLANES = 8
VMEM_LIMIT_BYTES = 56 * 1024 * 1024

HEADS_PER_LANE_GROUP = LANES // D_HEAD
HALO_ROWS = 16


def _params(*semantics):
    return pltpu.CompilerParams(dimension_semantics=semantics, vmem_limit_bytes=VMEM_LIMIT_BYTES)


def _dot(a, b):
    return jnp.dot(a, b, preferred_element_type=F32)


def _dot_nt(a, b):
    return lax.dot_general(a, b, (((1,), (1,)), ((), ())), preferred_element_type=F32)


def _layer_norm(z, g, b):
    mu = jnp.mean(z, axis=-1, keepdims=True)
    zc = z - mu
    var = jnp.mean(zc * zc, axis=-1, keepdims=True)
    return zc * lax.rsqrt(var + LN_EPS) * g + b


def _log_sigmoid(x):
    return jnp.minimum(x, 0.0) - jnp.log1p(jnp.exp(-jnp.abs(x)))


def _split3(x):
    hi = x.astype(BF16)
    r1 = x - hi.astype(F32)
    mid = r1.astype(BF16)
    lo = (r1 - mid.astype(F32)).astype(BF16)
    return hi, mid, lo


def _lane_is_even_head(shape):
    return lax.broadcasted_iota(jnp.int32, shape, len(shape) - 1) % LANES < D_HEAD


def _const_spec(shape):
    zeros = (0,) * len(shape)
    return pl.BlockSpec(shape, lambda *_: zeros)


def _ffn_kernel(*refs, alpha, ff_chunk, with_ple):
    if with_ple:
        (x_ref, wg_ref, wu_ref, wd_ref, g_ref, b_ref, p_ref, pwg_ref, pbg_ref, pwp_ref,
         o_ref, acc_ref) = refs
    else:
        x_ref, wg_ref, wu_ref, wd_ref, g_ref, b_ref, o_ref, acc_ref = refs
    x = x_ref[...]
    xb = x.astype(BF16)
    d_ff = wg_ref.shape[1]
    for c in range(d_ff // ff_chunk):
        cols = slice(c * ff_chunk, (c + 1) * ff_chunk)
        gate = _dot(xb, wg_ref[:, cols])
        up = _dot(xb, wu_ref[:, cols])
        h = (gate * jax.nn.sigmoid(gate) * up).astype(BF16)
        part = _dot(h, wd_ref[cols, :])
        if c == 0:
            acc_ref[...] = part
        else:
            acc_ref[...] += part
    y = _layer_norm(alpha * x + 0.5 * acc_ref[...], g_ref[...], b_ref[...])
    if with_ple:
        gate = jax.nn.sigmoid(_dot(y.astype(BF16), pwg_ref[...]) + pbg_ref[...])
        y = y + gate * _dot(p_ref[...].astype(BF16), pwp_ref[...])
    o_ref[...] = y


def _ffn(x, wg, wu, wd, g, b, *, alpha, tm, ple=None):
    rows, d = x.shape
    d_ff = wg.shape[1]
    ff_chunk = 256
    assert rows % tm == 0 and d_ff % ff_chunk == 0
    row_spec = pl.BlockSpec((tm, d), lambda i: (i, 0))
    in_specs = [row_spec, _const_spec((d, d_ff)), _const_spec((d, d_ff)), _const_spec((d_ff, d)),
                _const_spec((1, d)), _const_spec((1, d))]
    args = [x, wg, wu, wd, g, b]
    if ple is not None:
        p, pwg, pbg, pwp = ple
        in_specs += [pl.BlockSpec((tm, p.shape[1]), lambda i: (i, 0)), _const_spec(pwg.shape),
                     _const_spec((1, d)), _const_spec(pwp.shape)]
        args += [p, pwg, pbg, pwp]
    return pl.pallas_call(
        functools.partial(_ffn_kernel, alpha=alpha, ff_chunk=ff_chunk, with_ple=ple is not None),
        out_shape=jax.ShapeDtypeStruct((rows, d), F32),
        grid=(rows // tm,),
        in_specs=in_specs,
        out_specs=row_spec,
        scratch_shapes=[pltpu.VMEM((tm, d), F32)],
        compiler_params=_params("parallel"),
        name="ffn_ple" if ple is not None else "ffn",
    )(*args)


N_SPLIT = 3
DECAY_LANE = D_HEAD


def _head_slots(x):
    low = lax.broadcasted_iota(jnp.int32, (x.shape[0], LANES), 1) < D_HEAD
    slots = []
    for g in range(x.shape[1] // LANES):
        pair = x[:, g * LANES:(g + 1) * LANES]
        slots.append(jnp.where(low, pair, 0.0))
        slots.append(jnp.where(low, pltpu.roll(pair, D_HEAD, axis=1), 0.0))
    return slots


def _even_proj_kernel(*refs, prompt):
    if prompt:
        (y_ref, wqkv_ref, wf_ref, bf_ref, wu_ref, place_q_ref, place_k_ref, ones_q_ref, ones_k_ref, ones_v_ref,
         qa_ref, ka_ref, va_ref, kt_ref, vt_ref, logft_ref, u_ref, carry_ref) = refs
    else:
        y_ref, wqkv_ref, wf_ref, bf_ref, wu_ref, q_ref, k_ref, v_ref, logft_ref, u_ref = refs
    yb = y_ref[...].astype(BF16)
    hw = wqkv_ref.shape[1] // 3
    heads = logft_ref.shape[0]
    q = _dot(yb, wqkv_ref[:, 0:hw]) * (D_HEAD ** -0.5)
    k = _dot(yb, wqkv_ref[:, hw:2 * hw])
    v = _dot(yb, wqkv_ref[:, 2 * hw:3 * hw])
    logf = _log_sigmoid(_dot(yb, wf_ref[...]) + bf_ref[...])
    logft_ref[...] = logf.T[:heads, :]
    u_ref[...] = _dot(yb, wu_ref[...])
    if not prompt:
        q_ref[...] = q.astype(BF16)
        k_ref[...] = k
        v_ref[...] = v
        return
    kt_ref[...] = k.T
    vt_ref[...] = v.T
    @pl.when(pl.program_id(1) == 0)
    def _():
        carry_ref[...] = jnp.zeros_like(carry_ref)

    tm = logf.shape[0]
    tri = jnp.where(lax.broadcasted_iota(jnp.int32, (tm, tm), 0) >= lax.broadcasted_iota(jnp.int32, (tm, tm), 1),
                    1.0, 0.0).astype(BF16)
    c = functools.reduce(jnp.add, [_dot(tri, part) for part in _split3(logf)]) + carry_ref[0:1, :]
    carry_ref[...] = jnp.broadcast_to(c[tm - 1:tm, :], carry_ref.shape)
    c_terms = jnp.concatenate(_split3(c), axis=1)
    decay_q = _dot(c_terms, place_q_ref[...]) + ones_q_ref[...]
    decay_k = _dot(c_terms, place_k_ref[...]) + ones_k_ref[...]
    for h, (qs, ks, vs) in enumerate(zip(_head_slots(q), _head_slots(k), _head_slots(v))):
        lanes = slice(h * LANES, (h + 1) * LANES)
        qa_ref[:, lanes] = (qs + decay_q[:, lanes]).astype(BF16)
        ka_ref[:, lanes] = (ks + decay_k[:, lanes]).astype(BF16)
        va_ref[:, lanes] = (vs + ones_v_ref[:, lanes]).astype(BF16)


def _decay_placement(heads):
    place_q = np.zeros((N_SPLIT * LANES, heads * LANES), np.float32)
    place_k = np.zeros_like(place_q)
    ones_q = np.zeros((1, heads * LANES), np.float32)
    ones_k = np.zeros_like(ones_q)
    ones_v = np.zeros_like(ones_q)
    for h in range(heads):
        base = h * LANES + DECAY_LANE
        for j in range(N_SPLIT):
            place_q[j * LANES + h, base + j] = 1.0
            place_k[j * LANES + h, base + N_SPLIT + j] = -1.0
        ones_q[0, base + N_SPLIT:base + 2 * N_SPLIT] = 1.0
        ones_k[0, base:base + N_SPLIT] = 1.0
        ones_v[0, base:(h + 1) * LANES] = 1.0
    return (jnp.asarray(place_q, BF16), jnp.asarray(place_k, BF16), jnp.asarray(ones_q), jnp.asarray(ones_k),
            jnp.asarray(ones_v))


def _even_proj(y3, wqkv, wf, bf, wu, *, heads, tm, prompt):
    batch, seq_len, d = y3.shape
    hw = wqkv.shape[1] // 3
    pw = wu.shape[1]
    row = lambda w: pl.BlockSpec((None, tm, w), lambda b, i: (b, i, 0))
    col = lambda h: pl.BlockSpec((None, h, tm), lambda b, i: (b, 0, i))
    nat = lambda w, dt: jax.ShapeDtypeStruct((batch, seq_len, w), dt)
    args = [y3, wqkv, wf, bf, wu]
    tail_shape = (jax.ShapeDtypeStruct((batch, heads, seq_len), F32), nat(pw, F32))
    tail_specs = (col(heads), row(pw))
    if prompt:
        args += list(_decay_placement(heads))
        slot_w = heads * LANES
        kv_t = jax.ShapeDtypeStruct((batch, hw, seq_len), F32)
        out_shape = (nat(slot_w, BF16),) * 3 + (kv_t, kv_t) + tail_shape
        out_specs = (row(slot_w),) * 3 + (col(hw), col(hw)) + tail_specs
        scratch = [pltpu.VMEM((SUBLANES, LANES), F32)]
    else:
        out_shape = (nat(hw, BF16), nat(hw, F32), nat(hw, F32)) + tail_shape
        out_specs = (row(hw),) * 3 + tail_specs
        scratch = []
    return pl.pallas_call(
        functools.partial(_even_proj_kernel, prompt=prompt),
        out_shape=out_shape,
        grid=(batch, seq_len // tm),
        in_specs=[row(d)] + [_const_spec(a.shape) for a in args[1:]],
        out_specs=out_specs,
        scratch_shapes=scratch,
        compiler_params=_params("parallel", "arbitrary"),
        name="even_proj",
    )(*args)


SOFTMAX_ROWS = 64


def _fox_attn_kernel(qi_ref, ki_ref, q_ref, k_ref, v_ref, o_ref, m_ref, acc_ref, s_ref, p_ref, scale_ref,
                     *, heads):
    pair = pl.program_id(1)
    qi = qi_ref[pair]
    ki = ki_ref[pair]
    tq = q_ref.shape[0]
    tk = k_ref.shape[0]

    @pl.when(ki == 0)
    def _():
        m_ref[...] = jnp.full_like(m_ref, NEG_INF)
        acc_ref[...] = jnp.zeros_like(acc_ref)

    def step(masked):
        for h in range(heads):
            lanes = slice(h * LANES, (h + 1) * LANES)
            buf = h % 2
            s_ref[buf] = _dot_nt(q_ref[:, lanes], k_ref[:, lanes])
            for r0 in range(0, tq, SOFTMAX_ROWS):
                rows = slice(r0, r0 + SOFTMAX_ROWS)
                s = s_ref[buf, rows, :]
                if masked:
                    causal = (lax.broadcasted_iota(jnp.int32, s.shape, 1)
                              <= r0 + lax.broadcasted_iota(jnp.int32, s.shape, 0))
                    s = jnp.where(causal, s, NEG_INF)
                m_prev = m_ref[h, rows, :]
                m_new = jnp.maximum(m_prev, jnp.max(s, axis=1, keepdims=True))
                p_ref[buf, rows, :] = jnp.exp(s - jnp.tile(m_new, (1, tk // LANES))).astype(BF16)
                m_ref[h, rows, :] = m_new
                scale_ref[buf, rows, :] = jnp.exp(m_prev - m_new)
            acc_ref[h] = scale_ref[buf] * acc_ref[h] + _dot(p_ref[buf], v_ref[:, lanes])

    @pl.when(ki < qi)
    def _():
        step(masked=False)

    @pl.when(ki == qi)
    def _():
        step(masked=True)
        low = lax.broadcasted_iota(jnp.int32, (tq, LANES), 1) < D_HEAD
        for g in range(heads // HEADS_PER_LANE_GROUP):
            normed = []
            for sub in range(HEADS_PER_LANE_GROUP):
                acc = acc_ref[g * HEADS_PER_LANE_GROUP + sub]
                normed.append(acc / pltpu.roll(acc, D_HEAD, axis=1))
            out = jnp.where(low, normed[0], pltpu.roll(normed[1], D_HEAD, axis=1))
            o_ref[:, g * LANES:(g + 1) * LANES] = out.astype(o_ref.dtype)


def _fox_attention(q, k, v, *, heads, tq):
    batch, seq_len, slot_w = q.shape
    nq = seq_len // tq
    qi_tab = np.concatenate([np.full(i + 1, i) for i in range(nq)]).astype(np.int32)
    ki_tab = np.concatenate([np.arange(i + 1) for i in range(nq)]).astype(np.int32)
    q_map = lambda b, p, qi, ki: (b, qi[p], 0)
    k_map = lambda b, p, qi, ki: (b, ki[p], 0)
    return pl.pallas_call(
        functools.partial(_fox_attn_kernel, heads=heads),
        out_shape=jax.ShapeDtypeStruct((batch, seq_len, heads * D_HEAD), BF16),
        grid_spec=pltpu.PrefetchScalarGridSpec(
            num_scalar_prefetch=2,
            grid=(batch, len(qi_tab)),
            in_specs=[pl.BlockSpec((None, tq, slot_w), q_map), pl.BlockSpec((None, tq, slot_w), k_map),
                      pl.BlockSpec((None, tq, slot_w), k_map)],
            out_specs=pl.BlockSpec((None, tq, heads * D_HEAD), q_map),
            scratch_shapes=[pltpu.VMEM((heads, tq, LANES), F32), pltpu.VMEM((heads, tq, LANES), F32),
                            pltpu.VMEM((2, tq, tq), F32), pltpu.VMEM((2, tq, tq), BF16),
                            pltpu.VMEM((2, tq, LANES), F32)],
        ),
        compiler_params=_params("parallel", "arbitrary"),
        name="fox_attention",
    )(jnp.asarray(qi_tab), jnp.asarray(ki_tab), q, k, v)


def _pool_groups(ctx_ref, n_avail, w_pool_ref, scale_ref, rows, halo):
    gw = w_pool_ref.shape[1]
    outs = []
    for g, w in enumerate(POOL_WINDOWS):
        lanes = slice(g * gw, (g + 1) * gw)
        cur = ctx_ref[pl.ds(halo, rows), lanes]
        win = cur
        for j in range(1, w):
            win = win + ctx_ref[pl.ds(halo - j, rows), lanes]
        mean = win / jnp.minimum(float(w), n_avail)
        outs.append(_dot((mean - cur).astype(BF16), w_pool_ref[g]))
    return jnp.concatenate(outs, axis=-1) * scale_ref[...]


def _even_mix_kernel(att_ref, u_ref, halo_ref, y_ref, wpool_ref, pscale_ref, wo_ref, g_ref, b_ref,
                     o_ref, ctx_ref, *, alpha, blocks_per_seq):
    tm = u_ref.shape[0]
    fw = att_ref.shape[1]
    blk = pl.program_id(0) % blocks_per_seq
    ctx_ref[0:HALO_ROWS, :] = jnp.where(blk == 0, 0.0, halo_ref[...])
    ctx_ref[HALO_ROWS:HALO_ROWS + tm, :] = u_ref[...]
    pos = blk * tm + lax.broadcasted_iota(jnp.int32, (tm, 1), 0)
    n_avail = (pos + 1).astype(F32)
    pool = _pool_groups(ctx_ref, n_avail, wpool_ref, pscale_ref, tm, HALO_ROWS)
    mix = _dot(att_ref[...], wo_ref[0:fw, :]) + _dot(pool.astype(BF16), wo_ref[fw:, :])
    o_ref[...] = _layer_norm(alpha * y_ref[...] + mix, g_ref[...], b_ref[...])


def _even_mix(att, u, y, wpool, pscale, wo, g, b, *, alpha, seq_len, tm):
    rows, d = y.shape
    fw, pw = att.shape[1], u.shape[1]
    halo_blocks = tm // HALO_ROWS
    row = lambda w: pl.BlockSpec((tm, w), lambda i: (i, 0))
    return pl.pallas_call(
        functools.partial(_even_mix_kernel, alpha=alpha, blocks_per_seq=seq_len // tm),
        out_shape=jax.ShapeDtypeStruct((rows, d), F32),
        grid=(rows // tm,),
        in_specs=[row(fw), row(pw),
                  pl.BlockSpec((HALO_ROWS, pw), lambda i: (jnp.maximum(i * halo_blocks - 1, 0), 0)),
                  row(d), _const_spec(wpool.shape), _const_spec(pscale.shape), _const_spec(wo.shape),
                  _const_spec(g.shape), _const_spec(b.shape)],
        out_specs=row(d),
        scratch_shapes=[pltpu.VMEM((HALO_ROWS + tm, pw), F32)],
        compiler_params=_params("parallel"),
        name="even_mix",
    )(att, u, u, y, wpool, pscale, wo, g, b)


def _pool_sample_kernel(ctx_ref, wpool_ref, pscale_ref, o_ref, *, pos0, n_new):
    gw = wpool_ref.shape[1]
    for i in range(n_new):
        n_avail = float(pos0 + i + 1)
        outs = []
        for g, w in enumerate(POOL_WINDOWS):
            lanes = slice(g * gw, (g + 1) * gw)
            cur = ctx_ref[POOL_BUF + i, :, lanes]
            win = cur
            for j in range(1, w):
                win = win + ctx_ref[POOL_BUF + i - j, :, lanes]
            mean = win / min(float(w), n_avail)
            outs.append(_dot((mean - cur).astype(BF16), wpool_ref[g]))
        o_ref[i] = jnp.concatenate(outs, axis=-1) * pscale_ref[...]


def _pool_sample(ctx_tm, wpool, pscale, *, pos0, n_new):
    steps, nb, c = ctx_tm.shape
    return pl.pallas_call(
        functools.partial(_pool_sample_kernel, pos0=pos0, n_new=n_new),
        out_shape=jax.ShapeDtypeStruct((n_new, nb, c), F32),
        grid=(1,),
        in_specs=[_const_spec(ctx_tm.shape), _const_spec(wpool.shape), _const_spec(pscale.shape)],
        out_specs=_const_spec((n_new, nb, c)),
        compiler_params=_params("arbitrary"),
        name="pool_sample",
    )(ctx_tm, wpool, pscale)


def _pair_mix_kernel(a_ref, p_ref, y_ref, wo_ref, g_ref, b_ref, o_ref, *, alpha):
    fw = a_ref.shape[1]
    mix = _dot(a_ref[...], wo_ref[0:fw, :]) + _dot(p_ref[...].astype(BF16), wo_ref[fw:, :])
    o_ref[...] = _layer_norm(alpha * y_ref[...] + mix, g_ref[...], b_ref[...])


def _pair_mix(a, p, y, wo, g, b, *, alpha, tm):
    rows, d = y.shape
    row = lambda w: pl.BlockSpec((tm, w), lambda i: (i, 0))
    return pl.pallas_call(
        functools.partial(_pair_mix_kernel, alpha=alpha),
        out_shape=jax.ShapeDtypeStruct((rows, d), F32),
        grid=(rows // tm,),
        in_specs=[row(a.shape[1]), row(p.shape[1]), row(d), _const_spec(wo.shape),
                  _const_spec(g.shape), _const_spec(b.shape)],
        out_specs=row(d),
        compiler_params=_params("parallel"),
        name="pair_mix",
    )(a, p, y, wo, g, b)


def _fox_sample_kernel(*refs, pages_per_step, n_new, heads):
    pt_ref = refs[0]
    del pt_ref
    qbd_ref, knew_ref, vnew_ref, lnew_ref = refs[1:5]
    k_refs = refs[5:5 + pages_per_step]
    v_refs = refs[5 + pages_per_step:5 + 2 * pages_per_step]
    lf_refs = refs[5 + 2 * pages_per_step:5 + 3 * pages_per_step]
    o_ref, m_ref, l_ref, acc_ref, carry_ref = refs[5 + 3 * pages_per_step:]
    step = pl.program_id(1)
    n_rows = qbd_ref.shape[0]
    qbd = qbd_ref[...]

    @pl.when(step == 0)
    def _():
        lf = lnew_ref[...]
        lane = lax.broadcasted_iota(jnp.int32, lf.shape, 1)
        tok = lax.broadcasted_iota(jnp.int32, lf.shape, 0) // heads
        upto = jnp.where(lax.broadcasted_iota(jnp.int32, (LANES, LANES), 0)
                         <= lax.broadcasted_iota(jnp.int32, (LANES, LANES), 1), 1.0, 0.0).astype(BF16)
        hi, mid, lo = _split3(lf)
        pre = _dot(hi, upto) + _dot(mid, upto) + _dot(lo, upto)
        n_col = jnp.sum(jnp.where(lane == tok, pre, 0.0), axis=1, keepdims=True)
        kn = knew_ref[...].astype(BF16)
        s = _dot_nt(qbd, kn) + n_col - pre[:, 0:SUBLANES]
        key = lax.broadcasted_iota(jnp.int32, s.shape, 1)
        qtok = lax.broadcasted_iota(jnp.int32, s.shape, 0) // heads
        s = jnp.where(key <= qtok, s, NEG_INF)
        m = jnp.max(s, axis=1, keepdims=True)
        p = jnp.exp(s - m)
        m_ref[...] = jnp.broadcast_to(m, m_ref.shape)
        l_ref[...] = jnp.broadcast_to(jnp.sum(p, axis=1, keepdims=True), l_ref.shape)
        acc_ref[...] = _dot(p.astype(BF16), vnew_ref[...].astype(BF16))
        carry_ref[...] = jnp.broadcast_to(n_col, carry_ref.shape)

    src = lax.broadcasted_iota(jnp.int32, (PAGE_SIZE, PAGE_SIZE), 0)
    dst = lax.broadcasted_iota(jnp.int32, (PAGE_SIZE, PAGE_SIZE), 1)
    after = jnp.where(src > dst, 1.0, 0.0).astype(BF16)
    reps = n_rows // heads
    lf_all = jnp.concatenate([r[...] for r in lf_refs], axis=0)
    hi, mid, lo = _split3(lf_all)
    within_all = _dot(hi, after) + _dot(mid, after) + _dot(lo, after)
    total_all = jnp.sum(lf_all, axis=1, keepdims=True)
    carry = carry_ref[...]
    s_parts = []
    for t in range(pages_per_step):
        page = slice(t * heads, (t + 1) * heads)
        bias = carry + jnp.tile(within_all[page], (reps, 1))
        s_parts.append(_dot(qbd, k_refs[t][...].astype(BF16)) + bias)
        carry = carry + jnp.tile(total_all[page], (reps, 1))
    carry_ref[...] = carry
    s = jnp.concatenate(s_parts, axis=1)
    m_prev = m_ref[...]
    m_new = jnp.maximum(m_prev, jnp.max(s, axis=1, keepdims=True))
    p = jnp.exp(s - jnp.tile(m_new, (1, pages_per_step)))
    scale = jnp.exp(m_prev - m_new)
    m_ref[...] = m_new
    l_ref[...] = scale * l_ref[...] + jnp.sum(p, axis=1, keepdims=True)
    pb = p.astype(BF16)
    pv = _dot_nt(pb[:, 0:PAGE_SIZE], v_refs[0][...].astype(BF16))
    for t in range(1, pages_per_step):
        pv = pv + _dot_nt(pb[:, t * PAGE_SIZE:(t + 1) * PAGE_SIZE], v_refs[t][...].astype(BF16))
    acc_ref[...] = jnp.tile(scale, (1, acc_ref.shape[1] // LANES)) * acc_ref[...] + pv

    @pl.when(step == pl.num_programs(1) - 1)
    def _():
        full = acc_ref[...] / jnp.tile(l_ref[...], (1, acc_ref.shape[1] // LANES))
        row_head = lax.broadcasted_iota(jnp.int32, full.shape, 0) % heads
        lane_head = lax.broadcasted_iota(jnp.int32, full.shape, 1) // D_HEAD
        own = jnp.where(row_head == lane_head, full, 0.0)
        o_ref[...] = jnp.sum(own.reshape(n_new, heads, full.shape[1]), axis=1).astype(o_ref.dtype)


def _fox_sample(qbd, knew, vnew, lnew, cache_k, cache_v, cache_lft, page_table, *, heads, n_new,
                pages_per_step):
    nb, n_rows, width = qbd.shape
    n_pages = page_table.shape[1]
    assert n_pages % pages_per_step == 0
    pt = page_table.reshape(-1)

    def page_map(t):
        def index(b, s, pt_ref):
            logical = n_pages - 1 - (s * pages_per_step + t)
            return (pt_ref[b * n_pages + logical], 0, 0)
        return index

    batch3 = lambda shape: pl.BlockSpec((None,) + shape, lambda b, s, pt_ref: (b, 0, 0))
    in_specs = [batch3((n_rows, width)), batch3((SUBLANES, width)), batch3((SUBLANES, width)),
                batch3((n_rows, LANES))]
    in_specs += [pl.BlockSpec((None, width, PAGE_SIZE), page_map(t)) for t in range(pages_per_step)]
    in_specs += [pl.BlockSpec((None, width, PAGE_SIZE), page_map(t)) for t in range(pages_per_step)]
    in_specs += [pl.BlockSpec((None, heads, PAGE_SIZE), page_map(t)) for t in range(pages_per_step)]
    return pl.pallas_call(
        functools.partial(_fox_sample_kernel, pages_per_step=pages_per_step, n_new=n_new, heads=heads),
        out_shape=jax.ShapeDtypeStruct((nb, n_new, width), BF16),
        grid_spec=pltpu.PrefetchScalarGridSpec(
            num_scalar_prefetch=1,
            grid=(nb, n_pages // pages_per_step),
            in_specs=in_specs,
            out_specs=batch3((n_new, width)),
            scratch_shapes=[pltpu.VMEM((n_rows, LANES), F32), pltpu.VMEM((n_rows, LANES), F32),
                            pltpu.VMEM((n_rows, width), F32), pltpu.VMEM((n_rows, LANES), F32)],
        ),
        compiler_params=_params("parallel", "arbitrary"),
        name="fox_sample",
    )(pt, qbd, knew, vnew, lnew, *([cache_k] * pages_per_step), *([cache_v] * pages_per_step),
      *([cache_lft] * pages_per_step))


def _odd_proj_kernel(*refs, tm):
    n_groups = len(DIL_CONFIGS)
    y_ref, w_ref = refs[0], refs[1]
    de_refs = refs[2:2 + 3 * n_groups]
    tail_refs = refs[2 + 3 * n_groups:2 + 5 * n_groups]
    stage_refs = refs[2 + 5 * n_groups:]
    yb = y_ref[...].astype(BF16)
    n_lane_groups = de_refs[0].shape[1]
    gw = n_lane_groups * LANES
    stage = 0
    for g, (window, dil) in enumerate(DIL_CONFIGS):
        for part in range(3):
            col = (g * 3 + part) * gw
            res = _dot(yb, w_ref[:, col:col + gw])
            if part == 0:
                res = res * (D_HEAD ** -0.5)
            dst = de_refs[g * 3 + part]
            if dil == 1:
                for a in range(n_lane_groups):
                    dst[0, a] = res[:, a * LANES:(a + 1) * LANES].astype(BF16)
            else:
                s_ref = stage_refs[stage]
                stage += 1
                for a in range(n_lane_groups):
                    s_ref[a] = res[:, a * LANES:(a + 1) * LANES]
                for r in range(dil):
                    for a in range(n_lane_groups):
                        dst[r, a] = s_ref[a, pl.ds(r, tm // dil, stride=dil), :].astype(BF16)
            if part > 0:
                keep = min(window, tm)
                tail_refs[g * 2 + part - 1][...] = res[tm - keep:, :].T


def _odd_proj(y3, w, *, tm):
    batch, seq_len, d = y3.shape
    gw = w.shape[1] // (3 * len(DIL_CONFIGS))
    n_lane_groups = gw // LANES
    n_tiles = seq_len // tm
    out_shape, out_specs, n_stage = [], [], 0
    for window, dil in DIL_CONFIGS:
        assert tm % (dil * 16) == 0 and (window % tm == 0 or tm % window == 0)
        for _ in range(3):
            out_shape.append(jax.ShapeDtypeStruct((batch, dil, n_lane_groups, seq_len // dil, LANES), BF16))
            out_specs.append(pl.BlockSpec((None, dil, n_lane_groups, tm // dil, LANES),
                                          lambda b, i: (b, 0, 0, i, 0)))
        n_stage += 3 if dil > 1 else 0
    for window, dil in DIL_CONFIGS:
        keep = min(window, tm)
        first = n_tiles - window // keep
        for _ in range(2):
            out_shape.append(jax.ShapeDtypeStruct((batch, gw, window), F32))
            out_specs.append(pl.BlockSpec((None, gw, keep),
                                          lambda b, i, first=first: (b, 0, jnp.maximum(i - first, 0))))
    return pl.pallas_call(
        functools.partial(_odd_proj_kernel, tm=tm),
        out_shape=tuple(out_shape),
        grid=(batch, n_tiles),
        in_specs=[pl.BlockSpec((None, tm, d), lambda b, i: (b, i, 0)),
                  pl.BlockSpec(w.shape, lambda b, i: (0, 0), pipeline_mode=pl.Buffered(1))],
        out_specs=tuple(out_specs),
        scratch_shapes=[pltpu.VMEM((n_lane_groups, tm, LANES), F32)] * n_stage,
        compiler_params=_params("parallel", "arbitrary"),
        name="odd_proj",
    )(y3, w)


def _bmm_nt(a, b):
    return lax.dot_general(a, b, (((2,), (2,)), ((0,), (0,))), preferred_element_type=F32)


def _bmm(a, b):
    return lax.dot_general(a, b, (((2,), (1,)), ((0,), (0,))), preferred_element_type=F32)


def _dil_attn_kernel(q_ref, kh_ref, kc_ref, vh_ref, vc_ref, bfirst_ref, bprev_ref, bcur_ref, o_ref, lse_ref,
                     *, dil, heads, qb):
    nblk = q_ref.shape[2] // qb
    units = [(r, j) for r in range(dil) for j in range(nblk)]
    even_lane = _lane_is_even_head((1, qb, LANES))

    def cur_blocks(ref, g):
        return jnp.concatenate([ref[r, g].reshape(nblk, qb, LANES) for r in range(dil)], axis=0)

    def prev_blocks(cur, halo_ref, g):
        parts = []
        for r in range(dil):
            parts.append(halo_ref[r, g][None])
            if nblk > 1:
                parts.append(cur[r * nblk:(r + 1) * nblk - 1])
        return jnp.concatenate(parts, axis=0)

    for g in range(heads // HEADS_PER_LANE_GROUP):
        q2, kc, vc = cur_blocks(q_ref, g), cur_blocks(kc_ref, g), cur_blocks(vc_ref, g)
        kp, vp = prev_blocks(kc, kh_ref, g), prev_blocks(vc, vh_ref, g)
        outs, lses = [], []
        for sub in range(HEADS_PER_LANE_GROUP):
            h = g * HEADS_PER_LANE_GROUP + sub
            qh = jnp.where(even_lane if sub == 0 else ~even_lane, q2, jnp.zeros_like(q2))
            bias_prev = jnp.stack([bfirst_ref[h] if j == 0 else bprev_ref[h] for _, j in units])
            sp = _bmm_nt(qh, kp) + bias_prev
            sc = _bmm_nt(qh, kc) + bcur_ref[h][None]
            m = jnp.maximum(jnp.max(sp, axis=2, keepdims=True), jnp.max(sc, axis=2, keepdims=True))
            pp = jnp.exp(sp - m)
            pc = jnp.exp(sc - m)
            den = jnp.sum(pp, axis=2, keepdims=True) + jnp.sum(pc, axis=2, keepdims=True)
            pv = _bmm(pp.astype(BF16), vp) + _bmm(pc.astype(BF16), vc)
            outs.append(pv / den)
            lses.append(jnp.broadcast_to(m + jnp.log(den), pv.shape))
        o2 = jnp.where(even_lane, outs[0], outs[1])
        l2 = jnp.where(even_lane, lses[0], lses[1])
        for u, (r, j) in enumerate(units):
            rows = pl.ds(j * qb, qb) if dil == 1 else pl.ds(r + dil * qb * j, qb, stride=dil)
            o_ref[g, rows, :] = o2[u]
            lse_ref[g, rows, :] = l2[u]


def _dil_attention(q, k, v, bias_first, bias_prev, bias_cur, *, heads, qb, units):
    batch, dil, n_lane_groups, slots, _ = q.shape
    nblk = units // dil
    span = qb * nblk
    cur = pl.BlockSpec((None, dil, n_lane_groups, span, LANES), lambda b, i: (b, 0, 0, i, 0))
    halo = pl.BlockSpec((None, dil, n_lane_groups, qb, LANES),
                        lambda b, i: (b, 0, 0, jnp.maximum(i * nblk - 1, 0), 0))
    nat = pl.BlockSpec((None, n_lane_groups, span * dil, LANES), lambda b, i: (b, 0, i, 0))
    out_sds = jax.ShapeDtypeStruct((batch, n_lane_groups, slots * dil, LANES), F32)
    return pl.pallas_call(
        functools.partial(_dil_attn_kernel, dil=dil, heads=heads, qb=qb),
        out_shape=(out_sds, out_sds),
        grid=(batch, slots // span),
        in_specs=[cur, halo, cur, halo, cur,
                  pl.BlockSpec((None, heads, qb, qb), lambda b, i: (jnp.minimum(i, 1), 0, 0, 0)),
                  _const_spec(bias_prev.shape), _const_spec(bias_cur.shape)],
        out_specs=(nat, nat),
        compiler_params=_params("parallel", "arbitrary"),
        name=f"dil_attention_d{dil}",
    )(q, k, k, v, v, bias_first, bias_prev, bias_cur)


def _odd_mix_kernel(*refs, alpha):
    n = len(DIL_CONFIGS)
    o_refs, lse_refs = refs[:n], refs[n:2 * n]
    y_ref, wo_ref, g_ref, b_ref, out_ref = refs[2 * n:]
    merged = []
    for a in range(o_refs[0].shape[0]):
        lses = [r[a] for r in lse_refs]
        top = functools.reduce(jnp.maximum, lses)
        wts = [jnp.exp(l - top) for l in lses]
        num = functools.reduce(jnp.add, [w * r[a] for w, r in zip(wts, o_refs)])
        merged.append((num / functools.reduce(jnp.add, wts)).astype(BF16))
    mix = _dot(jnp.concatenate(merged, axis=-1), wo_ref[...])
    out_ref[...] = _layer_norm(alpha * y_ref[...] + mix, g_ref[...], b_ref[...])


def _odd_mix(outs, lses, y3, wo, g, b, *, alpha, tm):
    batch, seq_len, d = y3.shape
    n_lane_groups = outs[0].shape[1]
    part = pl.BlockSpec((None, n_lane_groups, tm, LANES), lambda b, i: (b, 0, i, 0))
    row = pl.BlockSpec((None, tm, d), lambda b, i: (b, i, 0))
    return pl.pallas_call(
        functools.partial(_odd_mix_kernel, alpha=alpha),
        out_shape=jax.ShapeDtypeStruct((batch, seq_len, d), F32),
        grid=(batch, seq_len // tm),
        in_specs=[part] * (2 * len(outs)) + [row, _const_spec(wo.shape), _const_spec(g.shape),
                                             _const_spec(b.shape)],
        out_specs=row,
        compiler_params=_params("parallel", "parallel"),
        name="odd_mix",
    )(*outs, *lses, y3, wo, g, b)


def _proj_kernel(x_ref, w_ref, o_ref):
    o_ref[...] = _dot(x_ref[...].astype(BF16), w_ref[...])


def _proj(x, w, *, tn):
    rows, d = x.shape
    n = w.shape[1]
    return pl.pallas_call(
        _proj_kernel,
        out_shape=jax.ShapeDtypeStruct((rows, n), F32),
        grid=(n // tn,),
        in_specs=[_const_spec((rows, d)), pl.BlockSpec((d, tn), lambda j: (0, j))],
        out_specs=pl.BlockSpec((rows, tn), lambda j: (0, j)),
        compiler_params=_params("parallel"),
        name="proj",
    )(x, w)


def _dil_sample_kernel(qbd_ref, kc_ref, vc_ref, kn_ref, vn_ref, ktail_ref, vtail_ref, tabc_ref, tabn_ref,
                       o_ref, lse_ref, ko_ref, vo_ref, *, n_new, heads):
    width, buf_len = kc_ref.shape
    qbd = qbd_ref[...]
    chunk = min(buf_len, 512)
    starts = range(0, buf_len, chunk)
    sc = [_dot(qbd, kc_ref[:, c0:c0 + chunk].astype(BF16)) + tabc_ref[:, c0:c0 + chunk] for c0 in starts]
    sn = _dot_nt(qbd, kn_ref[...].astype(BF16)) + tabn_ref[...]
    m = functools.reduce(jnp.maximum, [jnp.max(s, axis=1, keepdims=True) for s in sc + [sn]])
    pn = jnp.exp(sn - m)
    den = jnp.sum(pn, axis=1, keepdims=True)
    pv = _dot(pn.astype(BF16), vn_ref[...].astype(BF16))
    for s, c0 in zip(sc, starts):
        p = jnp.exp(s - m)
        den = den + jnp.sum(p, axis=1, keepdims=True)
        pv = pv + _dot_nt(p.astype(BF16), vc_ref[:, c0:c0 + chunk].astype(BF16))
    full = pv / den
    lse = jnp.broadcast_to(m + jnp.log(den), full.shape)
    own = (lax.broadcasted_iota(jnp.int32, full.shape, 0) % heads
           == lax.broadcasted_iota(jnp.int32, full.shape, 1) // D_HEAD)
    o_ref[...] = jnp.sum(jnp.where(own, full, 0.0).reshape(n_new, heads, width), axis=1)
    lse_ref[...] = jnp.sum(jnp.where(own, lse, 0.0).reshape(n_new, heads, width), axis=1)
    rows = 64
    is_new = lax.broadcasted_iota(jnp.int32, (rows, LANES), 1) >= LANES - n_new
    for src_ref, tail_ref, dst_ref in ((kc_ref, ktail_ref, ko_ref), (vc_ref, vtail_ref, vo_ref)):
        for r0 in range(0, width, rows):
            rolled = pltpu.roll(src_ref[r0:r0 + rows, :], buf_len - n_new, axis=1)
            if buf_len > LANES:
                dst_ref[r0:r0 + rows, 0:buf_len - LANES] = rolled[:, 0:buf_len - LANES]
            dst_ref[r0:r0 + rows, buf_len - LANES:] = jnp.where(is_new, tail_ref[r0:r0 + rows, :],
                                                                rolled[:, buf_len - LANES:])


def _dil_sample(qbd, kc, vc, kn, vn, ktail, vtail, tab_c, tab_n, *, n_new, heads):
    nb, width, buf_len = kc.shape
    n_rows = qbd.shape[1]
    b3 = lambda *shape: pl.BlockSpec((None,) + shape, lambda b: (b,) + (0,) * len(shape))
    small = jax.ShapeDtypeStruct((nb, n_new, width), F32)
    big = jax.ShapeDtypeStruct((nb, width, buf_len), F32)
    return pl.pallas_call(
        functools.partial(_dil_sample_kernel, n_new=n_new, heads=heads),
        out_shape=(small, small, big, big),
        grid=(nb,),
        in_specs=[b3(n_rows, width), b3(width, buf_len), b3(width, buf_len),
                  b3(SUBLANES, width), b3(SUBLANES, width), b3(width, LANES), b3(width, LANES),
                  _const_spec(tab_c.shape), _const_spec(tab_n.shape)],
        out_specs=(b3(n_new, width), b3(n_new, width), b3(width, buf_len), b3(width, buf_len)),
        compiler_params=_params("parallel"),
        name=f"dil_sample_l{buf_len}",
    )(qbd, kc, vc, kn, vn, ktail, vtail, tab_c, tab_n)


def _rel_bucket(dist):
    exact = REL_BUCKETS // 2
    d = jnp.maximum(dist, 1).astype(F32)
    large = exact + (jnp.log(d / exact) / math.log(REL_MAX_DIST / exact) * (REL_BUCKETS - exact)).astype(jnp.int32)
    large = jnp.minimum(large, REL_BUCKETS - 1)
    return jnp.where(dist < exact, dist, large)


def _group_bias(rel_bias, g, window, dil, heads):
    dist = jnp.arange(window // dil + 1) * dil
    onehot = _rel_bucket(dist)[:, None] == jnp.arange(REL_BUCKETS)[None, :]
    cols = rel_bias[:, g * heads:(g + 1) * heads].astype(F32)
    return jnp.sum(jnp.where(onehot[:, :, None], cols[None], 0.0), axis=1).T


def _toeplitz(w, n):
    heads, period = w.shape
    flat = jnp.tile(w, (1, n))[:, :n * (period - 1)]
    return flat.reshape(heads, n, period - 1)[:, :, :n]


def _prompt_bias_tiles(bias, qb):
    heads, n_keys = bias.shape
    assert n_keys == qb + 1
    neg = jnp.full((heads, qb), NEG_INF, F32)
    cur = _toeplitz(jnp.concatenate([bias[:, 0:qb], neg], axis=1), qb)
    prev = _toeplitz(jnp.concatenate([bias[:, qb:qb + 1], neg, bias[:, 1:qb]], axis=1), qb)
    cur, prev = jnp.swapaxes(cur, 1, 2), jnp.swapaxes(prev, 1, 2)
    return jnp.stack([jnp.full_like(prev, NEG_INF), prev]), prev, cur


def _sample_bias_tables(bias, dil, n_new, buf_len):
    heads, n_keys = bias.shape
    n_back = n_keys - 1
    assert n_back * dil == buf_len
    oldest_first = bias[:, :0:-1]
    gaps = jnp.full((heads, n_back, dil - 1), NEG_INF, F32)
    spread = jnp.concatenate([oldest_first[:, :, None], gaps], axis=2).reshape(heads, buf_len)
    neg_col = jnp.full((heads,), NEG_INF, F32)
    tab_c, tab_n = [], []
    for i in range(n_new):
        tab_c.append(jnp.concatenate([jnp.full((heads, i), NEG_INF, F32), spread[:, :buf_len - i]], axis=1))
        in_set = lambda j: j <= i and (i - j) % dil == 0
        tab_n.append(jnp.stack([bias[:, (i - j) // dil] if in_set(j) else neg_col for j in range(SUBLANES)],
                               axis=1))
    return jnp.concatenate(tab_c, axis=0), jnp.concatenate(tab_n, axis=0)


PROMPT_ROW_TILE = 512
ATTN_BLOCK = 512
DIL_SLOT_BLOCK = 128
DIL_UNITS_PER_STEP = 16
PAGES_PER_STEP = 16


def _pad_axis(x, axis, size):
    pad = [(0, 0)] * x.ndim
    pad[axis] = (0, size - x.shape[axis])
    return jnp.pad(x, pad)


def token_major(xt, lead):
    return jnp.transpose(xt.reshape(lead, -1, D_HEAD, xt.shape[-1]), (0, 3, 1, 2))


def kernel(x_prompt, x_sample, cache_fox_k, cache_fox_v, cache_fox_logf, state_pool, cache_dil0_k, cache_dil0_v, cache_dil1_k, cache_dil1_v, cache_dil2_k, cache_dil2_v, page_table, p_prompt, p_sample, w_in_even, b_fgate, pool_w, pool_scale, w_out_even, w_in_odd, w_out_odd, rel_bias, ffn1_wg, ffn1_wu, ffn1_wd, ffn2_wg, ffn2_wu, ffn2_wd, ln_g, ln_b, ple_wg, ple_bg, ple_wp):
    depth = ffn1_wg.shape[0]
    alpha = (2 * depth) ** 0.25
    batch, seq_len, d_model = x_prompt.shape
    nb, n_new, _ = x_sample.shape
    past_len = page_table.shape[1] * PAGE_SIZE
    fox_heads = cache_fox_k.shape[-2]
    fox_w = fox_heads * D_HEAD
    dil_heads = cache_dil0_k.shape[-2]
    dil_w = dil_heads * D_HEAD
    dil_caches_k = (cache_dil0_k, cache_dil1_k, cache_dil2_k)
    dil_caches_v = (cache_dil0_v, cache_dil1_v, cache_dil2_v)
    rows_p, rows_s = batch * seq_len, nb * n_new
    tm = PROMPT_ROW_TILE
    bf = lambda w: w.astype(BF16)
    vec = lambda a: a.reshape(1, -1)

    yp = x_prompt.reshape(rows_p, d_model)
    ys = x_sample.reshape(rows_s, d_model)
    outs = {name: [] for name in ("fkp", "fvp", "flp", "fks", "fvs", "fls", "plp", "pls")}
    dkp, dvp, dks, dvs = ([[] for _ in DIL_CONFIGS] for _ in range(4))

    for i in range(depth):
        w1 = (bf(ffn1_wg[i]), bf(ffn1_wu[i]), bf(ffn1_wd[i]), vec(ln_g[i, 0]), vec(ln_b[i, 0]))
        yp = _ffn(yp, *w1, alpha=alpha, tm=tm)
        ys = _ffn(ys, *w1, alpha=alpha, tm=rows_s)
        g1, b1 = vec(ln_g[i, 1]), vec(ln_b[i, 1])
        if i % 2 == 0:
            e = i // 2
            w_in = w_in_even[e]
            wqkv = bf(w_in[:, :3 * fox_w])
            wf = bf(_pad_axis(w_in[:, 3 * fox_w:3 * fox_w + fox_heads], 1, LANES))
            bfg = _pad_axis(vec(b_fgate[e]), 1, LANES)
            wu = bf(w_in[:, 3 * fox_w + fox_heads:])
            wpool, pscale, wo = bf(pool_w[e]), vec(pool_scale[e]), bf(w_out_even[e])
            qa, ka, va, kt, vt, lft, u = _even_proj(yp.reshape(batch, seq_len, d_model), wqkv, wf, bfg, wu,
                                                    heads=fox_heads, tm=tm, prompt=True)
            att = _fox_attention(qa, ka, va, heads=fox_heads, tq=ATTN_BLOCK).reshape(rows_p, fox_w)
            u = u.reshape(rows_p, -1)
            yp_next = _even_mix(att, u, yp, wpool, pscale, wo, g1, b1, alpha=alpha, seq_len=seq_len, tm=tm)
            outs["fkp"].append(token_major(kt, batch))
            outs["fvp"].append(token_major(vt, batch))
            outs["flp"].append(jnp.swapaxes(lft, 1, 2))
            outs["plp"].append(u.reshape(batch, seq_len, -1)[:, seq_len - POOL_BUF:])
            qs, ks, vs, lfts, us = _even_proj(ys[None], wqkv, wf, bfg, wu, heads=fox_heads, tm=rows_s,
                                              prompt=False)
            head_mask = (jnp.arange(fox_w)[None, :] // D_HEAD == jnp.arange(fox_heads)[:, None]).astype(BF16)
            qbd = (qs.reshape(nb, n_new, 1, fox_w) * head_mask).reshape(nb, n_new * fox_heads, fox_w)
            knew = _pad_axis(ks.reshape(nb, n_new, fox_w), 1, SUBLANES)
            vnew = _pad_axis(vs.reshape(nb, n_new, fox_w), 1, SUBLANES)
            lf3 = jnp.swapaxes(lfts.reshape(fox_heads, nb, n_new), 0, 1)
            lnew = jnp.broadcast_to(lf3[:, None], (nb, n_new, fox_heads, n_new)).reshape(nb, n_new * fox_heads, n_new)
            lnew = _pad_axis(lnew, 2, LANES)
            n_phys = cache_fox_k.shape[1]
            page_t = lambda c: jnp.transpose(c, (0, 2, 3, 1)).reshape(n_phys, fox_w, PAGE_SIZE)
            att_s = _fox_sample(qbd, knew, vnew, lnew, page_t(cache_fox_k[e]), page_t(cache_fox_v[e]),
                                jnp.swapaxes(cache_fox_logf[e], 1, 2), page_table,
                                heads=fox_heads, n_new=n_new, pages_per_step=PAGES_PER_STEP)
            ctx_s = jnp.concatenate([state_pool[e].astype(F32), us.reshape(nb, n_new, -1)], axis=1)
            pool_s = _pool_sample(jnp.swapaxes(ctx_s, 0, 1), wpool, pscale, pos0=past_len, n_new=n_new)
            pool_s = jnp.swapaxes(pool_s, 0, 1).reshape(rows_s, -1)
            ys_next = _pair_mix(att_s.reshape(rows_s, fox_w), pool_s, ys, wo, g1, b1, alpha=alpha, tm=rows_s)
            outs["fks"].append(ks.reshape(nb, n_new, fox_heads, D_HEAD))
            outs["fvs"].append(vs.reshape(nb, n_new, fox_heads, D_HEAD))
            outs["fls"].append(jnp.swapaxes(lf3, 1, 2))
            outs["pls"].append(ctx_s[:, -POOL_BUF:])
        else:
            o = i // 2
            w_in, wo = bf(w_in_odd[o]), bf(w_out_odd[o])
            biases = [_group_bias(rel_bias, g, window, dil, dil_heads) for g, (window, dil) in enumerate(DIL_CONFIGS)]
            proj = _odd_proj(yp.reshape(batch, seq_len, d_model), w_in, tm=tm)
            n_groups = len(DIL_CONFIGS)
            o_parts, lse_parts = [], []
            for g, (window, dil) in enumerate(DIL_CONFIGS):
                qd, kd, vd = proj[3 * g:3 * g + 3]
                bias_first, bias_prev, bias_cur = _prompt_bias_tiles(biases[g], DIL_SLOT_BLOCK)
                og, lg = _dil_attention(qd, kd, vd, bias_first, bias_prev, bias_cur, heads=dil_heads,
                                        qb=DIL_SLOT_BLOCK, units=DIL_UNITS_PER_STEP)
                o_parts.append(og)
                lse_parts.append(lg)
                dkp[g].append(token_major(proj[3 * n_groups + 2 * g], batch))
                dvp[g].append(token_major(proj[3 * n_groups + 2 * g + 1], batch))
            yp_next = _odd_mix(o_parts, lse_parts, yp.reshape(batch, seq_len, d_model), wo, g1, b1,
                               alpha=alpha, tm=tm).reshape(rows_p, d_model)
            lane_major = lambda a: jnp.swapaxes(a.reshape(rows_s, dil_w // LANES, LANES), 0, 1)[None]
            proj_s = _proj(ys, w_in, tn=3 * dil_w).reshape(nb, n_new, n_groups, 3, dil_w)
            head_mask = (jnp.arange(dil_w)[None, :] // D_HEAD == jnp.arange(dil_heads)[:, None]).astype(F32)
            o_parts, lse_parts = [], []
            for g, (window, dil) in enumerate(DIL_CONFIGS):
                qg = proj_s[:, :, g, 0] * (D_HEAD ** -0.5)
                qbd = (qg[:, :, None, :] * head_mask).astype(BF16).reshape(nb, n_new * dil_heads, dil_w)
                k_new, v_new = proj_s[:, :, g, 1], proj_s[:, :, g, 2]
                buf_len = dil_caches_k[g].shape[2]
                pos_minor = lambda c: jnp.transpose(c, (0, 2, 3, 1)).reshape(nb, dil_w, buf_len)
                last_lanes = lambda x: jnp.pad(jnp.swapaxes(x, 1, 2), ((0, 0), (0, 0), (LANES - n_new, 0)))
                tab_c, tab_n = _sample_bias_tables(biases[g], dil, n_new, buf_len)
                og, lg, k_roll, v_roll = _dil_sample(
                    qbd, pos_minor(dil_caches_k[g][o]), pos_minor(dil_caches_v[g][o]),
                    _pad_axis(k_new, 1, SUBLANES), _pad_axis(v_new, 1, SUBLANES),
                    last_lanes(k_new), last_lanes(v_new), tab_c, tab_n, n_new=n_new, heads=dil_heads)
                o_parts.append(lane_major(og))
                lse_parts.append(lane_major(lg))
                dks[g].append(token_major(k_roll, nb))
                dvs[g].append(token_major(v_roll, nb))
            ys_next = _odd_mix(o_parts, lse_parts, ys[None], wo, g1, b1, alpha=alpha,
                               tm=rows_s).reshape(rows_s, d_model)
        yp, ys = yp_next, ys_next
        w2 = (bf(ffn2_wg[i]), bf(ffn2_wu[i]), bf(ffn2_wd[i]), vec(ln_g[i, 2]), vec(ln_b[i, 2]))
        ple_w = (bf(ple_wg[i]), vec(ple_bg[i]), bf(ple_wp[i]))
        yp = _ffn(yp, *w2, alpha=alpha, tm=tm, ple=(p_prompt[i].reshape(rows_p, -1),) + ple_w)
        ys = _ffn(ys, *w2, alpha=alpha, tm=rows_s, ple=(p_sample[i].reshape(rows_s, -1),) + ple_w)

    stack = lambda parts: jnp.stack(parts)
    result = [yp.reshape(batch, seq_len, d_model), ys.reshape(nb, n_new, d_model)]
    result += [stack(outs[name]) for name in ("fkp", "fvp", "flp", "fks", "fvs", "fls", "plp", "pls")]
    for g in range(len(DIL_CONFIGS)):
        result += [stack(dkp[g]), stack(dvp[g])]
    for g in range(len(DIL_CONFIGS)):
        result += [stack(dks[g]), stack(dvs[g])]
    return tuple(result)
```

```python
import functools
import math

import jax
import jax.numpy as jnp
import numpy as np
from jax import lax
from jax.experimental import pallas as pl
from jax.experimental.pallas import tpu as pltpu

F32 = jnp.float32
BF16 = jnp.bfloat16

D_HEAD = 64
POOL_WINDOWS = (2, 4, 8, 16)
POOL_BUF = max(POOL_WINDOWS) - 1
DIL_CONFIGS = ((128, 1), (512, 4), (2048, 16))
REL_BUCKETS = 32
REL_MAX_DIST = 2048
LN_EPS = 1e-5
NEG_INF = -1e30
PAGE_SIZE = 128

LANES = 128
SUBLANES = 8
VMEM_LIMIT_BYTES = 56 * 1024 * 1024

HEADS_PER_LANE_GROUP = LANES // D_HEAD
HALO_ROWS = 16


def _params(*semantics):
    return pltpu.CompilerParams(dimension_semantics=semantics, vmem_limit_bytes=VMEM_LIMIT_BYTES)


def _dot(a, b):
    return jnp.dot(a, b, preferred_element_type=F32)


def _dot_nt(a, b):
    return lax.dot_general(a, b, (((1,), (1,)), ((), ())), preferred_element_type=F32)


def _layer_norm(z, g, b):
    mu = jnp.mean(z, axis=-1, keepdims=True)
    zc = z - mu
    var = jnp.mean(zc * zc, axis=-1, keepdims=True)
    return zc * lax.rsqrt(var + LN_EPS) * g + b


def _log_sigmoid(x):
    return jnp.minimum(x, 0.0) - jnp.log1p(jnp.exp(-jnp.abs(x)))


def _split3(x):
    hi = x.astype(BF16)
    r1 = x - hi.astype(F32)
    mid = r1.astype(BF16)
    lo = (r1 - mid.astype(F32)).astype(BF16)
    return hi, mid, lo


def _lane_is_even_head(shape):
    return lax.broadcasted_iota(jnp.int32, shape, len(shape) - 1) % LANES < D_HEAD


def _const_spec(shape):
    zeros = (0,) * len(shape)
    return pl.BlockSpec(shape, lambda *_: zeros)


def _ffn_kernel(*refs, alpha, ff_chunk, with_ple):
    if with_ple:
        (x_ref, wg_ref, wu_ref, wd_ref, g_ref, b_ref, p_ref, pwg_ref, pbg_ref, pwp_ref,
         o_ref, h_ref) = refs
    else:
        x_ref, wg_ref, wu_ref, wd_ref, g_ref, b_ref, o_ref, h_ref = refs
    x = x_ref[...]
    xb = x.astype(BF16)
    d_ff = wg_ref.shape[1]
    chunks = [slice(c * ff_chunk, (c + 1) * ff_chunk) for c in range(d_ff // ff_chunk)]
    for cols in chunks:
        gate = _dot(xb, wg_ref[:, cols].astype(BF16))
        up = _dot(xb, wu_ref[:, cols].astype(BF16))
        h_ref[:, cols] = (gate * jax.nn.sigmoid(gate) * up).astype(BF16)
    down = functools.reduce(jnp.add, [_dot(h_ref[:, cols], wd_ref[cols, :].astype(BF16)) for cols in chunks])
    y = _layer_norm(alpha * x + 0.5 * down, g_ref[...], b_ref[...])
    if with_ple:
        gate = jax.nn.sigmoid(_dot(y.astype(BF16), pwg_ref[...].astype(BF16)) + pbg_ref[...])
        y = y + gate * _dot(p_ref[...].astype(BF16), pwp_ref[...].astype(BF16))
    o_ref[...] = y


def _ffn(x, wg, wu, wd, g, b, *, layer, alpha, tm, ple=None):
    rows, d = x.shape
    d_ff = wg.shape[2]
    ff_chunk = 256
    assert rows % tm == 0 and d_ff % ff_chunk == 0
    row_spec = pl.BlockSpec((tm, d), lambda i: (i, 0))
    resident = lambda shape: pl.BlockSpec((None,) + shape, lambda i: (layer, 0, 0), pipeline_mode=pl.Buffered(1))
    in_specs = [row_spec, resident((d, d_ff)), resident((d, d_ff)), resident((d_ff, d)),
                _const_spec((1, d)), _const_spec((1, d))]
    args = [x, wg, wu, wd, g, b]
    if ple is not None:
        p, pwg, pbg, pwp = ple
        in_specs += [pl.BlockSpec((None, tm, p.shape[2]), lambda i: (layer, i, 0)), resident(pwg.shape[1:]),
                     _const_spec((1, d)), resident(pwp.shape[1:])]
        args += [p, pwg, pbg, pwp]
    return pl.pallas_call(
        functools.partial(_ffn_kernel, alpha=alpha, ff_chunk=ff_chunk, with_ple=ple is not None),
        out_shape=jax.ShapeDtypeStruct((rows, d), F32),
        grid=(rows // tm,),
        in_specs=in_specs,
        out_specs=row_spec,
        scratch_shapes=[pltpu.VMEM((tm, d_ff), BF16)],
        compiler_params=_params("parallel"),
        name="ffn_ple" if ple is not None else "ffn",
    )(*args)


N_SPLIT = 3
DECAY_LANE = D_HEAD


def _head_slots(x):
    low = lax.broadcasted_iota(jnp.int32, (x.shape[0], LANES), 1) < D_HEAD
    slots = []
    for g in range(x.shape[1] // LANES):
        pair = x[:, g * LANES:(g + 1) * LANES]
        slots.append(jnp.where(low, pair, 0.0))
        slots.append(jnp.where(low, pltpu.roll(pair, D_HEAD, axis=1), 0.0))
    return slots


def _even_proj_kernel(*refs, prompt):
    if prompt:
        (y_ref, wqkv_ref, wf_ref, bf_ref, wu_ref, place_q_ref, place_k_ref, ones_q_ref, ones_k_ref, ones_v_ref,
         qa_ref, ka_ref, va_ref, kt_ref, vt_ref, logft_ref, u_ref, carry_ref) = refs
    else:
        y_ref, wqkv_ref, wf_ref, bf_ref, wu_ref, q_ref, k_ref, v_ref, logft_ref, u_ref = refs
    yb = y_ref[...].astype(BF16)
    hw = wqkv_ref.shape[1] // 3
    heads = logft_ref.shape[0]
    q = _dot(yb, wqkv_ref[:, 0:hw]) * (D_HEAD ** -0.5)
    k = _dot(yb, wqkv_ref[:, hw:2 * hw])
    v = _dot(yb, wqkv_ref[:, 2 * hw:3 * hw])
    logf = _log_sigmoid(_dot(yb, wf_ref[...]) + bf_ref[...])
    logft_ref[...] = logf.T[:heads, :]
    u_ref[...] = _dot(yb, wu_ref[...])
    if not prompt:
        q_ref[...] = q.astype(BF16)
        k_ref[...] = k
        v_ref[...] = v
        return
    tm = logf.shape[0]
    kt_ref[...] = k.T
    vt_ref[...] = v.T
    @pl.when(pl.program_id(1) == 0)
    def _():
        carry_ref[...] = jnp.zeros_like(carry_ref)

    tri = jnp.where(lax.broadcasted_iota(jnp.int32, (tm, tm), 0) >= lax.broadcasted_iota(jnp.int32, (tm, tm), 1),
                    1.0, 0.0).astype(BF16)
    c = functools.reduce(jnp.add, [_dot(tri, part) for part in _split3(logf)]) + carry_ref[0:1, :]
    carry_ref[...] = jnp.broadcast_to(c[tm - 1:tm, :], carry_ref.shape)
    c_terms = jnp.concatenate(_split3(c), axis=1)
    decay_q = _dot(c_terms, place_q_ref[...]) + ones_q_ref[...]
    decay_k = _dot(c_terms, place_k_ref[...]) + ones_k_ref[...]
    for h, (qs, ks, vs) in enumerate(zip(_head_slots(q), _head_slots(k), _head_slots(v))):
        lanes = slice(h * LANES, (h + 1) * LANES)
        qa_ref[:, lanes] = (qs + decay_q[:, lanes]).astype(BF16)
        ka_ref[:, lanes] = (ks + decay_k[:, lanes]).astype(BF16)
        va_ref[:, lanes] = (vs + ones_v_ref[:, lanes]).astype(BF16)


def _decay_placement(heads):
    place_q = np.zeros((N_SPLIT * LANES, heads * LANES), np.float32)
    place_k = np.zeros_like(place_q)
    ones_q = np.zeros((1, heads * LANES), np.float32)
    ones_k = np.zeros_like(ones_q)
    ones_v = np.zeros_like(ones_q)
    for h in range(heads):
        base = h * LANES + DECAY_LANE
        for j in range(N_SPLIT):
            place_q[j * LANES + h, base + j] = 1.0
            place_k[j * LANES + h, base + N_SPLIT + j] = -1.0
        ones_q[0, base + N_SPLIT:base + 2 * N_SPLIT] = 1.0
        ones_k[0, base:base + N_SPLIT] = 1.0
        ones_v[0, base:(h + 1) * LANES] = 1.0
    return (jnp.asarray(place_q, BF16), jnp.asarray(place_k, BF16), jnp.asarray(ones_q), jnp.asarray(ones_k),
            jnp.asarray(ones_v))


def _even_proj(y3, wqkv, wf, bf, wu, *, heads, tm, prompt):
    batch, seq_len, d = y3.shape
    hw = wqkv.shape[1] // 3
    pw = wu.shape[1]
    row = lambda w: pl.BlockSpec((None, tm, w), lambda b, i: (b, i, 0))
    col = lambda h: pl.BlockSpec((None, h, tm), lambda b, i: (b, 0, i))
    nat = lambda w, dt: jax.ShapeDtypeStruct((batch, seq_len, w), dt)
    args = [y3, wqkv, wf, bf, wu]
    tail_shape = (jax.ShapeDtypeStruct((batch, heads, seq_len), F32), nat(pw, F32))
    tail_specs = (col(heads), row(pw))
    if prompt:
        args += list(_decay_placement(heads))
        slot_w = heads * LANES
        kv_t = jax.ShapeDtypeStruct((batch, hw, seq_len), F32)
        out_shape = (nat(slot_w, BF16),) * 3 + (kv_t, kv_t) + tail_shape
        out_specs = (row(slot_w),) * 3 + (col(hw), col(hw)) + tail_specs
        scratch = [pltpu.VMEM((SUBLANES, LANES), F32)]
    else:
        out_shape = (nat(hw, BF16), nat(hw, F32), nat(hw, F32)) + tail_shape
        out_specs = (row(hw),) * 3 + tail_specs
        scratch = []
    return pl.pallas_call(
        functools.partial(_even_proj_kernel, prompt=prompt),
        out_shape=out_shape,
        grid=(batch, seq_len // tm),
        in_specs=[row(d)] + [_const_spec(a.shape) for a in args[1:]],
        out_specs=out_specs,
        scratch_shapes=scratch,
        compiler_params=_params("parallel", "arbitrary"),
        name="even_proj",
    )(*args)


SOFTMAX_ROWS = 64


def _fox_attn_kernel(qi_ref, ki_ref, q_ref, k_ref, v_ref, o_ref, m_ref, acc_ref, s_ref, p_ref, scale_ref,
                     *, heads):
    pair = pl.program_id(1)
    qi = qi_ref[pair]
    ki = ki_ref[pair]
    tq = q_ref.shape[0]
    tk = k_ref.shape[0]

    @pl.when(ki == 0)
    def _():
        m_ref[...] = jnp.full_like(m_ref, NEG_INF)
        acc_ref[...] = jnp.zeros_like(acc_ref)

    def step(masked):
        for h in range(heads):
            lanes = slice(h * LANES, (h + 1) * LANES)
            buf = h % 2
            s_ref[buf] = _dot_nt(q_ref[:, lanes], k_ref[:, lanes])
            for r0 in range(0, tq, SOFTMAX_ROWS):
                rows = slice(r0, r0 + SOFTMAX_ROWS)
                s = s_ref[buf, rows, :]
                if masked:
                    causal = (lax.broadcasted_iota(jnp.int32, s.shape, 1)
                              <= r0 + lax.broadcasted_iota(jnp.int32, s.shape, 0))
                    s = jnp.where(causal, s, NEG_INF)
                m_prev = m_ref[h, rows, :]
                m_new = jnp.maximum(m_prev, jnp.max(s, axis=1, keepdims=True))
                p_ref[buf, rows, :] = jnp.exp(s - jnp.tile(m_new, (1, tk // LANES))).astype(BF16)
                m_ref[h, rows, :] = m_new
                scale_ref[buf, rows, :] = jnp.exp(m_prev - m_new)
            acc_ref[h] = scale_ref[buf] * acc_ref[h] + _dot(p_ref[buf], v_ref[:, lanes])

    @pl.when(ki < qi)
    def _():
        step(masked=False)

    @pl.when(ki == qi)
    def _():
        step(masked=True)
        low = lax.broadcasted_iota(jnp.int32, (tq, LANES), 1) < D_HEAD
        for g in range(heads // HEADS_PER_LANE_GROUP):
            normed = []
            for sub in range(HEADS_PER_LANE_GROUP):
                acc = acc_ref[g * HEADS_PER_LANE_GROUP + sub]
                normed.append(acc / pltpu.roll(acc, D_HEAD, axis=1))
            out = jnp.where(low, normed[0], pltpu.roll(normed[1], D_HEAD, axis=1))
            o_ref[:, g * LANES:(g + 1) * LANES] = out.astype(o_ref.dtype)


def _fox_attention(q, k, v, *, heads, tq):
    batch, seq_len, slot_w = q.shape
    nq = seq_len // tq
    qi_tab = np.concatenate([np.full(i + 1, i) for i in range(nq)]).astype(np.int32)
    ki_tab = np.concatenate([np.arange(i + 1) for i in range(nq)]).astype(np.int32)
    q_map = lambda b, p, qi, ki: (b, qi[p], 0)
    k_map = lambda b, p, qi, ki: (b, ki[p], 0)
    return pl.pallas_call(
        functools.partial(_fox_attn_kernel, heads=heads),
        out_shape=jax.ShapeDtypeStruct((batch, seq_len, heads * D_HEAD), BF16),
        grid_spec=pltpu.PrefetchScalarGridSpec(
            num_scalar_prefetch=2,
            grid=(batch, len(qi_tab)),
            in_specs=[pl.BlockSpec((None, tq, slot_w), q_map), pl.BlockSpec((None, tq, slot_w), k_map),
                      pl.BlockSpec((None, tq, slot_w), k_map)],
            out_specs=pl.BlockSpec((None, tq, heads * D_HEAD), q_map),
            scratch_shapes=[pltpu.VMEM((heads, tq, LANES), F32), pltpu.VMEM((heads, tq, LANES), F32),
                            pltpu.VMEM((2, tq, tq), F32), pltpu.VMEM((2, tq, tq), BF16),
                            pltpu.VMEM((2, tq, LANES), F32)],
        ),
        compiler_params=_params("parallel", "arbitrary"),
        name="fox_attention",
    )(jnp.asarray(qi_tab), jnp.asarray(ki_tab), q, k, v)


def _pool_groups(ctx_ref, n_avail, w_pool_ref, scale_ref, rows, halo):
    gw = w_pool_ref.shape[1]
    outs = []
    for g, w in enumerate(POOL_WINDOWS):
        lanes = slice(g * gw, (g + 1) * gw)
        cur = ctx_ref[pl.ds(halo, rows), lanes]
        win = cur
        for j in range(1, w):
            win = win + ctx_ref[pl.ds(halo - j, rows), lanes]
        mean = win / jnp.minimum(float(w), n_avail)
        outs.append(_dot((mean - cur).astype(BF16), w_pool_ref[g]))
    return jnp.concatenate(outs, axis=-1) * scale_ref[...]


def _even_mix_kernel(att_ref, u_ref, halo_ref, y_ref, wpool_ref, pscale_ref, wo_ref, g_ref, b_ref,
                     o_ref, ctx_ref, *, alpha, blocks_per_seq):
    tm = u_ref.shape[0]
    fw = att_ref.shape[1]
    blk = pl.program_id(0) % blocks_per_seq
    ctx_ref[0:HALO_ROWS, :] = jnp.where(blk == 0, 0.0, halo_ref[...])
    ctx_ref[HALO_ROWS:HALO_ROWS + tm, :] = u_ref[...]
    pos = blk * tm + lax.broadcasted_iota(jnp.int32, (tm, 1), 0)
    n_avail = (pos + 1).astype(F32)
    pool = _pool_groups(ctx_ref, n_avail, wpool_ref, pscale_ref, tm, HALO_ROWS)
    mix = _dot(att_ref[...], wo_ref[0:fw, :]) + _dot(pool.astype(BF16), wo_ref[fw:, :])
    o_ref[...] = _layer_norm(alpha * y_ref[...] + mix, g_ref[...], b_ref[...])


def _even_mix(att, u, y, wpool, pscale, wo, g, b, *, alpha, seq_len, tm):
    rows, d = y.shape
    fw, pw = att.shape[1], u.shape[1]
    halo_blocks = tm // HALO_ROWS
    row = lambda w: pl.BlockSpec((tm, w), lambda i: (i, 0))
    return pl.pallas_call(
        functools.partial(_even_mix_kernel, alpha=alpha, blocks_per_seq=seq_len // tm),
        out_shape=jax.ShapeDtypeStruct((rows, d), F32),
        grid=(rows // tm,),
        in_specs=[row(fw), row(pw),
                  pl.BlockSpec((HALO_ROWS, pw), lambda i: (jnp.maximum(i * halo_blocks - 1, 0), 0)),
                  row(d), _const_spec(wpool.shape), _const_spec(pscale.shape), _const_spec(wo.shape),
                  _const_spec(g.shape), _const_spec(b.shape)],
        out_specs=row(d),
        scratch_shapes=[pltpu.VMEM((HALO_ROWS + tm, pw), F32)],
        compiler_params=_params("parallel"),
        name="even_mix",
    )(att, u, u, y, wpool, pscale, wo, g, b)


def _pool_sample_kernel(ctx_ref, wpool_ref, pscale_ref, o_ref, *, pos0, n_new):
    gw = wpool_ref.shape[1]
    for i in range(n_new):
        n_avail = float(pos0 + i + 1)
        outs = []
        for g, w in enumerate(POOL_WINDOWS):
            lanes = slice(g * gw, (g + 1) * gw)
            cur = ctx_ref[POOL_BUF + i, :, lanes]
            win = cur
            for j in range(1, w):
                win = win + ctx_ref[POOL_BUF + i - j, :, lanes]
            mean = win / min(float(w), n_avail)
            outs.append(_dot((mean - cur).astype(BF16), wpool_ref[g]))
        o_ref[i] = jnp.concatenate(outs, axis=-1) * pscale_ref[...]


def _pool_sample(ctx_tm, wpool, pscale, *, pos0, n_new):
    steps, nb, c = ctx_tm.shape
    return pl.pallas_call(
        functools.partial(_pool_sample_kernel, pos0=pos0, n_new=n_new),
        out_shape=jax.ShapeDtypeStruct((n_new, nb, c), F32),
        grid=(1,),
        in_specs=[_const_spec(ctx_tm.shape), _const_spec(wpool.shape), _const_spec(pscale.shape)],
        out_specs=_const_spec((n_new, nb, c)),
        compiler_params=_params("arbitrary"),
        name="pool_sample",
    )(ctx_tm, wpool, pscale)


def _pair_mix_kernel(a_ref, p_ref, y_ref, wo_ref, g_ref, b_ref, o_ref, *, alpha):
    fw = a_ref.shape[1]
    mix = _dot(a_ref[...], wo_ref[0:fw, :]) + _dot(p_ref[...].astype(BF16), wo_ref[fw:, :])
    o_ref[...] = _layer_norm(alpha * y_ref[...] + mix, g_ref[...], b_ref[...])


def _pair_mix(a, p, y, wo, g, b, *, alpha, tm):
    rows, d = y.shape
    row = lambda w: pl.BlockSpec((tm, w), lambda i: (i, 0))
    return pl.pallas_call(
        functools.partial(_pair_mix_kernel, alpha=alpha),
        out_shape=jax.ShapeDtypeStruct((rows, d), F32),
        grid=(rows // tm,),
        in_specs=[row(a.shape[1]), row(p.shape[1]), row(d), _const_spec(wo.shape),
                  _const_spec(g.shape), _const_spec(b.shape)],
        out_specs=row(d),
        compiler_params=_params("parallel"),
        name="pair_mix",
    )(a, p, y, wo, g, b)


def _fox_sample_kernel(*refs, pages_per_step, n_new, heads):
    pt_ref = refs[0]
    del pt_ref
    qbd_ref, knew_ref, vnew_ref, lnew_ref = refs[1:5]
    k_refs = refs[5:5 + pages_per_step]
    v_refs = refs[5 + pages_per_step:5 + 2 * pages_per_step]
    lf_refs = refs[5 + 2 * pages_per_step:5 + 3 * pages_per_step]
    o_ref, m_ref, l_ref, acc_ref, carry_ref = refs[5 + 3 * pages_per_step:]
    step = pl.program_id(1)
    n_rows = qbd_ref.shape[0]
    qbd = qbd_ref[...]

    @pl.when(step == 0)
    def _():
        lf = lnew_ref[...]
        lane = lax.broadcasted_iota(jnp.int32, lf.shape, 1)
        tok = lax.broadcasted_iota(jnp.int32, lf.shape, 0) // heads
        upto = jnp.where(lax.broadcasted_iota(jnp.int32, (LANES, LANES), 0)
                         <= lax.broadcasted_iota(jnp.int32, (LANES, LANES), 1), 1.0, 0.0).astype(BF16)
        hi, mid, lo = _split3(lf)
        pre = _dot(hi, upto) + _dot(mid, upto) + _dot(lo, upto)
        n_col = jnp.sum(jnp.where(lane == tok, pre, 0.0), axis=1, keepdims=True)
        kn = knew_ref[...].astype(BF16)
        s = _dot_nt(qbd, kn) + n_col - pre[:, 0:SUBLANES]
        key = lax.broadcasted_iota(jnp.int32, s.shape, 1)
        qtok = lax.broadcasted_iota(jnp.int32, s.shape, 0) // heads
        s = jnp.where(key <= qtok, s, NEG_INF)
        m = jnp.max(s, axis=1, keepdims=True)
        p = jnp.exp(s - m)
        m_ref[...] = jnp.broadcast_to(m, m_ref.shape)
        l_ref[...] = jnp.broadcast_to(jnp.sum(p, axis=1, keepdims=True), l_ref.shape)
        acc_ref[...] = _dot(p.astype(BF16), vnew_ref[...].astype(BF16))
        carry_ref[...] = jnp.broadcast_to(n_col, carry_ref.shape)

    src = lax.broadcasted_iota(jnp.int32, (PAGE_SIZE, PAGE_SIZE), 0)
    dst = lax.broadcasted_iota(jnp.int32, (PAGE_SIZE, PAGE_SIZE), 1)
    after = jnp.where(src > dst, 1.0, 0.0).astype(BF16)
    reps = n_rows // heads
    lf_all = jnp.concatenate([r[...] for r in lf_refs], axis=0)
    hi, mid, lo = _split3(lf_all)
    within_all = _dot(hi, after) + _dot(mid, after) + _dot(lo, after)
    total_all = jnp.sum(lf_all, axis=1, keepdims=True)
    carry = carry_ref[...]
    s_parts = []
    for t in range(pages_per_step):
        page = slice(t * heads, (t + 1) * heads)
        bias = carry + jnp.tile(within_all[page], (reps, 1))
        s_parts.append(_dot(qbd, k_refs[t][...].astype(BF16)) + bias)
        carry = carry + jnp.tile(total_all[page], (reps, 1))
    carry_ref[...] = carry
    s = jnp.concatenate(s_parts, axis=1)
    m_prev = m_ref[...]
    m_new = jnp.maximum(m_prev, jnp.max(s, axis=1, keepdims=True))
    p = jnp.exp(s - jnp.tile(m_new, (1, pages_per_step)))
    scale = jnp.exp(m_prev - m_new)
    m_ref[...] = m_new
    l_ref[...] = scale * l_ref[...] + jnp.sum(p, axis=1, keepdims=True)
    pb = p.astype(BF16)
    pv = _dot_nt(pb[:, 0:PAGE_SIZE], v_refs[0][...].astype(BF16))
    for t in range(1, pages_per_step):
        pv = pv + _dot_nt(pb[:, t * PAGE_SIZE:(t + 1) * PAGE_SIZE], v_refs[t][...].astype(BF16))
    acc_ref[...] = jnp.tile(scale, (1, acc_ref.shape[1] // LANES)) * acc_ref[...] + pv

    @pl.when(step == pl.num_programs(1) - 1)
    def _():
        full = acc_ref[...] / jnp.tile(l_ref[...], (1, acc_ref.shape[1] // LANES))
        row_head = lax.broadcasted_iota(jnp.int32, full.shape, 0) % heads
        lane_head = lax.broadcasted_iota(jnp.int32, full.shape, 1) // D_HEAD
        own = jnp.where(row_head == lane_head, full, 0.0)
        o_ref[...] = jnp.sum(own.reshape(n_new, heads, full.shape[1]), axis=1).astype(o_ref.dtype)


def _fox_sample(qbd, knew, vnew, lnew, cache_k, cache_v, cache_lft, page_table, *, heads, n_new,
                pages_per_step):
    nb, n_rows, width = qbd.shape
    n_pages = page_table.shape[1]
    assert n_pages % pages_per_step == 0
    pt = page_table.reshape(-1)

    def page_map(t):
        def index(b, s, pt_ref):
            logical = n_pages - 1 - (s * pages_per_step + t)
            return (pt_ref[b * n_pages + logical], 0, 0)
        return index

    batch3 = lambda shape: pl.BlockSpec((None,) + shape, lambda b, s, pt_ref: (b, 0, 0))
    in_specs = [batch3((n_rows, width)), batch3((SUBLANES, width)), batch3((SUBLANES, width)),
                batch3((n_rows, LANES))]
    in_specs += [pl.BlockSpec((None, width, PAGE_SIZE), page_map(t)) for t in range(pages_per_step)]
    in_specs += [pl.BlockSpec((None, width, PAGE_SIZE), page_map(t)) for t in range(pages_per_step)]
    in_specs += [pl.BlockSpec((None, heads, PAGE_SIZE), page_map(t)) for t in range(pages_per_step)]
    return pl.pallas_call(
        functools.partial(_fox_sample_kernel, pages_per_step=pages_per_step, n_new=n_new, heads=heads),
        out_shape=jax.ShapeDtypeStruct((nb, n_new, width), BF16),
        grid_spec=pltpu.PrefetchScalarGridSpec(
            num_scalar_prefetch=1,
            grid=(nb, n_pages // pages_per_step),
            in_specs=in_specs,
            out_specs=batch3((n_new, width)),
            scratch_shapes=[pltpu.VMEM((n_rows, LANES), F32), pltpu.VMEM((n_rows, LANES), F32),
                            pltpu.VMEM((n_rows, width), F32), pltpu.VMEM((n_rows, LANES), F32)],
        ),
        compiler_params=_params("parallel", "arbitrary"),
        name="fox_sample",
    )(pt, qbd, knew, vnew, lnew, *([cache_k] * pages_per_step), *([cache_v] * pages_per_step),
      *([cache_lft] * pages_per_step))


def _odd_proj_kernel(*refs, tm):
    n_groups = len(DIL_CONFIGS)
    y_ref, w_ref = refs[0], refs[1]
    de_refs = refs[2:2 + 3 * n_groups]
    tail_refs = refs[2 + 3 * n_groups:2 + 5 * n_groups]
    stage_refs = refs[2 + 5 * n_groups:]
    yb = y_ref[...].astype(BF16)
    n_lane_groups = de_refs[0].shape[1]
    gw = n_lane_groups * LANES
    stage = 0
    for g, (window, dil) in enumerate(DIL_CONFIGS):
        for part in range(3):
            col = (g * 3 + part) * gw
            res = _dot(yb, w_ref[:, col:col + gw])
            if part == 0:
                res = res * (D_HEAD ** -0.5)
            dst = de_refs[g * 3 + part]
            if dil == 1:
                for a in range(n_lane_groups):
                    dst[0, a] = res[:, a * LANES:(a + 1) * LANES].astype(BF16)
            else:
                s_ref = stage_refs[stage]
                stage += 1
                for a in range(n_lane_groups):
                    s_ref[a] = res[:, a * LANES:(a + 1) * LANES]
                for r in range(dil):
                    for a in range(n_lane_groups):
                        dst[r, a] = s_ref[a, pl.ds(r, tm // dil, stride=dil), :].astype(BF16)
            if part > 0:
                keep = min(window, tm)
                tail_refs[g * 2 + part - 1][...] = res[tm - keep:, :].T


def _odd_proj(y3, w, *, tm):
    batch, seq_len, d = y3.shape
    gw = w.shape[1] // (3 * len(DIL_CONFIGS))
    n_lane_groups = gw // LANES
    n_tiles = seq_len // tm
    out_shape, out_specs, n_stage = [], [], 0
    for window, dil in DIL_CONFIGS:
        assert tm % (dil * 16) == 0 and (window % tm == 0 or tm % window == 0)
        for _ in range(3):
            out_shape.append(jax.ShapeDtypeStruct((batch, dil, n_lane_groups, seq_len // dil, LANES), BF16))
            out_specs.append(pl.BlockSpec((None, dil, n_lane_groups, tm // dil, LANES),
                                          lambda b, i: (b, 0, 0, i, 0)))
        n_stage += 3 if dil > 1 else 0
    for window, dil in DIL_CONFIGS:
        keep = min(window, tm)
        first = n_tiles - window // keep
        for _ in range(2):
            out_shape.append(jax.ShapeDtypeStruct((batch, gw, window), F32))
            out_specs.append(pl.BlockSpec((None, gw, keep),
                                          lambda b, i, first=first: (b, 0, jnp.maximum(i - first, 0))))
    return pl.pallas_call(
        functools.partial(_odd_proj_kernel, tm=tm),
        out_shape=tuple(out_shape),
        grid=(batch, n_tiles),
        in_specs=[pl.BlockSpec((None, tm, d), lambda b, i: (b, i, 0)),
                  pl.BlockSpec(w.shape, lambda b, i: (0, 0), pipeline_mode=pl.Buffered(1))],
        out_specs=tuple(out_specs),
        scratch_shapes=[pltpu.VMEM((n_lane_groups, tm, LANES), F32)] * n_stage,
        compiler_params=_params("parallel", "arbitrary"),
        name="odd_proj",
    )(y3, w)


def _bmm_nt(a, b):
    return lax.dot_general(a, b, (((2,), (2,)), ((0,), (0,))), preferred_element_type=F32)


def _bmm(a, b):
    return lax.dot_general(a, b, (((2,), (1,)), ((0,), (0,))), preferred_element_type=F32)


def _dil_attn_kernel(q_ref, kh_ref, kc_ref, vh_ref, vc_ref, bfirst_ref, bprev_ref, bcur_ref, o_ref, lse_ref,
                     *, dil, heads, qb):
    nblk = q_ref.shape[2] // qb
    units = [(r, j) for r in range(dil) for j in range(nblk)]
    even_lane = _lane_is_even_head((1, qb, LANES))

    def cur_blocks(ref, g):
        return jnp.concatenate([ref[r, g].reshape(nblk, qb, LANES) for r in range(dil)], axis=0)

    def prev_blocks(cur, halo_ref, g):
        parts = []
        for r in range(dil):
            parts.append(halo_ref[r, g][None])
            if nblk > 1:
                parts.append(cur[r * nblk:(r + 1) * nblk - 1])
        return jnp.concatenate(parts, axis=0)

    for g in range(heads // HEADS_PER_LANE_GROUP):
        q2, kc, vc = cur_blocks(q_ref, g), cur_blocks(kc_ref, g), cur_blocks(vc_ref, g)
        kp, vp = prev_blocks(kc, kh_ref, g), prev_blocks(vc, vh_ref, g)
        outs, lses = [], []
        for sub in range(HEADS_PER_LANE_GROUP):
            h = g * HEADS_PER_LANE_GROUP + sub
            qh = jnp.where(even_lane if sub == 0 else ~even_lane, q2, jnp.zeros_like(q2))
            bias_prev = jnp.stack([bfirst_ref[h] if j == 0 else bprev_ref[h] for _, j in units])
            sp = _bmm_nt(qh, kp) + bias_prev
            sc = _bmm_nt(qh, kc) + bcur_ref[h][None]
            m = jnp.maximum(jnp.max(sp, axis=2, keepdims=True), jnp.max(sc, axis=2, keepdims=True))
            pp = jnp.exp(sp - m)
            pc = jnp.exp(sc - m)
            den = jnp.sum(pp, axis=2, keepdims=True) + jnp.sum(pc, axis=2, keepdims=True)
            pv = _bmm(pp.astype(BF16), vp) + _bmm(pc.astype(BF16), vc)
            outs.append(pv / den)
            lses.append(jnp.broadcast_to(m + jnp.log(den), pv.shape))
        o2 = jnp.where(even_lane, outs[0], outs[1])
        l2 = jnp.where(even_lane, lses[0], lses[1])
        for u, (r, j) in enumerate(units):
            rows = pl.ds(j * qb, qb) if dil == 1 else pl.ds(r + dil * qb * j, qb, stride=dil)
            o_ref[g, rows, :] = o2[u]
            lse_ref[g, rows, :] = l2[u]


def _dil_attention(q, k, v, bias_first, bias_prev, bias_cur, *, heads, qb, units):
    batch, dil, n_lane_groups, slots, _ = q.shape
    nblk = units // dil
    span = qb * nblk
    cur = pl.BlockSpec((None, dil, n_lane_groups, span, LANES), lambda b, i: (b, 0, 0, i, 0))
    halo = pl.BlockSpec((None, dil, n_lane_groups, qb, LANES),
                        lambda b, i: (b, 0, 0, jnp.maximum(i * nblk - 1, 0), 0))
    nat = pl.BlockSpec((None, n_lane_groups, span * dil, LANES), lambda b, i: (b, 0, i, 0))
    out_sds = jax.ShapeDtypeStruct((batch, n_lane_groups, slots * dil, LANES), F32)
    return pl.pallas_call(
        functools.partial(_dil_attn_kernel, dil=dil, heads=heads, qb=qb),
        out_shape=(out_sds, out_sds),
        grid=(batch, slots // span),
        in_specs=[cur, halo, cur, halo, cur,
                  pl.BlockSpec((None, heads, qb, qb), lambda b, i: (jnp.minimum(i, 1), 0, 0, 0)),
                  _const_spec(bias_prev.shape), _const_spec(bias_cur.shape)],
        out_specs=(nat, nat),
        compiler_params=_params("parallel", "arbitrary"),
        name=f"dil_attention_d{dil}",
    )(q, k, k, v, v, bias_first, bias_prev, bias_cur)


def _odd_mix_kernel(*refs, alpha):
    n = len(DIL_CONFIGS)
    o_refs, lse_refs = refs[:n], refs[n:2 * n]
    y_ref, wo_ref, g_ref, b_ref, out_ref = refs[2 * n:]
    merged = []
    for a in range(o_refs[0].shape[0]):
        lses = [r[a] for r in lse_refs]
        top = functools.reduce(jnp.maximum, lses)
        wts = [jnp.exp(l - top) for l in lses]
        num = functools.reduce(jnp.add, [w * r[a] for w, r in zip(wts, o_refs)])
        merged.append((num / functools.reduce(jnp.add, wts)).astype(BF16))
    mix = _dot(jnp.concatenate(merged, axis=-1), wo_ref[...])
    out_ref[...] = _layer_norm(alpha * y_ref[...] + mix, g_ref[...], b_ref[...])


def _odd_mix(outs, lses, y3, wo, g, b, *, alpha, tm):
    batch, seq_len, d = y3.shape
    n_lane_groups = outs[0].shape[1]
    part = pl.BlockSpec((None, n_lane_groups, tm, LANES), lambda b, i: (b, 0, i, 0))
    row = pl.BlockSpec((None, tm, d), lambda b, i: (b, i, 0))
    return pl.pallas_call(
        functools.partial(_odd_mix_kernel, alpha=alpha),
        out_shape=jax.ShapeDtypeStruct((batch, seq_len, d), F32),
        grid=(batch, seq_len // tm),
        in_specs=[part] * (2 * len(outs)) + [row, _const_spec(wo.shape), _const_spec(g.shape),
                                             _const_spec(b.shape)],
        out_specs=row,
        compiler_params=_params("parallel", "parallel"),
        name="odd_mix",
    )(*outs, *lses, y3, wo, g, b)


def _proj_kernel(x_ref, w_ref, o_ref):
    o_ref[...] = _dot(x_ref[...].astype(BF16), w_ref[...])


def _proj(x, w, *, tn):
    rows, d = x.shape
    n = w.shape[1]
    return pl.pallas_call(
        _proj_kernel,
        out_shape=jax.ShapeDtypeStruct((rows, n), F32),
        grid=(n // tn,),
        in_specs=[_const_spec((rows, d)), pl.BlockSpec((d, tn), lambda j: (0, j))],
        out_specs=pl.BlockSpec((rows, tn), lambda j: (0, j)),
        compiler_params=_params("parallel"),
        name="proj",
    )(x, w)


def _dil_sample_kernel(qbd_ref, kc_ref, vc_ref, kn_ref, vn_ref, tabc_ref, tabn_ref,
                       o_ref, lse_ref, ko_ref, vo_ref, *, n_new, heads):
    width, buf_len = kc_ref.shape
    qbd = qbd_ref[...]
    chunk = min(buf_len, 512)
    starts = range(0, buf_len, chunk)
    sc = [_dot(qbd, kc_ref[:, c0:c0 + chunk].astype(BF16)) + tabc_ref[:, c0:c0 + chunk] for c0 in starts]
    sn = _dot_nt(qbd, kn_ref[...].astype(BF16)) + tabn_ref[...]
    m = functools.reduce(jnp.maximum, [jnp.max(s, axis=1, keepdims=True) for s in sc + [sn]])
    pn = jnp.exp(sn - m)
    den = jnp.sum(pn, axis=1, keepdims=True)
    pv = _dot(pn.astype(BF16), vn_ref[...].astype(BF16))
    for s, c0 in zip(sc, starts):
        p = jnp.exp(s - m)
        den = den + jnp.sum(p, axis=1, keepdims=True)
        pv = pv + _dot_nt(p.astype(BF16), vc_ref[:, c0:c0 + chunk].astype(BF16))
    full = pv / den
    lse = jnp.broadcast_to(m + jnp.log(den), full.shape)
    own = (lax.broadcasted_iota(jnp.int32, full.shape, 0) % heads
           == lax.broadcasted_iota(jnp.int32, full.shape, 1) // D_HEAD)
    o_ref[...] = jnp.sum(jnp.where(own, full, 0.0).reshape(n_new, heads, width), axis=1)
    lse_ref[...] = jnp.sum(jnp.where(own, lse, 0.0).reshape(n_new, heads, width), axis=1)
    rows = 64
    is_new = lax.broadcasted_iota(jnp.int32, (rows, LANES), 1) >= LANES - n_new
    zero_rows = jnp.zeros((LANES - SUBLANES, width), F32)
    for src_ref, new_ref, dst_ref in ((kc_ref, kn_ref, ko_ref), (vc_ref, vn_ref, vo_ref)):
        tail = jnp.concatenate([zero_rows, new_ref[...]], axis=0).T
        for r0 in range(0, width, rows):
            rolled = pltpu.roll(src_ref[r0:r0 + rows, :], buf_len - n_new, axis=1)
            if buf_len > LANES:
                dst_ref[r0:r0 + rows, 0:buf_len - LANES] = rolled[:, 0:buf_len - LANES]
            dst_ref[r0:r0 + rows, buf_len - LANES:] = jnp.where(is_new, tail[r0:r0 + rows, :],
                                                                rolled[:, buf_len - LANES:])


def _dil_sample(qbd, kc, vc, kn, vn, tab_c, tab_n, *, n_new, heads):
    nb, width, buf_len = kc.shape
    n_rows = qbd.shape[1]
    b3 = lambda *shape: pl.BlockSpec((None,) + shape, lambda b: (b,) + (0,) * len(shape))
    small = jax.ShapeDtypeStruct((nb, n_new, width), F32)
    big = jax.ShapeDtypeStruct((nb, width, buf_len), F32)
    return pl.pallas_call(
        functools.partial(_dil_sample_kernel, n_new=n_new, heads=heads),
        out_shape=(small, small, big, big),
        grid=(nb,),
        in_specs=[b3(n_rows, width), b3(width, buf_len), b3(width, buf_len),
                  b3(SUBLANES, width), b3(SUBLANES, width),
                  _const_spec(tab_c.shape), _const_spec(tab_n.shape)],
        out_specs=(b3(n_new, width), b3(n_new, width), b3(width, buf_len), b3(width, buf_len)),
        compiler_params=_params("parallel"),
        name=f"dil_sample_l{buf_len}",
    )(qbd, kc, vc, kn, vn, tab_c, tab_n)


def _rel_bucket(dist):
    exact = REL_BUCKETS // 2
    d = jnp.maximum(dist, 1).astype(F32)
    large = exact + (jnp.log(d / exact) / math.log(REL_MAX_DIST / exact) * (REL_BUCKETS - exact)).astype(jnp.int32)
    large = jnp.minimum(large, REL_BUCKETS - 1)
    return jnp.where(dist < exact, dist, large)


def _group_bias(rel_bias, g, window, dil, heads):
    dist = jnp.arange(window // dil + 1) * dil
    onehot = _rel_bucket(dist)[:, None] == jnp.arange(REL_BUCKETS)[None, :]
    cols = rel_bias[:, g * heads:(g + 1) * heads].astype(F32)
    return jnp.sum(jnp.where(onehot[:, :, None], cols[None], 0.0), axis=1).T


def _toeplitz(w, n):
    heads, period = w.shape
    flat = jnp.tile(w, (1, n))[:, :n * (period - 1)]
    return flat.reshape(heads, n, period - 1)[:, :, :n]


def _prompt_bias_tiles(bias, qb):
    heads, n_keys = bias.shape
    assert n_keys == qb + 1
    neg = jnp.full((heads, qb), NEG_INF, F32)
    cur = _toeplitz(jnp.concatenate([bias[:, 0:qb], neg], axis=1), qb)
    prev = _toeplitz(jnp.concatenate([bias[:, qb:qb + 1], neg, bias[:, 1:qb]], axis=1), qb)
    cur, prev = jnp.swapaxes(cur, 1, 2), jnp.swapaxes(prev, 1, 2)
    return jnp.stack([jnp.full_like(prev, NEG_INF), prev]), prev, cur


def _sample_bias_tables(bias, dil, n_new, buf_len):
    heads, n_keys = bias.shape
    n_back = n_keys - 1
    assert n_back * dil == buf_len
    oldest_first = bias[:, :0:-1]
    gaps = jnp.full((heads, n_back, dil - 1), NEG_INF, F32)
    spread = jnp.concatenate([oldest_first[:, :, None], gaps], axis=2).reshape(heads, buf_len)
    neg_col = jnp.full((heads,), NEG_INF, F32)
    tab_c, tab_n = [], []
    for i in range(n_new):
        tab_c.append(jnp.concatenate([jnp.full((heads, i), NEG_INF, F32), spread[:, :buf_len - i]], axis=1))
        in_set = lambda j: 0 <= j <= i and (i - j) % dil == 0
        tab_n.append(jnp.stack([bias[:, (i - j) // dil] if in_set(j) else neg_col
                                for j in range(n_new - SUBLANES, n_new)], axis=1))
    return jnp.concatenate(tab_c, axis=0), jnp.concatenate(tab_n, axis=0)


PROMPT_ROW_TILE = 512
ATTN_BLOCK = 512
DIL_SLOT_BLOCK = 128
DIL_UNITS_PER_STEP = 16
PAGES_PER_STEP = 16


def _pad_axis(x, axis, size):
    pad = [(0, 0)] * x.ndim
    pad[axis] = (0, size - x.shape[axis])
    return jnp.pad(x, pad)


def token_major(xt, lead):
    return jnp.transpose(xt.reshape(lead, -1, D_HEAD, xt.shape[-1]), (0, 3, 1, 2))


def kernel(x_prompt, x_sample, cache_fox_k, cache_fox_v, cache_fox_logf, state_pool, cache_dil0_k, cache_dil0_v, cache_dil1_k, cache_dil1_v, cache_dil2_k, cache_dil2_v, page_table, p_prompt, p_sample, w_in_even, b_fgate, pool_w, pool_scale, w_out_even, w_in_odd, w_out_odd, rel_bias, ffn1_wg, ffn1_wu, ffn1_wd, ffn2_wg, ffn2_wu, ffn2_wd, ln_g, ln_b, ple_wg, ple_bg, ple_wp):
    depth = ffn1_wg.shape[0]
    alpha = (2 * depth) ** 0.25
    batch, seq_len, d_model = x_prompt.shape
    nb, n_new, _ = x_sample.shape
    past_len = page_table.shape[1] * PAGE_SIZE
    fox_heads = cache_fox_k.shape[-2]
    fox_w = fox_heads * D_HEAD
    dil_heads = cache_dil0_k.shape[-2]
    dil_w = dil_heads * D_HEAD
    dil_caches_k = (cache_dil0_k, cache_dil1_k, cache_dil2_k)
    dil_caches_v = (cache_dil0_v, cache_dil1_v, cache_dil2_v)
    rows_p, rows_s = batch * seq_len, nb * n_new
    tm = PROMPT_ROW_TILE
    bf = lambda w: w.astype(BF16)
    vec = lambda a: a.reshape(1, -1)

    yp = x_prompt.reshape(rows_p, d_model)
    ys = x_sample.reshape(rows_s, d_model)
    outs = {name: [] for name in ("fkp", "fvp", "flp", "fks", "fvs", "fls", "plp", "pls")}
    dkp, dvp, dks, dvs = ([[] for _ in DIL_CONFIGS] for _ in range(4))

    for i in range(depth):
        w1 = (ffn1_wg, ffn1_wu, ffn1_wd, vec(ln_g[i, 0]), vec(ln_b[i, 0]))
        yp = _ffn(yp, *w1, layer=i, alpha=alpha, tm=tm)
        ys = _ffn(ys, *w1, layer=i, alpha=alpha, tm=rows_s)
        g1, b1 = vec(ln_g[i, 1]), vec(ln_b[i, 1])
        if i % 2 == 0:
            e = i // 2
            w_in = w_in_even[e]
            wqkv = bf(w_in[:, :3 * fox_w])
            wf = bf(_pad_axis(w_in[:, 3 * fox_w:3 * fox_w + fox_heads], 1, LANES))
            bfg = _pad_axis(vec(b_fgate[e]), 1, LANES)
            wu = bf(w_in[:, 3 * fox_w + fox_heads:])
            wpool, pscale, wo = bf(pool_w[e]), vec(pool_scale[e]), bf(w_out_even[e])
            qa, ka, va, kt, vt, lft, u = _even_proj(yp.reshape(batch, seq_len, d_model), wqkv, wf, bfg, wu,
                                                    heads=fox_heads, tm=tm, prompt=True)
            att = _fox_attention(qa, ka, va, heads=fox_heads, tq=ATTN_BLOCK).reshape(rows_p, fox_w)
            u = u.reshape(rows_p, -1)
            yp_next = _even_mix(att, u, yp, wpool, pscale, wo, g1, b1, alpha=alpha, seq_len=seq_len, tm=tm)
            outs["fkp"].append(token_major(kt, batch))
            outs["fvp"].append(token_major(vt, batch))
            outs["flp"].append(jnp.swapaxes(lft, 1, 2))
            outs["plp"].append(u.reshape(batch, seq_len, -1)[:, seq_len - POOL_BUF:])
            qs, ks, vs, lfts, us = _even_proj(ys[None], wqkv, wf, bfg, wu, heads=fox_heads, tm=rows_s,
                                              prompt=False)
            head_mask = (jnp.arange(fox_w)[None, :] // D_HEAD == jnp.arange(fox_heads)[:, None]).astype(BF16)
            qbd = (qs.reshape(nb, n_new, 1, fox_w) * head_mask).reshape(nb, n_new * fox_heads, fox_w)
            knew = _pad_axis(ks.reshape(nb, n_new, fox_w), 1, SUBLANES)
            vnew = _pad_axis(vs.reshape(nb, n_new, fox_w), 1, SUBLANES)
            lf3 = jnp.swapaxes(lfts.reshape(fox_heads, nb, n_new), 0, 1)
            lnew = jnp.broadcast_to(lf3[:, None], (nb, n_new, fox_heads, n_new)).reshape(nb, n_new * fox_heads, n_new)
            lnew = _pad_axis(lnew, 2, LANES)
            n_phys = cache_fox_k.shape[1]
            page_t = lambda c: jnp.transpose(c, (0, 2, 3, 1)).reshape(n_phys, fox_w, PAGE_SIZE)
            att_s = _fox_sample(qbd, knew, vnew, lnew, page_t(cache_fox_k[e]), page_t(cache_fox_v[e]),
                                jnp.swapaxes(cache_fox_logf[e], 1, 2), page_table,
                                heads=fox_heads, n_new=n_new, pages_per_step=PAGES_PER_STEP)
            ctx_s = jnp.concatenate([state_pool[e].astype(F32), us.reshape(nb, n_new, -1)], axis=1)
            pool_s = _pool_sample(jnp.swapaxes(ctx_s, 0, 1), wpool, pscale, pos0=past_len, n_new=n_new)
            pool_s = jnp.swapaxes(pool_s, 0, 1).reshape(rows_s, -1)
            ys_next = _pair_mix(att_s.reshape(rows_s, fox_w), pool_s, ys, wo, g1, b1, alpha=alpha, tm=rows_s)
            outs["fks"].append(ks.reshape(nb, n_new, fox_heads, D_HEAD))
            outs["fvs"].append(vs.reshape(nb, n_new, fox_heads, D_HEAD))
            outs["fls"].append(jnp.swapaxes(lf3, 1, 2))
            outs["pls"].append(ctx_s[:, -POOL_BUF:])
        else:
            o = i // 2
            w_in, wo = bf(w_in_odd[o]), bf(w_out_odd[o])
            biases = [_group_bias(rel_bias, g, window, dil, dil_heads) for g, (window, dil) in enumerate(DIL_CONFIGS)]
            proj = _odd_proj(yp.reshape(batch, seq_len, d_model), w_in, tm=tm)
            n_groups = len(DIL_CONFIGS)
            o_parts, lse_parts = [], []
            for g, (window, dil) in enumerate(DIL_CONFIGS):
                qd, kd, vd = proj[3 * g:3 * g + 3]
                bias_first, bias_prev, bias_cur = _prompt_bias_tiles(biases[g], DIL_SLOT_BLOCK)
                og, lg = _dil_attention(qd, kd, vd, bias_first, bias_prev, bias_cur, heads=dil_heads,
                                        qb=DIL_SLOT_BLOCK, units=DIL_UNITS_PER_STEP)
                o_parts.append(og)
                lse_parts.append(lg)
                dkp[g].append(token_major(proj[3 * n_groups + 2 * g], batch))
                dvp[g].append(token_major(proj[3 * n_groups + 2 * g + 1], batch))
            yp_next = _odd_mix(o_parts, lse_parts, yp.reshape(batch, seq_len, d_model), wo, g1, b1,
                               alpha=alpha, tm=tm).reshape(rows_p, d_model)
            lane_major = lambda a: jnp.swapaxes(a.reshape(rows_s, dil_w // LANES, LANES), 0, 1)[None]
            proj_s = _proj(ys, w_in, tn=3 * dil_w).reshape(nb, n_new, n_groups, 3, dil_w)
            head_mask = (jnp.arange(dil_w)[None, :] // D_HEAD == jnp.arange(dil_heads)[:, None]).astype(F32)
            o_parts, lse_parts = [], []
            for g, (window, dil) in enumerate(DIL_CONFIGS):
                qg = proj_s[:, :, g, 0] * (D_HEAD ** -0.5)
                qbd = (qg[:, :, None, :] * head_mask).astype(BF16).reshape(nb, n_new * dil_heads, dil_w)
                k_new, v_new = proj_s[:, :, g, 1], proj_s[:, :, g, 2]
                buf_len = dil_caches_k[g].shape[2]
                pos_minor = lambda c: jnp.transpose(c, (0, 2, 3, 1)).reshape(nb, dil_w, buf_len)
                rows_last = lambda x: jnp.pad(x, ((0, 0), (SUBLANES - n_new, 0), (0, 0)))
                tab_c, tab_n = _sample_bias_tables(biases[g], dil, n_new, buf_len)
                og, lg, k_roll, v_roll = _dil_sample(
                    qbd, pos_minor(dil_caches_k[g][o]), pos_minor(dil_caches_v[g][o]),
                    rows_last(k_new), rows_last(v_new), tab_c, tab_n, n_new=n_new, heads=dil_heads)
                o_parts.append(lane_major(og))
                lse_parts.append(lane_major(lg))
                dks[g].append(token_major(k_roll, nb))
                dvs[g].append(token_major(v_roll, nb))
            ys_next = _odd_mix(o_parts, lse_parts, ys[None], wo, g1, b1, alpha=alpha,
                               tm=rows_s).reshape(rows_s, d_model)
        yp, ys = yp_next, ys_next
        w2 = (ffn2_wg, ffn2_wu, ffn2_wd, vec(ln_g[i, 2]), vec(ln_b[i, 2]))
        ple_w = (ple_wg, vec(ple_bg[i]), ple_wp)
        yp = _ffn(yp, *w2, layer=i, alpha=alpha, tm=tm, ple=(p_prompt.reshape(depth, rows_p, -1),) + ple_w)
        ys = _ffn(ys, *w2, layer=i, alpha=alpha, tm=rows_s, ple=(p_sample.reshape(depth, rows_s, -1),) + ple_w)

    stack = lambda parts: jnp.stack(parts)
    result = [yp.reshape(batch, seq_len, d_model), ys.reshape(nb, n_new, d_model)]
    result += [stack(outs[name]) for name in ("fkp", "fvp", "flp", "fks", "fvs", "fls", "plp", "pls")]
    for g in range(len(DIL_CONFIGS)):
        result += [stack(dkp[g]), stack(dvp[g])]
    for g in range(len(DIL_CONFIGS)):
        result += [stack(dks[g]), stack(dvs[g])]
    return tuple(result)
```

```python
import functools
import math

import jax
import jax.numpy as jnp
import numpy as np
from jax import lax
from jax.experimental import pallas as pl
from jax.experimental.pallas import tpu as pltpu

F32 = jnp.float32
BF16 = jnp.bfloat16

D_HEAD = 64
POOL_WINDOWS = (2, 4, 8, 16)
POOL_BUF = max(POOL_WINDOWS) - 1
DIL_CONFIGS = ((128, 1), (512, 4), (2048, 16))
REL_BUCKETS = 32
REL_MAX_DIST = 2048
LN_EPS = 1e-5
NEG_INF = -1e30
PAGE_SIZE = 128

LANES = 128
SUBLANES = 8
VMEM_LIMIT_BYTES = 56 * 1024 * 1024

HEADS_PER_LANE_GROUP = LANES // D_HEAD
HALO_ROWS = 16


def _params(*semantics):
    return pltpu.CompilerParams(dimension_semantics=semantics, vmem_limit_bytes=VMEM_LIMIT_BYTES)


def _dot(a, b):
    return jnp.dot(a, b, preferred_element_type=F32)


def _dot_nt(a, b):
    return lax.dot_general(a, b, (((1,), (1,)), ((), ())), preferred_element_type=F32)


def _layer_norm(z, g, b):
    mu = jnp.mean(z, axis=-1, keepdims=True)
    zc = z - mu
    var = jnp.mean(zc * zc, axis=-1, keepdims=True)
    return zc * lax.rsqrt(var + LN_EPS) * g + b


def _log_sigmoid(x):
    return jnp.minimum(x, 0.0) - jnp.log1p(jnp.exp(-jnp.abs(x)))


def _split3(x):
    hi = x.astype(BF16)
    r1 = x - hi.astype(F32)
    mid = r1.astype(BF16)
    lo = (r1 - mid.astype(F32)).astype(BF16)
    return hi, mid, lo


def _lane_is_even_head(shape):
    return lax.broadcasted_iota(jnp.int32, shape, len(shape) - 1) % LANES < D_HEAD


def _const_spec(shape):
    zeros = (0,) * len(shape)
    return pl.BlockSpec(shape, lambda *_: zeros)


def _ffn_kernel(*refs, alpha, ff_chunk, with_ple):
    if with_ple:
        (x_ref, wg_ref, wu_ref, wd_ref, g_ref, b_ref, p_ref, pwg_ref, pbg_ref, pwp_ref,
         o_ref, h_ref) = refs
    else:
        x_ref, wg_ref, wu_ref, wd_ref, g_ref, b_ref, o_ref, h_ref = refs
    x = x_ref[...]
    xb = x.astype(BF16)
    d_ff = wg_ref.shape[1]
    chunks = [slice(c * ff_chunk, (c + 1) * ff_chunk) for c in range(d_ff // ff_chunk)]
    for cols in chunks:
        gate = _dot(xb, wg_ref[:, cols].astype(BF16))
        up = _dot(xb, wu_ref[:, cols].astype(BF16))
        h_ref[:, cols] = (gate * jax.nn.sigmoid(gate) * up).astype(BF16)
    down = functools.reduce(jnp.add, [_dot(h_ref[:, cols], wd_ref[cols, :].astype(BF16)) for cols in chunks])
    y = _layer_norm(alpha * x + 0.5 * down, g_ref[...], b_ref[...])
    if with_ple:
        gate = jax.nn.sigmoid(_dot(y.astype(BF16), pwg_ref[...].astype(BF16)) + pbg_ref[...])
        y = y + gate * _dot(p_ref[...].astype(BF16), pwp_ref[...].astype(BF16))
    o_ref[...] = y


def _ffn(x, wg, wu, wd, g, b, *, layer, alpha, tm, ple=None):
    rows, d = x.shape
    d_ff = wg.shape[2]
    ff_chunk = 256
    assert rows % tm == 0 and d_ff % ff_chunk == 0
    row_spec = pl.BlockSpec((tm, d), lambda i: (i, 0))
    resident = lambda shape: pl.BlockSpec((None,) + shape, lambda i: (layer, 0, 0), pipeline_mode=pl.Buffered(1))
    in_specs = [row_spec, resident((d, d_ff)), resident((d, d_ff)), resident((d_ff, d)),
                _const_spec((1, d)), _const_spec((1, d))]
    args = [x, wg, wu, wd, g, b]
    if ple is not None:
        p, pwg, pbg, pwp = ple
        in_specs += [pl.BlockSpec((None, tm, p.shape[2]), lambda i: (layer, i, 0)), resident(pwg.shape[1:]),
                     _const_spec((1, d)), resident(pwp.shape[1:])]
        args += [p, pwg, pbg, pwp]
    return pl.pallas_call(
        functools.partial(_ffn_kernel, alpha=alpha, ff_chunk=ff_chunk, with_ple=ple is not None),
        out_shape=jax.ShapeDtypeStruct((rows, d), F32),
        grid=(rows // tm,),
        in_specs=in_specs,
        out_specs=row_spec,
        scratch_shapes=[pltpu.VMEM((tm, d_ff), BF16)],
        compiler_params=_params("parallel"),
        name="ffn_ple" if ple is not None else "ffn",
    )(*args)


N_SPLIT = 3
DECAY_LANE = D_HEAD


def _head_slots(x):
    low = lax.broadcasted_iota(jnp.int32, (x.shape[0], LANES), 1) < D_HEAD
    slots = []
    for g in range(x.shape[1] // LANES):
        pair = x[:, g * LANES:(g + 1) * LANES]
        slots.append(jnp.where(low, pair, 0.0))
        slots.append(jnp.where(low, pltpu.roll(pair, D_HEAD, axis=1), 0.0))
    return slots


def _even_proj_kernel(*refs, prompt):
    if prompt:
        (y_ref, wqkv_ref, wf_ref, bf_ref, wu_ref, place_q_ref, place_k_ref, ones_q_ref, ones_k_ref, ones_v_ref,
         qa_ref, ka_ref, va_ref, kt_ref, vt_ref, logft_ref, u_ref, carry_ref) = refs
    else:
        y_ref, wqkv_ref, wf_ref, bf_ref, wu_ref, q_ref, k_ref, v_ref, logft_ref, u_ref = refs
    yb = y_ref[...].astype(BF16)
    hw = wqkv_ref.shape[1] // 3
    heads = logft_ref.shape[0]
    q = _dot(yb, wqkv_ref[:, 0:hw]) * (D_HEAD ** -0.5)
    k = _dot(yb, wqkv_ref[:, hw:2 * hw])
    v = _dot(yb, wqkv_ref[:, 2 * hw:3 * hw])
    logf = _log_sigmoid(_dot(yb, wf_ref[...]) + bf_ref[...])
    logft_ref[...] = logf.T[:heads, :]
    u_ref[...] = _dot(yb, wu_ref[...])
    if not prompt:
        q_ref[...] = q.astype(BF16)
        k_ref[...] = k
        v_ref[...] = v
        return
    tm = logf.shape[0]
    kt_ref[...] = k.T
    vt_ref[...] = v.T
    @pl.when(pl.program_id(1) == 0)
    def _():
        carry_ref[...] = jnp.zeros_like(carry_ref)

    tri = jnp.where(lax.broadcasted_iota(jnp.int32, (tm, tm), 0) >= lax.broadcasted_iota(jnp.int32, (tm, tm), 1),
                    1.0, 0.0).astype(BF16)
    c = functools.reduce(jnp.add, [_dot(tri, part) for part in _split3(logf)]) + carry_ref[0:1, :]
    carry_ref[...] = jnp.broadcast_to(c[tm - 1:tm, :], carry_ref.shape)
    c_terms = jnp.concatenate(_split3(c), axis=1)
    decay_q = _dot(c_terms, place_q_ref[...]) + ones_q_ref[...]
    decay_k = _dot(c_terms, place_k_ref[...]) + ones_k_ref[...]
    for h, (qs, ks, vs) in enumerate(zip(_head_slots(q), _head_slots(k), _head_slots(v))):
        lanes = slice(h * LANES, (h + 1) * LANES)
        qa_ref[:, lanes] = (qs + decay_q[:, lanes]).astype(BF16)
        ka_ref[:, lanes] = (ks + decay_k[:, lanes]).astype(BF16)
        va_ref[:, lanes] = (vs + ones_v_ref[:, lanes]).astype(BF16)


def _decay_placement(heads):
    place_q = np.zeros((N_SPLIT * LANES, heads * LANES), np.float32)
    place_k = np.zeros_like(place_q)
    ones_q = np.zeros((1, heads * LANES), np.float32)
    ones_k = np.zeros_like(ones_q)
    ones_v = np.zeros_like(ones_q)
    for h in range(heads):
        base = h * LANES + DECAY_LANE
        for j in range(N_SPLIT):
            place_q[j * LANES + h, base + j] = 1.0
            place_k[j * LANES + h, base + N_SPLIT + j] = -1.0
        ones_q[0, base + N_SPLIT:base + 2 * N_SPLIT] = 1.0
        ones_k[0, base:base + N_SPLIT] = 1.0
        ones_v[0, base:(h + 1) * LANES] = 1.0
    return (jnp.asarray(place_q, BF16), jnp.asarray(place_k, BF16), jnp.asarray(ones_q), jnp.asarray(ones_k),
            jnp.asarray(ones_v))


def _even_proj(y3, wqkv, wf, bf, wu, *, heads, tm, prompt):
    batch, seq_len, d = y3.shape
    hw = wqkv.shape[1] // 3
    pw = wu.shape[1]
    row = lambda w: pl.BlockSpec((None, tm, w), lambda b, i: (b, i, 0))
    col = lambda h: pl.BlockSpec((None, h, tm), lambda b, i: (b, 0, i))
    nat = lambda w, dt: jax.ShapeDtypeStruct((batch, seq_len, w), dt)
    args = [y3, wqkv, wf, bf, wu]
    tail_shape = (jax.ShapeDtypeStruct((batch, heads, seq_len), F32), nat(pw, F32))
    tail_specs = (col(heads), row(pw))
    if prompt:
        args += list(_decay_placement(heads))
        slot_w = heads * LANES
        kv_t = jax.ShapeDtypeStruct((batch, hw, seq_len), F32)
        out_shape = (nat(slot_w, BF16),) * 3 + (kv_t, kv_t) + tail_shape
        out_specs = (row(slot_w),) * 3 + (col(hw), col(hw)) + tail_specs
        scratch = [pltpu.VMEM((SUBLANES, LANES), F32)]
    else:
        out_shape = (nat(hw, BF16), nat(hw, F32), nat(hw, F32)) + tail_shape
        out_specs = (row(hw),) * 3 + tail_specs
        scratch = []
    return pl.pallas_call(
        functools.partial(_even_proj_kernel, prompt=prompt),
        out_shape=out_shape,
        grid=(batch, seq_len // tm),
        in_specs=[row(d)] + [_const_spec(a.shape) for a in args[1:]],
        out_specs=out_specs,
        scratch_shapes=scratch,
        compiler_params=_params("parallel", "arbitrary"),
        name="even_proj",
    )(*args)


SOFTMAX_ROWS = 64


def _attn_step(masked, q_ref, k_ref, v_ref, m_ref, acc_ref, s_ref, p_ref, scale_ref, heads):
    tq, tk = q_ref.shape[0], k_ref.shape[0]
    for h in range(heads):
        lanes = slice(h * LANES, (h + 1) * LANES)
        buf = h % 2
        s_ref[buf] = _dot_nt(q_ref[:, lanes], k_ref[:, lanes])
        for r0 in range(0, tq, SOFTMAX_ROWS):
            rows = slice(r0, r0 + SOFTMAX_ROWS)
            s = s_ref[buf, rows, :]
            if masked:
                causal = (lax.broadcasted_iota(jnp.int32, s.shape, 1)
                          <= r0 + lax.broadcasted_iota(jnp.int32, s.shape, 0))
                s = jnp.where(causal, s, NEG_INF)
            m_prev = m_ref[h, rows, :]
            m_new = jnp.maximum(m_prev, jnp.max(s, axis=1, keepdims=True))
            p_ref[buf, rows, :] = jnp.exp(s - jnp.tile(m_new, (1, tk // LANES))).astype(BF16)
            m_ref[h, rows, :] = m_new
            scale_ref[buf, rows, :] = jnp.exp(m_prev - m_new)
        acc_ref[h] = scale_ref[buf] * acc_ref[h] + _dot(p_ref[buf], v_ref[:, lanes])


def _attn_finish(o_ref, acc_ref, heads):
    low = lax.broadcasted_iota(jnp.int32, (o_ref.shape[0], LANES), 1) < D_HEAD
    for g in range(heads // HEADS_PER_LANE_GROUP):
        normed = []
        for sub in range(HEADS_PER_LANE_GROUP):
            acc = acc_ref[g * HEADS_PER_LANE_GROUP + sub]
            normed.append(acc / pltpu.roll(acc, D_HEAD, axis=1))
        out = jnp.where(low, normed[0], pltpu.roll(normed[1], D_HEAD, axis=1))
        o_ref[:, g * LANES:(g + 1) * LANES] = out.astype(o_ref.dtype)


def _fox_attn_kernel(qi_ref, ki_ref, q_ref, k_ref, v_ref, o_ref, m_ref, acc_ref, s_ref, p_ref, scale_ref,
                     *, heads):
    pair = pl.program_id(1)
    qi = qi_ref[pair]
    ki = ki_ref[pair]
    host = (q_ref, k_ref, v_ref, m_ref, acc_ref, s_ref, p_ref, scale_ref, heads)

    @pl.when(ki == 0)
    def _():
        m_ref[...] = jnp.full_like(m_ref, NEG_INF)
        acc_ref[...] = jnp.zeros_like(acc_ref)

    @pl.when(ki < qi)
    def _():
        _attn_step(False, *host)

    @pl.when(ki == qi)
    def _():
        _attn_step(True, *host)
        _attn_finish(o_ref, acc_ref, heads)


def _fox_attention(q, k, v, *, heads, tq):
    batch, seq_len, slot_w = q.shape
    nq = seq_len // tq
    qi_tab = np.concatenate([np.full(i + 1, i) for i in range(nq)]).astype(np.int32)
    ki_tab = np.concatenate([np.arange(i + 1) for i in range(nq)]).astype(np.int32)
    q_map = lambda b, p, qi, ki: (b, qi[p], 0)
    k_map = lambda b, p, qi, ki: (b, ki[p], 0)
    return pl.pallas_call(
        functools.partial(_fox_attn_kernel, heads=heads),
        out_shape=jax.ShapeDtypeStruct((batch, seq_len, heads * D_HEAD), BF16),
        grid_spec=pltpu.PrefetchScalarGridSpec(
            num_scalar_prefetch=2,
            grid=(batch, len(qi_tab)),
            in_specs=[pl.BlockSpec((None, tq, slot_w), q_map), pl.BlockSpec((None, tq, slot_w), k_map),
                      pl.BlockSpec((None, tq, slot_w), k_map)],
            out_specs=pl.BlockSpec((None, tq, heads * D_HEAD), q_map),
            scratch_shapes=[pltpu.VMEM((heads, tq, LANES), F32), pltpu.VMEM((heads, tq, LANES), F32),
                            pltpu.VMEM((2, tq, tq), F32), pltpu.VMEM((2, tq, tq), BF16),
                            pltpu.VMEM((2, tq, LANES), F32)],
        ),
        compiler_params=_params("parallel", "arbitrary"),
        name="fox_attention",
    )(jnp.asarray(qi_tab), jnp.asarray(ki_tab), q, k, v)


def _pool_groups(ctx_ref, n_avail, w_pool_ref, scale_ref, rows, halo):
    gw = w_pool_ref.shape[1]
    outs = []
    for g, w in enumerate(POOL_WINDOWS):
        lanes = slice(g * gw, (g + 1) * gw)
        cur = ctx_ref[pl.ds(halo, rows), lanes]
        win = cur
        for j in range(1, w):
            win = win + ctx_ref[pl.ds(halo - j, rows), lanes]
        mean = win / jnp.minimum(float(w), n_avail)
        outs.append(_dot((mean - cur).astype(BF16), w_pool_ref[g]))
    return jnp.concatenate(outs, axis=-1) * scale_ref[...]


def _even_mix_kernel(att_ref, u_ref, halo_ref, y_ref, wpool_ref, pscale_ref, wo_ref, g_ref, b_ref,
                     o_ref, ctx_ref, *, alpha, blocks_per_seq):
    tm = u_ref.shape[0]
    fw = att_ref.shape[1]
    blk = pl.program_id(0) % blocks_per_seq
    ctx_ref[0:HALO_ROWS, :] = jnp.where(blk == 0, 0.0, halo_ref[...])
    ctx_ref[HALO_ROWS:HALO_ROWS + tm, :] = u_ref[...]
    pos = blk * tm + lax.broadcasted_iota(jnp.int32, (tm, 1), 0)
    n_avail = (pos + 1).astype(F32)
    pool = _pool_groups(ctx_ref, n_avail, wpool_ref, pscale_ref, tm, HALO_ROWS)
    mix = _dot(att_ref[...], wo_ref[0:fw, :]) + _dot(pool.astype(BF16), wo_ref[fw:, :])
    o_ref[...] = _layer_norm(alpha * y_ref[...] + mix, g_ref[...], b_ref[...])


def _even_mix(att, u, y, wpool, pscale, wo, g, b, *, alpha, seq_len, tm):
    rows, d = y.shape
    fw, pw = att.shape[1], u.shape[1]
    halo_blocks = tm // HALO_ROWS
    row = lambda w: pl.BlockSpec((tm, w), lambda i: (i, 0))
    return pl.pallas_call(
        functools.partial(_even_mix_kernel, alpha=alpha, blocks_per_seq=seq_len // tm),
        out_shape=jax.ShapeDtypeStruct((rows, d), F32),
        grid=(rows // tm,),
        in_specs=[row(fw), row(pw),
                  pl.BlockSpec((HALO_ROWS, pw), lambda i: (jnp.maximum(i * halo_blocks - 1, 0), 0)),
                  row(d), _const_spec(wpool.shape), _const_spec(pscale.shape), _const_spec(wo.shape),
                  _const_spec(g.shape), _const_spec(b.shape)],
        out_specs=row(d),
        scratch_shapes=[pltpu.VMEM((HALO_ROWS + tm, pw), F32)],
        compiler_params=_params("parallel"),
        name="even_mix",
    )(att, u, u, y, wpool, pscale, wo, g, b)


def _pool_sample_kernel(ctx_ref, wpool_ref, pscale_ref, o_ref, *, pos0, n_new):
    gw = wpool_ref.shape[1]
    for i in range(n_new):
        n_avail = float(pos0 + i + 1)
        outs = []
        for g, w in enumerate(POOL_WINDOWS):
            lanes = slice(g * gw, (g + 1) * gw)
            cur = ctx_ref[POOL_BUF + i, :, lanes]
            win = cur
            for j in range(1, w):
                win = win + ctx_ref[POOL_BUF + i - j, :, lanes]
            mean = win / min(float(w), n_avail)
            outs.append(_dot((mean - cur).astype(BF16), wpool_ref[g]))
        o_ref[i] = jnp.concatenate(outs, axis=-1) * pscale_ref[...]


def _pool_sample(ctx_tm, wpool, pscale, *, pos0, n_new):
    steps, nb, c = ctx_tm.shape
    return pl.pallas_call(
        functools.partial(_pool_sample_kernel, pos0=pos0, n_new=n_new),
        out_shape=jax.ShapeDtypeStruct((n_new, nb, c), F32),
        grid=(1,),
        in_specs=[_const_spec(ctx_tm.shape), _const_spec(wpool.shape), _const_spec(pscale.shape)],
        out_specs=_const_spec((n_new, nb, c)),
        compiler_params=_params("arbitrary"),
        name="pool_sample",
    )(ctx_tm, wpool, pscale)


def _pair_mix_kernel(a_ref, p_ref, y_ref, wo_ref, g_ref, b_ref, o_ref, *, alpha):
    fw = a_ref.shape[1]
    mix = _dot(a_ref[...], wo_ref[0:fw, :]) + _dot(p_ref[...].astype(BF16), wo_ref[fw:, :])
    o_ref[...] = _layer_norm(alpha * y_ref[...] + mix, g_ref[...], b_ref[...])


def _pair_mix(a, p, y, wo, g, b, *, alpha, tm):
    rows, d = y.shape
    row = lambda w: pl.BlockSpec((tm, w), lambda i: (i, 0))
    return pl.pallas_call(
        functools.partial(_pair_mix_kernel, alpha=alpha),
        out_shape=jax.ShapeDtypeStruct((rows, d), F32),
        grid=(rows // tm,),
        in_specs=[row(a.shape[1]), row(p.shape[1]), row(d), _const_spec(wo.shape),
                  _const_spec(g.shape), _const_spec(b.shape)],
        out_specs=row(d),
        compiler_params=_params("parallel"),
        name="pair_mix",
    )(a, p, y, wo, g, b)


def _sample_new_tokens(qbd_ref, knew_ref, vnew_ref, lnew_ref, m_ref, l_ref, acc_ref, carry_ref, heads):
    qbd = qbd_ref[...]
    lf = lnew_ref[...]
    lane = lax.broadcasted_iota(jnp.int32, lf.shape, 1)
    tok = lax.broadcasted_iota(jnp.int32, lf.shape, 0) // heads
    upto = jnp.where(lax.broadcasted_iota(jnp.int32, (LANES, LANES), 0)
                     <= lax.broadcasted_iota(jnp.int32, (LANES, LANES), 1), 1.0, 0.0).astype(BF16)
    hi, mid, lo = _split3(lf)
    pre = _dot(hi, upto) + _dot(mid, upto) + _dot(lo, upto)
    n_col = jnp.sum(jnp.where(lane == tok, pre, 0.0), axis=1, keepdims=True)
    kn = knew_ref[...].astype(BF16)
    s = _dot_nt(qbd, kn) + n_col - pre[:, 0:SUBLANES]
    key = lax.broadcasted_iota(jnp.int32, s.shape, 1)
    qtok = lax.broadcasted_iota(jnp.int32, s.shape, 0) // heads
    s = jnp.where(key <= qtok, s, NEG_INF)
    m = jnp.max(s, axis=1, keepdims=True)
    p = jnp.exp(s - m)
    m_ref[...] = jnp.broadcast_to(m, m_ref.shape)
    l_ref[...] = jnp.broadcast_to(jnp.sum(p, axis=1, keepdims=True), l_ref.shape)
    acc_ref[...] = _dot(p.astype(BF16), vnew_ref[...].astype(BF16))
    carry_ref[...] = jnp.broadcast_to(n_col, carry_ref.shape)


def _sample_pages(qbd_ref, k_refs, v_refs, lf_refs, m_ref, l_ref, acc_ref, carry_ref, heads):
    pages_per_step = len(k_refs)
    n_rows = qbd_ref.shape[0]
    qbd = qbd_ref[...]
    src = lax.broadcasted_iota(jnp.int32, (PAGE_SIZE, PAGE_SIZE), 0)
    dst = lax.broadcasted_iota(jnp.int32, (PAGE_SIZE, PAGE_SIZE), 1)
    after = jnp.where(src > dst, 1.0, 0.0).astype(BF16)
    reps = n_rows // heads
    lf_all = jnp.concatenate([r[...] for r in lf_refs], axis=0)
    hi, mid, lo = _split3(lf_all)
    within_all = _dot(hi, after) + _dot(mid, after) + _dot(lo, after)
    total_all = jnp.sum(lf_all, axis=1, keepdims=True)
    carry = carry_ref[...]
    s_parts = []
    for t in range(pages_per_step):
        page = slice(t * heads, (t + 1) * heads)
        bias = carry + jnp.tile(within_all[page], (reps, 1))
        s_parts.append(_dot(qbd, k_refs[t][...].astype(BF16)) + bias)
        carry = carry + jnp.tile(total_all[page], (reps, 1))
    carry_ref[...] = carry
    s = jnp.concatenate(s_parts, axis=1)
    m_prev = m_ref[...]
    m_new = jnp.maximum(m_prev, jnp.max(s, axis=1, keepdims=True))
    p = jnp.exp(s - jnp.tile(m_new, (1, pages_per_step)))
    scale = jnp.exp(m_prev - m_new)
    m_ref[...] = m_new
    l_ref[...] = scale * l_ref[...] + jnp.sum(p, axis=1, keepdims=True)
    pb = p.astype(BF16)
    pv = _dot_nt(pb[:, 0:PAGE_SIZE], v_refs[0][...].astype(BF16))
    for t in range(1, pages_per_step):
        pv = pv + _dot_nt(pb[:, t * PAGE_SIZE:(t + 1) * PAGE_SIZE], v_refs[t][...].astype(BF16))
    acc_ref[...] = jnp.tile(scale, (1, acc_ref.shape[1] // LANES)) * acc_ref[...] + pv


def _sample_finish(o_ref, l_ref, acc_ref, heads):
    full = acc_ref[...] / jnp.tile(l_ref[...], (1, acc_ref.shape[1] // LANES))
    row_head = lax.broadcasted_iota(jnp.int32, full.shape, 0) % heads
    lane_head = lax.broadcasted_iota(jnp.int32, full.shape, 1) // D_HEAD
    own = jnp.where(row_head == lane_head, full, 0.0)
    o_ref[...] = jnp.sum(own.reshape(full.shape[0] // heads, heads, full.shape[1]), axis=1).astype(o_ref.dtype)


def _fox_sample_kernel(*refs, pages_per_step, heads):
    qbd_ref, knew_ref, vnew_ref, lnew_ref = refs[1:5]
    k_refs = refs[5:5 + pages_per_step]
    v_refs = refs[5 + pages_per_step:5 + 2 * pages_per_step]
    lf_refs = refs[5 + 2 * pages_per_step:5 + 3 * pages_per_step]
    o_ref, m_ref, l_ref, acc_ref, carry_ref = refs[5 + 3 * pages_per_step:]
    state = (m_ref, l_ref, acc_ref, carry_ref, heads)

    @pl.when(pl.program_id(1) == 0)
    def _():
        _sample_new_tokens(qbd_ref, knew_ref, vnew_ref, lnew_ref, *state)

    _sample_pages(qbd_ref, k_refs, v_refs, lf_refs, *state)

    @pl.when(pl.program_id(1) == pl.num_programs(1) - 1)
    def _():
        _sample_finish(o_ref, l_ref, acc_ref, heads)


def _fox_sample(qbd, knew, vnew, lnew, cache_k, cache_v, cache_lft, page_table, *, heads, n_new,
                pages_per_step):
    nb, n_rows, width = qbd.shape
    n_pages = page_table.shape[1]
    assert n_pages % pages_per_step == 0
    pt = page_table.reshape(-1)

    def page_map(t):
        def index(b, s, pt_ref):
            logical = n_pages - 1 - (s * pages_per_step + t)
            return (pt_ref[b * n_pages + logical], 0, 0)
        return index

    batch3 = lambda shape: pl.BlockSpec((None,) + shape, lambda b, s, pt_ref: (b, 0, 0))
    in_specs = [batch3((n_rows, width)), batch3((SUBLANES, width)), batch3((SUBLANES, width)),
                batch3((n_rows, LANES))]
    in_specs += [pl.BlockSpec((None, width, PAGE_SIZE), page_map(t)) for t in range(pages_per_step)]
    in_specs += [pl.BlockSpec((None, width, PAGE_SIZE), page_map(t)) for t in range(pages_per_step)]
    in_specs += [pl.BlockSpec((None, heads, PAGE_SIZE), page_map(t)) for t in range(pages_per_step)]
    return pl.pallas_call(
        functools.partial(_fox_sample_kernel, pages_per_step=pages_per_step, heads=heads),
        out_shape=jax.ShapeDtypeStruct((nb, n_new, width), BF16),
        grid_spec=pltpu.PrefetchScalarGridSpec(
            num_scalar_prefetch=1,
            grid=(nb, n_pages // pages_per_step),
            in_specs=in_specs,
            out_specs=batch3((n_new, width)),
            scratch_shapes=[pltpu.VMEM((n_rows, LANES), F32), pltpu.VMEM((n_rows, LANES), F32),
                            pltpu.VMEM((n_rows, width), F32), pltpu.VMEM((n_rows, LANES), F32)],
        ),
        compiler_params=_params("parallel", "arbitrary"),
        name="fox_sample",
    )(pt, qbd, knew, vnew, lnew, *([cache_k] * pages_per_step), *([cache_v] * pages_per_step),
      *([cache_lft] * pages_per_step))


def _odd_proj_kernel(*refs, tm):
    n_groups = len(DIL_CONFIGS)
    y_ref, w_ref = refs[0], refs[1]
    de_refs = refs[2:2 + 3 * n_groups]
    tail_refs = refs[2 + 3 * n_groups:2 + 5 * n_groups]
    stage_refs = refs[2 + 5 * n_groups:]
    yb = y_ref[...].astype(BF16)
    n_lane_groups = de_refs[0].shape[1]
    gw = n_lane_groups * LANES
    stage = 0
    for g, (window, dil) in enumerate(DIL_CONFIGS):
        for part in range(3):
            col = (g * 3 + part) * gw
            res = _dot(yb, w_ref[:, col:col + gw])
            if part == 0:
                res = res * (D_HEAD ** -0.5)
            dst = de_refs[g * 3 + part]
            if dil == 1:
                for a in range(n_lane_groups):
                    dst[0, a] = res[:, a * LANES:(a + 1) * LANES].astype(BF16)
            else:
                s_ref = stage_refs[stage]
                stage += 1
                for a in range(n_lane_groups):
                    s_ref[a] = res[:, a * LANES:(a + 1) * LANES]
                for r in range(dil):
                    for a in range(n_lane_groups):
                        dst[r, a] = s_ref[a, pl.ds(r, tm // dil, stride=dil), :].astype(BF16)
            if part > 0:
                keep = min(window, tm)
                tail_refs[g * 2 + part - 1][...] = res[tm - keep:, :].T


def _odd_proj(y3, w, *, tm):
    batch, seq_len, d = y3.shape
    gw = w.shape[1] // (3 * len(DIL_CONFIGS))
    n_lane_groups = gw // LANES
    n_tiles = seq_len // tm
    out_shape, out_specs, n_stage = [], [], 0
    for window, dil in DIL_CONFIGS:
        assert tm % (dil * 16) == 0 and (window % tm == 0 or tm % window == 0)
        for _ in range(3):
            out_shape.append(jax.ShapeDtypeStruct((batch, dil, n_lane_groups, seq_len // dil, LANES), BF16))
            out_specs.append(pl.BlockSpec((None, dil, n_lane_groups, tm // dil, LANES),
                                          lambda b, i: (b, 0, 0, i, 0)))
        n_stage += 3 if dil > 1 else 0
    for window, dil in DIL_CONFIGS:
        keep = min(window, tm)
        first = n_tiles - window // keep
        for _ in range(2):
            out_shape.append(jax.ShapeDtypeStruct((batch, gw, window), F32))
            out_specs.append(pl.BlockSpec((None, gw, keep),
                                          lambda b, i, first=first: (b, 0, jnp.maximum(i - first, 0))))
    return pl.pallas_call(
        functools.partial(_odd_proj_kernel, tm=tm),
        out_shape=tuple(out_shape),
        grid=(batch, n_tiles),
        in_specs=[pl.BlockSpec((None, tm, d), lambda b, i: (b, i, 0)),
                  pl.BlockSpec(w.shape, lambda b, i: (0, 0), pipeline_mode=pl.Buffered(1))],
        out_specs=tuple(out_specs),
        scratch_shapes=[pltpu.VMEM((n_lane_groups, tm, LANES), F32)] * n_stage,
        compiler_params=_params("parallel", "arbitrary"),
        name="odd_proj",
    )(y3, w)


def _bmm_nt(a, b):
    return lax.dot_general(a, b, (((2,), (2,)), ((0,), (0,))), preferred_element_type=F32)


def _bmm(a, b):
    return lax.dot_general(a, b, (((2,), (1,)), ((0,), (0,))), preferred_element_type=F32)


def _dil_attn_kernel(q_ref, kh_ref, kc_ref, vh_ref, vc_ref, bfirst_ref, bprev_ref, bcur_ref, o_ref, lse_ref,
                     *, dil, heads, qb):
    nblk = q_ref.shape[2] // qb
    units = [(r, j) for r in range(dil) for j in range(nblk)]
    even_lane = _lane_is_even_head((1, qb, LANES))

    def cur_blocks(ref, g):
        return jnp.concatenate([ref[r, g].reshape(nblk, qb, LANES) for r in range(dil)], axis=0)

    def prev_blocks(cur, halo_ref, g):
        parts = []
        for r in range(dil):
            parts.append(halo_ref[r, g][None])
            if nblk > 1:
                parts.append(cur[r * nblk:(r + 1) * nblk - 1])
        return jnp.concatenate(parts, axis=0)

    for g in range(heads // HEADS_PER_LANE_GROUP):
        q2, kc, vc = cur_blocks(q_ref, g), cur_blocks(kc_ref, g), cur_blocks(vc_ref, g)
        kp, vp = prev_blocks(kc, kh_ref, g), prev_blocks(vc, vh_ref, g)
        outs, lses = [], []
        for sub in range(HEADS_PER_LANE_GROUP):
            h = g * HEADS_PER_LANE_GROUP + sub
            qh = jnp.where(even_lane if sub == 0 else ~even_lane, q2, jnp.zeros_like(q2))
            bias_prev = jnp.stack([bfirst_ref[h] if j == 0 else bprev_ref[h] for _, j in units])
            sp = _bmm_nt(qh, kp) + bias_prev
            sc = _bmm_nt(qh, kc) + bcur_ref[h][None]
            m = jnp.maximum(jnp.max(sp, axis=2, keepdims=True), jnp.max(sc, axis=2, keepdims=True))
            pp = jnp.exp(sp - m)
            pc = jnp.exp(sc - m)
            den = jnp.sum(pp, axis=2, keepdims=True) + jnp.sum(pc, axis=2, keepdims=True)
            pv = _bmm(pp.astype(BF16), vp) + _bmm(pc.astype(BF16), vc)
            outs.append(pv / den)
            lses.append(jnp.broadcast_to(m + jnp.log(den), pv.shape))
        o2 = jnp.where(even_lane, outs[0], outs[1])
        l2 = jnp.where(even_lane, lses[0], lses[1])
        for u, (r, j) in enumerate(units):
            rows = pl.ds(j * qb, qb) if dil == 1 else pl.ds(r + dil * qb * j, qb, stride=dil)
            o_ref[g, rows, :] = o2[u]
            lse_ref[g, rows, :] = l2[u]


def _dil_attention(q, k, v, bias_first, bias_prev, bias_cur, *, heads, qb, units):
    batch, dil, n_lane_groups, slots, _ = q.shape
    nblk = units // dil
    span = qb * nblk
    cur = pl.BlockSpec((None, dil, n_lane_groups, span, LANES), lambda b, i: (b, 0, 0, i, 0))
    halo = pl.BlockSpec((None, dil, n_lane_groups, qb, LANES),
                        lambda b, i: (b, 0, 0, jnp.maximum(i * nblk - 1, 0), 0))
    nat = pl.BlockSpec((None, n_lane_groups, span * dil, LANES), lambda b, i: (b, 0, i, 0))
    out_sds = jax.ShapeDtypeStruct((batch, n_lane_groups, slots * dil, LANES), F32)
    return pl.pallas_call(
        functools.partial(_dil_attn_kernel, dil=dil, heads=heads, qb=qb),
        out_shape=(out_sds, out_sds),
        grid=(batch, slots // span),
        in_specs=[cur, halo, cur, halo, cur,
                  pl.BlockSpec((None, heads, qb, qb), lambda b, i: (jnp.minimum(i, 1), 0, 0, 0)),
                  _const_spec(bias_prev.shape), _const_spec(bias_cur.shape)],
        out_specs=(nat, nat),
        compiler_params=_params("parallel", "arbitrary"),
        name=f"dil_attention_d{dil}",
    )(q, k, k, v, v, bias_first, bias_prev, bias_cur)


def _odd_mix_kernel(*refs, alpha):
    n = len(DIL_CONFIGS)
    o_refs, lse_refs = refs[:n], refs[n:2 * n]
    y_ref, wo_ref, g_ref, b_ref, out_ref = refs[2 * n:]
    merged = []
    for a in range(o_refs[0].shape[0]):
        lses = [r[a] for r in lse_refs]
        top = functools.reduce(jnp.maximum, lses)
        wts = [jnp.exp(l - top) for l in lses]
        num = functools.reduce(jnp.add, [w * r[a] for w, r in zip(wts, o_refs)])
        merged.append((num / functools.reduce(jnp.add, wts)).astype(BF16))
    mix = _dot(jnp.concatenate(merged, axis=-1), wo_ref[...])
    out_ref[...] = _layer_norm(alpha * y_ref[...] + mix, g_ref[...], b_ref[...])


def _odd_mix(outs, lses, y3, wo, g, b, *, alpha, tm):
    batch, seq_len, d = y3.shape
    n_lane_groups = outs[0].shape[1]
    part = pl.BlockSpec((None, n_lane_groups, tm, LANES), lambda b, i: (b, 0, i, 0))
    row = pl.BlockSpec((None, tm, d), lambda b, i: (b, i, 0))
    return pl.pallas_call(
        functools.partial(_odd_mix_kernel, alpha=alpha),
        out_shape=jax.ShapeDtypeStruct((batch, seq_len, d), F32),
        grid=(batch, seq_len // tm),
        in_specs=[part] * (2 * len(outs)) + [row, _const_spec(wo.shape), _const_spec(g.shape),
                                             _const_spec(b.shape)],
        out_specs=row,
        compiler_params=_params("parallel", "parallel"),
        name="odd_mix",
    )(*outs, *lses, y3, wo, g, b)


def _proj_kernel(x_ref, w_ref, o_ref):
    o_ref[...] = _dot(x_ref[...].astype(BF16), w_ref[...])


def _proj(x, w, *, tn):
    rows, d = x.shape
    n = w.shape[1]
    return pl.pallas_call(
        _proj_kernel,
        out_shape=jax.ShapeDtypeStruct((rows, n), F32),
        grid=(n // tn,),
        in_specs=[_const_spec((rows, d)), pl.BlockSpec((d, tn), lambda j: (0, j))],
        out_specs=pl.BlockSpec((rows, tn), lambda j: (0, j)),
        compiler_params=_params("parallel"),
        name="proj",
    )(x, w)


def _dil_sample_kernel(qbd_ref, kc_ref, vc_ref, kn_ref, vn_ref, tabc_ref, tabn_ref,
                       o_ref, lse_ref, ko_ref, vo_ref, *, n_new, heads):
    width, buf_len = kc_ref.shape
    qbd = qbd_ref[...]
    chunk = min(buf_len, 512)
    starts = range(0, buf_len, chunk)
    sc = [_dot(qbd, kc_ref[:, c0:c0 + chunk].astype(BF16)) + tabc_ref[:, c0:c0 + chunk] for c0 in starts]
    sn = _dot_nt(qbd, kn_ref[...].astype(BF16)) + tabn_ref[...]
    m = functools.reduce(jnp.maximum, [jnp.max(s, axis=1, keepdims=True) for s in sc + [sn]])
    pn = jnp.exp(sn - m)
    den = jnp.sum(pn, axis=1, keepdims=True)
    pv = _dot(pn.astype(BF16), vn_ref[...].astype(BF16))
    for s, c0 in zip(sc, starts):
        p = jnp.exp(s - m)
        den = den + jnp.sum(p, axis=1, keepdims=True)
        pv = pv + _dot_nt(p.astype(BF16), vc_ref[:, c0:c0 + chunk].astype(BF16))
    full = pv / den
    lse = jnp.broadcast_to(m + jnp.log(den), full.shape)
    own = (lax.broadcasted_iota(jnp.int32, full.shape, 0) % heads
           == lax.broadcasted_iota(jnp.int32, full.shape, 1) // D_HEAD)
    o_ref[...] = jnp.sum(jnp.where(own, full, 0.0).reshape(n_new, heads, width), axis=1)
    lse_ref[...] = jnp.sum(jnp.where(own, lse, 0.0).reshape(n_new, heads, width), axis=1)
    rows = 64
    is_new = lax.broadcasted_iota(jnp.int32, (rows, LANES), 1) >= LANES - n_new
    zero_rows = jnp.zeros((LANES - SUBLANES, width), F32)
    for src_ref, new_ref, dst_ref in ((kc_ref, kn_ref, ko_ref), (vc_ref, vn_ref, vo_ref)):
        tail = jnp.concatenate([zero_rows, new_ref[...]], axis=0).T
        for r0 in range(0, width, rows):
            rolled = pltpu.roll(src_ref[r0:r0 + rows, :], buf_len - n_new, axis=1)
            if buf_len > LANES:
                dst_ref[r0:r0 + rows, 0:buf_len - LANES] = rolled[:, 0:buf_len - LANES]
            dst_ref[r0:r0 + rows, buf_len - LANES:] = jnp.where(is_new, tail[r0:r0 + rows, :],
                                                                rolled[:, buf_len - LANES:])


def _dil_sample(qbd, kc, vc, kn, vn, tab_c, tab_n, *, n_new, heads):
    nb, width, buf_len = kc.shape
    n_rows = qbd.shape[1]
    b3 = lambda *shape: pl.BlockSpec((None,) + shape, lambda b: (b,) + (0,) * len(shape))
    small = jax.ShapeDtypeStruct((nb, n_new, width), F32)
    big = jax.ShapeDtypeStruct((nb, width, buf_len), F32)
    return pl.pallas_call(
        functools.partial(_dil_sample_kernel, n_new=n_new, heads=heads),
        out_shape=(small, small, big, big),
        grid=(nb,),
        in_specs=[b3(n_rows, width), b3(width, buf_len), b3(width, buf_len),
                  b3(SUBLANES, width), b3(SUBLANES, width),
                  _const_spec(tab_c.shape), _const_spec(tab_n.shape)],
        out_specs=(b3(n_new, width), b3(n_new, width), b3(width, buf_len), b3(width, buf_len)),
        compiler_params=_params("parallel"),
        name=f"dil_sample_l{buf_len}",
    )(qbd, kc, vc, kn, vn, tab_c, tab_n)


def _rel_bucket(dist):
    exact = REL_BUCKETS // 2
    d = jnp.maximum(dist, 1).astype(F32)
    large = exact + (jnp.log(d / exact) / math.log(REL_MAX_DIST / exact) * (REL_BUCKETS - exact)).astype(jnp.int32)
    large = jnp.minimum(large, REL_BUCKETS - 1)
    return jnp.where(dist < exact, dist, large)


def _group_bias(rel_bias, g, window, dil, heads):
    dist = jnp.arange(window // dil + 1) * dil
    onehot = _rel_bucket(dist)[:, None] == jnp.arange(REL_BUCKETS)[None, :]
    cols = rel_bias[:, g * heads:(g + 1) * heads].astype(F32)
    return jnp.sum(jnp.where(onehot[:, :, None], cols[None], 0.0), axis=1).T


def _toeplitz(w, n):
    heads, period = w.shape
    flat = jnp.tile(w, (1, n))[:, :n * (period - 1)]
    return flat.reshape(heads, n, period - 1)[:, :, :n]


def _prompt_bias_tiles(bias, qb):
    heads, n_keys = bias.shape
    assert n_keys == qb + 1
    neg = jnp.full((heads, qb), NEG_INF, F32)
    cur = _toeplitz(jnp.concatenate([bias[:, 0:qb], neg], axis=1), qb)
    prev = _toeplitz(jnp.concatenate([bias[:, qb:qb + 1], neg, bias[:, 1:qb]], axis=1), qb)
    cur, prev = jnp.swapaxes(cur, 1, 2), jnp.swapaxes(prev, 1, 2)
    return jnp.stack([jnp.full_like(prev, NEG_INF), prev]), prev, cur


def _sample_bias_tables(bias, dil, n_new, buf_len):
    heads, n_keys = bias.shape
    n_back = n_keys - 1
    assert n_back * dil == buf_len
    oldest_first = bias[:, :0:-1]
    gaps = jnp.full((heads, n_back, dil - 1), NEG_INF, F32)
    spread = jnp.concatenate([oldest_first[:, :, None], gaps], axis=2).reshape(heads, buf_len)
    neg_col = jnp.full((heads,), NEG_INF, F32)
    tab_c, tab_n = [], []
    for i in range(n_new):
        tab_c.append(jnp.concatenate([jnp.full((heads, i), NEG_INF, F32), spread[:, :buf_len - i]], axis=1))
        in_set = lambda j: 0 <= j <= i and (i - j) % dil == 0
        tab_n.append(jnp.stack([bias[:, (i - j) // dil] if in_set(j) else neg_col
                                for j in range(n_new - SUBLANES, n_new)], axis=1))
    return jnp.concatenate(tab_c, axis=0), jnp.concatenate(tab_n, axis=0)


PROMPT_ROW_TILE = 512
ATTN_BLOCK = 512
DIL_SLOT_BLOCK = 128
DIL_UNITS_PER_STEP = 16
PAGES_PER_STEP = 32


def _pad_axis(x, axis, size):
    pad = [(0, 0)] * x.ndim
    pad[axis] = (0, size - x.shape[axis])
    return jnp.pad(x, pad)


def token_major(xt, lead):
    return jnp.transpose(xt.reshape(lead, -1, D_HEAD, xt.shape[-1]), (0, 3, 1, 2))


def kernel(x_prompt, x_sample, cache_fox_k, cache_fox_v, cache_fox_logf, state_pool, cache_dil0_k, cache_dil0_v, cache_dil1_k, cache_dil1_v, cache_dil2_k, cache_dil2_v, page_table, p_prompt, p_sample, w_in_even, b_fgate, pool_w, pool_scale, w_out_even, w_in_odd, w_out_odd, rel_bias, ffn1_wg, ffn1_wu, ffn1_wd, ffn2_wg, ffn2_wu, ffn2_wd, ln_g, ln_b, ple_wg, ple_bg, ple_wp):
    depth = ffn1_wg.shape[0]
    alpha = (2 * depth) ** 0.25
    batch, seq_len, d_model = x_prompt.shape
    nb, n_new, _ = x_sample.shape
    past_len = page_table.shape[1] * PAGE_SIZE
    fox_heads = cache_fox_k.shape[-2]
    fox_w = fox_heads * D_HEAD
    dil_heads = cache_dil0_k.shape[-2]
    dil_w = dil_heads * D_HEAD
    dil_caches_k = (cache_dil0_k, cache_dil1_k, cache_dil2_k)
    dil_caches_v = (cache_dil0_v, cache_dil1_v, cache_dil2_v)
    rows_p, rows_s = batch * seq_len, nb * n_new
    tm = PROMPT_ROW_TILE
    bf = lambda w: w.astype(BF16)
    vec = lambda a: a.reshape(1, -1)

    yp = x_prompt.reshape(rows_p, d_model)
    ys = x_sample.reshape(rows_s, d_model)
    outs = {name: [] for name in ("fkp", "fvp", "flp", "fks", "fvs", "fls", "plp", "pls")}
    dkp, dvp, dks, dvs = ([[] for _ in DIL_CONFIGS] for _ in range(4))

    for i in range(depth):
        w1 = (ffn1_wg, ffn1_wu, ffn1_wd, vec(ln_g[i, 0]), vec(ln_b[i, 0]))
        yp = _ffn(yp, *w1, layer=i, alpha=alpha, tm=tm)
        ys = _ffn(ys, *w1, layer=i, alpha=alpha, tm=rows_s)
        g1, b1 = vec(ln_g[i, 1]), vec(ln_b[i, 1])
        if i % 2 == 0:
            e = i // 2
            w_in = w_in_even[e]
            wqkv = bf(w_in[:, :3 * fox_w])
            wf = bf(_pad_axis(w_in[:, 3 * fox_w:3 * fox_w + fox_heads], 1, LANES))
            bfg = _pad_axis(vec(b_fgate[e]), 1, LANES)
            wu = bf(w_in[:, 3 * fox_w + fox_heads:])
            wpool, pscale, wo = bf(pool_w[e]), vec(pool_scale[e]), bf(w_out_even[e])
            qa, ka, va, kt, vt, lft, u = _even_proj(yp.reshape(batch, seq_len, d_model), wqkv, wf, bfg, wu,
                                                    heads=fox_heads, tm=tm, prompt=True)
            att = _fox_attention(qa, ka, va, heads=fox_heads, tq=ATTN_BLOCK).reshape(rows_p, fox_w)
            u = u.reshape(rows_p, -1)
            yp_next = _even_mix(att, u, yp, wpool, pscale, wo, g1, b1, alpha=alpha, seq_len=seq_len, tm=tm)
            outs["fkp"].append(token_major(kt, batch))
            outs["fvp"].append(token_major(vt, batch))
            outs["flp"].append(jnp.swapaxes(lft, 1, 2))
            outs["plp"].append(u.reshape(batch, seq_len, -1)[:, seq_len - POOL_BUF:])
            qs, ks, vs, lfts, us = _even_proj(ys[None], wqkv, wf, bfg, wu, heads=fox_heads, tm=rows_s,
                                              prompt=False)
            head_mask = (jnp.arange(fox_w)[None, :] // D_HEAD == jnp.arange(fox_heads)[:, None]).astype(BF16)
            qbd = (qs.reshape(nb, n_new, 1, fox_w) * head_mask).reshape(nb, n_new * fox_heads, fox_w)
            knew = _pad_axis(ks.reshape(nb, n_new, fox_w), 1, SUBLANES)
            vnew = _pad_axis(vs.reshape(nb, n_new, fox_w), 1, SUBLANES)
            lf3 = jnp.swapaxes(lfts.reshape(fox_heads, nb, n_new), 0, 1)
            lnew = jnp.broadcast_to(lf3[:, None], (nb, n_new, fox_heads, n_new)).reshape(nb, n_new * fox_heads, n_new)
            lnew = _pad_axis(lnew, 2, LANES)
            n_phys = cache_fox_k.shape[1]
            page_t = lambda c: jnp.transpose(c, (0, 2, 3, 1)).reshape(n_phys, fox_w, PAGE_SIZE)
            att_s = _fox_sample(qbd, knew, vnew, lnew, page_t(cache_fox_k[e]), page_t(cache_fox_v[e]),
                                jnp.swapaxes(cache_fox_logf[e], 1, 2), page_table,
                                heads=fox_heads, n_new=n_new, pages_per_step=PAGES_PER_STEP)
            ctx_s = jnp.concatenate([state_pool[e].astype(F32), us.reshape(nb, n_new, -1)], axis=1)
            pool_s = _pool_sample(jnp.swapaxes(ctx_s, 0, 1), wpool, pscale, pos0=past_len, n_new=n_new)
            pool_s = jnp.swapaxes(pool_s, 0, 1).reshape(rows_s, -1)
            ys_next = _pair_mix(att_s.reshape(rows_s, fox_w), pool_s, ys, wo, g1, b1, alpha=alpha, tm=rows_s)
            outs["fks"].append(ks.reshape(nb, n_new, fox_heads, D_HEAD))
            outs["fvs"].append(vs.reshape(nb, n_new, fox_heads, D_HEAD))
            outs["fls"].append(jnp.swapaxes(lf3, 1, 2))
            outs["pls"].append(ctx_s[:, -POOL_BUF:])
        else:
            o = i // 2
            w_in, wo = bf(w_in_odd[o]), bf(w_out_odd[o])
            biases = [_group_bias(rel_bias, g, window, dil, dil_heads) for g, (window, dil) in enumerate(DIL_CONFIGS)]
            proj = _odd_proj(yp.reshape(batch, seq_len, d_model), w_in, tm=tm)
            n_groups = len(DIL_CONFIGS)
            o_parts, lse_parts = [], []
            for g, (window, dil) in enumerate(DIL_CONFIGS):
                qd, kd, vd = proj[3 * g:3 * g + 3]
                bias_first, bias_prev, bias_cur = _prompt_bias_tiles(biases[g], DIL_SLOT_BLOCK)
                og, lg = _dil_attention(qd, kd, vd, bias_first, bias_prev, bias_cur, heads=dil_heads,
                                        qb=DIL_SLOT_BLOCK, units=DIL_UNITS_PER_STEP)
                o_parts.append(og)
                lse_parts.append(lg)
                dkp[g].append(token_major(proj[3 * n_groups + 2 * g], batch))
                dvp[g].append(token_major(proj[3 * n_groups + 2 * g + 1], batch))
            yp_next = _odd_mix(o_parts, lse_parts, yp.reshape(batch, seq_len, d_model), wo, g1, b1,
                               alpha=alpha, tm=tm).reshape(rows_p, d_model)
            lane_major = lambda a: jnp.swapaxes(a.reshape(rows_s, dil_w // LANES, LANES), 0, 1)[None]
            proj_s = _proj(ys, w_in, tn=3 * dil_w).reshape(nb, n_new, n_groups, 3, dil_w)
            head_mask = (jnp.arange(dil_w)[None, :] // D_HEAD == jnp.arange(dil_heads)[:, None]).astype(F32)
            o_parts, lse_parts = [], []
            for g, (window, dil) in enumerate(DIL_CONFIGS):
                qg = proj_s[:, :, g, 0] * (D_HEAD ** -0.5)
                qbd = (qg[:, :, None, :] * head_mask).astype(BF16).reshape(nb, n_new * dil_heads, dil_w)
                k_new, v_new = proj_s[:, :, g, 1], proj_s[:, :, g, 2]
                buf_len = dil_caches_k[g].shape[2]
                pos_minor = lambda c: jnp.transpose(c, (0, 2, 3, 1)).reshape(nb, dil_w, buf_len)
                rows_last = lambda x: jnp.pad(x, ((0, 0), (SUBLANES - n_new, 0), (0, 0)))
                tab_c, tab_n = _sample_bias_tables(biases[g], dil, n_new, buf_len)
                og, lg, k_roll, v_roll = _dil_sample(
                    qbd, pos_minor(dil_caches_k[g][o]), pos_minor(dil_caches_v[g][o]),
                    rows_last(k_new), rows_last(v_new), tab_c, tab_n, n_new=n_new, heads=dil_heads)
                o_parts.append(lane_major(og))
                lse_parts.append(lane_major(lg))
                dks[g].append(token_major(k_roll, nb))
                dvs[g].append(token_major(v_roll, nb))
            ys_next = _odd_mix(o_parts, lse_parts, ys[None], wo, g1, b1, alpha=alpha,
                               tm=rows_s).reshape(rows_s, d_model)
        yp, ys = yp_next, ys_next
        w2 = (ffn2_wg, ffn2_wu, ffn2_wd, vec(ln_g[i, 2]), vec(ln_b[i, 2]))
        ple_w = (ple_wg, vec(ple_bg[i]), ple_wp)
        yp = _ffn(yp, *w2, layer=i, alpha=alpha, tm=tm, ple=(p_prompt.reshape(depth, rows_p, -1),) + ple_w)
        ys = _ffn(ys, *w2, layer=i, alpha=alpha, tm=rows_s, ple=(p_sample.reshape(depth, rows_s, -1),) + ple_w)

    stack = lambda parts: jnp.stack(parts)
    result = [yp.reshape(batch, seq_len, d_model), ys.reshape(nb, n_new, d_model)]
    result += [stack(outs[name]) for name in ("fkp", "fvp", "flp", "fks", "fvs", "fls", "plp", "pls")]
    for g in range(len(DIL_CONFIGS)):
        result += [stack(dkp[g]), stack(dvp[g])]
    for g in range(len(DIL_CONFIGS)):
        result += [stack(dks[g]), stack(dvs[g])]
    return tuple(result)
```

```python
import functools
import math

import jax
import jax.numpy as jnp
import numpy as np
from jax import lax
from jax.experimental import pallas as pl
from jax.experimental.pallas import tpu as pltpu

F32 = jnp.float32
BF16 = jnp.bfloat16

D_HEAD = 64
POOL_WINDOWS = (2, 4, 8, 16)
POOL_BUF = max(POOL_WINDOWS) - 1
DIL_CONFIGS = ((128, 1), (512, 4), (2048, 16))
REL_BUCKETS = 32
REL_MAX_DIST = 2048
LN_EPS = 1e-5
NEG_INF = -1e30
LOG2_E = math.log2(math.e)
LN_2 = math.log(2.0)
PAGE_SIZE = 128

LANES = 128
SUBLANES = 8
VMEM_LIMIT_BYTES = 56 * 1024 * 1024

HEADS_PER_LANE_GROUP = LANES // D_HEAD
HALO_ROWS = 16


def _params(*semantics):
    return pltpu.CompilerParams(dimension_semantics=semantics, vmem_limit_bytes=VMEM_LIMIT_BYTES)


def _dot(a, b):
    return jnp.dot(a, b, preferred_element_type=F32)


def _dot_nt(a, b):
    return lax.dot_general(a, b, (((1,), (1,)), ((), ())), preferred_element_type=F32)


def _layer_norm(z, g, b):
    mu = jnp.mean(z, axis=-1, keepdims=True)
    zc = z - mu
    var = jnp.mean(zc * zc, axis=-1, keepdims=True)
    return zc * lax.rsqrt(var + LN_EPS) * g + b


def _log_sigmoid(x):
    return jnp.minimum(x, 0.0) - jnp.log1p(jnp.exp(-jnp.abs(x)))


def _split3(x):
    hi = x.astype(BF16)
    r1 = x - hi.astype(F32)
    mid = r1.astype(BF16)
    lo = (r1 - mid.astype(F32)).astype(BF16)
    return hi, mid, lo


def _lane_is_even_head(shape):
    return lax.broadcasted_iota(jnp.int32, shape, len(shape) - 1) % LANES < D_HEAD


def _const_spec(shape):
    zeros = (0,) * len(shape)
    return pl.BlockSpec(shape, lambda *_: zeros)


def _ffn_kernel(*refs, alpha, ff_chunk, with_ple):
    if with_ple:
        (x_ref, wg_ref, wu_ref, wd_ref, g_ref, b_ref, p_ref, pwg_ref, pbg_ref, pwp_ref,
         o_ref, h_ref) = refs
    else:
        x_ref, wg_ref, wu_ref, wd_ref, g_ref, b_ref, o_ref, h_ref = refs
    x = x_ref[...]
    xb = x.astype(BF16)
    d_ff = wg_ref.shape[1]
    chunks = [slice(c * ff_chunk, (c + 1) * ff_chunk) for c in range(d_ff // ff_chunk)]
    for cols in chunks:
        gate = _dot(xb, wg_ref[:, cols].astype(BF16))
        up = _dot(xb, wu_ref[:, cols].astype(BF16))
        h_ref[:, cols] = (gate * jax.nn.sigmoid(gate) * up).astype(BF16)
    down = functools.reduce(jnp.add, [_dot(h_ref[:, cols], wd_ref[cols, :].astype(BF16)) for cols in chunks])
    y = _layer_norm(alpha * x + 0.5 * down, g_ref[...], b_ref[...])
    if with_ple:
        gate = jax.nn.sigmoid(_dot(y.astype(BF16), pwg_ref[...].astype(BF16)) + pbg_ref[...])
        y = y + gate * _dot(p_ref[...].astype(BF16), pwp_ref[...].astype(BF16))
    o_ref[...] = y


def _ffn(x, wg, wu, wd, g, b, *, layer, alpha, tm, ple=None):
    rows, d = x.shape
    d_ff = wg.shape[2]
    ff_chunk = 256
    assert rows % tm == 0 and d_ff % ff_chunk == 0
    row_spec = pl.BlockSpec((tm, d), lambda i: (i, 0))
    resident = lambda shape: pl.BlockSpec((None,) + shape, lambda i: (layer, 0, 0), pipeline_mode=pl.Buffered(1))
    in_specs = [row_spec, resident((d, d_ff)), resident((d, d_ff)), resident((d_ff, d)),
                _const_spec((1, d)), _const_spec((1, d))]
    args = [x, wg, wu, wd, g, b]
    if ple is not None:
        p, pwg, pbg, pwp = ple
        in_specs += [pl.BlockSpec((None, tm, p.shape[2]), lambda i: (layer, i, 0)), resident(pwg.shape[1:]),
                     _const_spec((1, d)), resident(pwp.shape[1:])]
        args += [p, pwg, pbg, pwp]
    return pl.pallas_call(
        functools.partial(_ffn_kernel, alpha=alpha, ff_chunk=ff_chunk, with_ple=ple is not None),
        out_shape=jax.ShapeDtypeStruct((rows, d), F32),
        grid=(rows // tm,),
        in_specs=in_specs,
        out_specs=row_spec,
        scratch_shapes=[pltpu.VMEM((tm, d_ff), BF16)],
        compiler_params=_params("parallel"),
        name="ffn_ple" if ple is not None else "ffn",
    )(*args)


N_SPLIT = 3
DECAY_LANE = D_HEAD


def _head_slots(x):
    low = lax.broadcasted_iota(jnp.int32, (x.shape[0], LANES), 1) < D_HEAD
    slots = []
    for g in range(x.shape[1] // LANES):
        pair = x[:, g * LANES:(g + 1) * LANES]
        slots.append(jnp.where(low, pair, 0.0))
        slots.append(jnp.where(low, pltpu.roll(pair, D_HEAD, axis=1), 0.0))
    return slots


def _even_proj_kernel(*refs, prompt):
    if prompt:
        (y_ref, wqkv_ref, wf_ref, bf_ref, wu_ref, place_q_ref, place_k_ref, ones_q_ref, ones_k_ref, ones_v_ref,
         qa_ref, ka_ref, va_ref, kt_ref, vt_ref, logft_ref, u_ref, carry_ref) = refs
    else:
        y_ref, wqkv_ref, wf_ref, bf_ref, wu_ref, q_ref, k_ref, v_ref, logft_ref, u_ref = refs
    yb = y_ref[...].astype(BF16)
    hw = wqkv_ref.shape[1] // 3
    heads = logft_ref.shape[0]
    q = _dot(yb, wqkv_ref[:, 0:hw]) * (D_HEAD ** -0.5 * (LOG2_E if prompt else 1.0))
    k = _dot(yb, wqkv_ref[:, hw:2 * hw])
    v = _dot(yb, wqkv_ref[:, 2 * hw:3 * hw])
    logf = _log_sigmoid(_dot(yb, wf_ref[...]) + bf_ref[...])
    logft_ref[...] = logf.T[:heads, :]
    u_ref[...] = _dot(yb, wu_ref[...])
    if not prompt:
        q_ref[...] = q.astype(BF16)
        k_ref[...] = k
        v_ref[...] = v
        return
    tm = logf.shape[0]
    kt_ref[...] = k.T
    vt_ref[...] = v.T
    @pl.when(pl.program_id(1) == 0)
    def _():
        carry_ref[...] = jnp.zeros_like(carry_ref)

    tri = jnp.where(lax.broadcasted_iota(jnp.int32, (tm, tm), 0) >= lax.broadcasted_iota(jnp.int32, (tm, tm), 1),
                    1.0, 0.0).astype(BF16)
    c = functools.reduce(jnp.add, [_dot(tri, part) for part in _split3(logf)]) + carry_ref[0:1, :]
    carry_ref[...] = jnp.broadcast_to(c[tm - 1:tm, :], carry_ref.shape)
    c_terms = jnp.concatenate(_split3(c * LOG2_E), axis=1)
    decay_q = _dot(c_terms, place_q_ref[...]) + ones_q_ref[...]
    decay_k = _dot(c_terms, place_k_ref[...]) + ones_k_ref[...]
    for h, (qs, ks, vs) in enumerate(zip(_head_slots(q), _head_slots(k), _head_slots(v))):
        lanes = slice(h * LANES, (h + 1) * LANES)
        qa_ref[:, lanes] = (qs + decay_q[:, lanes]).astype(BF16)
        ka_ref[:, lanes] = (ks + decay_k[:, lanes]).astype(BF16)
        va_ref[:, lanes] = (vs + ones_v_ref[:, lanes]).astype(BF16)


def _decay_placement(heads):
    place_q = np.zeros((N_SPLIT * LANES, heads * LANES), np.float32)
    place_k = np.zeros_like(place_q)
    ones_q = np.zeros((1, heads * LANES), np.float32)
    ones_k = np.zeros_like(ones_q)
    ones_v = np.zeros_like(ones_q)
    for h in range(heads):
        base = h * LANES + DECAY_LANE
        for j in range(N_SPLIT):
            place_q[j * LANES + h, base + j] = 1.0
            place_k[j * LANES + h, base + N_SPLIT + j] = -1.0
        ones_q[0, base + N_SPLIT:base + 2 * N_SPLIT] = 1.0
        ones_k[0, base:base + N_SPLIT] = 1.0
        ones_v[0, base:(h + 1) * LANES] = 1.0
    return (jnp.asarray(place_q, BF16), jnp.asarray(place_k, BF16), jnp.asarray(ones_q), jnp.asarray(ones_k),
            jnp.asarray(ones_v))


def _even_proj(y3, wqkv, wf, bf, wu, *, heads, tm, prompt):
    batch, seq_len, d = y3.shape
    hw = wqkv.shape[1] // 3
    pw = wu.shape[1]
    row = lambda w: pl.BlockSpec((None, tm, w), lambda b, i: (b, i, 0))
    col = lambda h: pl.BlockSpec((None, h, tm), lambda b, i: (b, 0, i))
    nat = lambda w, dt: jax.ShapeDtypeStruct((batch, seq_len, w), dt)
    args = [y3, wqkv, wf, bf, wu]
    tail_shape = (jax.ShapeDtypeStruct((batch, heads, seq_len), F32), nat(pw, F32))
    tail_specs = (col(heads), row(pw))
    if prompt:
        args += list(_decay_placement(heads))
        slot_w = heads * LANES
        kv_t = jax.ShapeDtypeStruct((batch, hw, seq_len), F32)
        out_shape = (nat(slot_w, BF16),) * 3 + (kv_t, kv_t) + tail_shape
        out_specs = (row(slot_w),) * 3 + (col(hw), col(hw)) + tail_specs
        scratch = [pltpu.VMEM((SUBLANES, LANES), F32)]
    else:
        out_shape = (nat(hw, BF16), nat(hw, F32), nat(hw, F32)) + tail_shape
        out_specs = (row(hw),) * 3 + tail_specs
        scratch = []
    return pl.pallas_call(
        functools.partial(_even_proj_kernel, prompt=prompt),
        out_shape=out_shape,
        grid=(batch, seq_len // tm),
        in_specs=[row(d)] + [_const_spec(a.shape) for a in args[1:]],
        out_specs=out_specs,
        scratch_shapes=scratch,
        compiler_params=_params("parallel", "arbitrary"),
        name="even_proj",
    )(*args)


SOFTMAX_ROWS = 64
ATTN_BUFFERS = 2


def _attn_step(masked, q_ref, k_ref, v_ref, m_ref, acc_ref, s_ref, p_ref, scale_ref, heads):
    tq, tk = q_ref.shape[0], k_ref.shape[0]
    for h in range(heads):
        lanes = slice(h * LANES, (h + 1) * LANES)
        buf = h % s_ref.shape[0]
        s_ref[buf] = _dot_nt(q_ref[:, lanes], k_ref[:, lanes])
        for r0 in range(0, tq, SOFTMAX_ROWS):
            rows = slice(r0, r0 + SOFTMAX_ROWS)
            s = s_ref[buf, rows, :]
            if masked:
                causal = (lax.broadcasted_iota(jnp.int32, s.shape, 1)
                          <= r0 + lax.broadcasted_iota(jnp.int32, s.shape, 0))
                s = jnp.where(causal, s, NEG_INF)
            m_prev = m_ref[h, rows, :]
            m_new = jnp.maximum(m_prev, jnp.max(s, axis=1, keepdims=True))
            p_ref[buf, rows, :] = jnp.exp2(s - jnp.tile(m_new, (1, tk // LANES))).astype(BF16)
            m_ref[h, rows, :] = m_new
            scale_ref[buf, rows, :] = jnp.exp2(m_prev - m_new)
        acc_ref[h] = scale_ref[buf] * acc_ref[h] + _dot(p_ref[buf], v_ref[:, lanes])


def _attn_finish(o_ref, acc_ref, heads):
    low = lax.broadcasted_iota(jnp.int32, (o_ref.shape[0], LANES), 1) < D_HEAD
    for g in range(heads // HEADS_PER_LANE_GROUP):
        normed = []
        for sub in range(HEADS_PER_LANE_GROUP):
            acc = acc_ref[g * HEADS_PER_LANE_GROUP + sub]
            normed.append(acc / pltpu.roll(acc, D_HEAD, axis=1))
        out = jnp.where(low, normed[0], pltpu.roll(normed[1], D_HEAD, axis=1))
        o_ref[:, g * LANES:(g + 1) * LANES] = out.astype(o_ref.dtype)


def _fox_attn_kernel(qi_ref, ki_ref, q_ref, k_ref, v_ref, o_ref, m_ref, acc_ref, s_ref, p_ref, scale_ref,
                     *, heads):
    pair = pl.program_id(1)
    qi = qi_ref[pair]
    ki = ki_ref[pair]
    host = (q_ref, k_ref, v_ref, m_ref, acc_ref, s_ref, p_ref, scale_ref, heads)

    @pl.when(ki == 0)
    def _():
        m_ref[...] = jnp.full_like(m_ref, NEG_INF)
        acc_ref[...] = jnp.zeros_like(acc_ref)

    @pl.when(ki < qi)
    def _():
        _attn_step(False, *host)

    @pl.when(ki == qi)
    def _():
        _attn_step(True, *host)
        _attn_finish(o_ref, acc_ref, heads)


def _fox_attention(q, k, v, *, heads, tq):
    batch, seq_len, slot_w = q.shape
    nq = seq_len // tq
    qi_tab = np.concatenate([np.full(i + 1, i) for i in range(nq)]).astype(np.int32)
    ki_tab = np.concatenate([np.arange(i + 1) for i in range(nq)]).astype(np.int32)
    q_map = lambda b, p, qi, ki: (b, qi[p], 0)
    k_map = lambda b, p, qi, ki: (b, ki[p], 0)
    return pl.pallas_call(
        functools.partial(_fox_attn_kernel, heads=heads),
        out_shape=jax.ShapeDtypeStruct((batch, seq_len, heads * D_HEAD), BF16),
        grid_spec=pltpu.PrefetchScalarGridSpec(
            num_scalar_prefetch=2,
            grid=(batch, len(qi_tab)),
            in_specs=[pl.BlockSpec((None, tq, slot_w), q_map), pl.BlockSpec((None, tq, slot_w), k_map),
                      pl.BlockSpec((None, tq, slot_w), k_map)],
            out_specs=pl.BlockSpec((None, tq, heads * D_HEAD), q_map),
            scratch_shapes=[pltpu.VMEM((heads, tq, LANES), F32), pltpu.VMEM((heads, tq, LANES), F32),
                            pltpu.VMEM((ATTN_BUFFERS, tq, tq), F32), pltpu.VMEM((ATTN_BUFFERS, tq, tq), BF16),
                            pltpu.VMEM((ATTN_BUFFERS, tq, LANES), F32)],
        ),
        compiler_params=_params("parallel", "arbitrary"),
        name="fox_attention",
    )(jnp.asarray(qi_tab), jnp.asarray(ki_tab), q, k, v)


def _pool_groups(ctx_ref, n_avail, w_pool_ref, scale_ref, rows, halo):
    gw = w_pool_ref.shape[1]
    outs = []
    for g, w in enumerate(POOL_WINDOWS):
        lanes = slice(g * gw, (g + 1) * gw)
        cur = ctx_ref[pl.ds(halo, rows), lanes]
        win = cur
        for j in range(1, w):
            win = win + ctx_ref[pl.ds(halo - j, rows), lanes]
        mean = win / jnp.minimum(float(w), n_avail)
        outs.append(_dot((mean - cur).astype(BF16), w_pool_ref[g]))
    return jnp.concatenate(outs, axis=-1) * scale_ref[...]


def _even_mix_kernel(att_ref, u_ref, halo_ref, y_ref, wpool_ref, pscale_ref, wo_ref, g_ref, b_ref,
                     o_ref, ctx_ref, *, alpha, blocks_per_seq):
    tm = u_ref.shape[0]
    fw = att_ref.shape[1]
    blk = pl.program_id(0) % blocks_per_seq
    ctx_ref[0:HALO_ROWS, :] = jnp.where(blk == 0, 0.0, halo_ref[...])
    ctx_ref[HALO_ROWS:HALO_ROWS + tm, :] = u_ref[...]
    pos = blk * tm + lax.broadcasted_iota(jnp.int32, (tm, 1), 0)
    n_avail = (pos + 1).astype(F32)
    pool = _pool_groups(ctx_ref, n_avail, wpool_ref, pscale_ref, tm, HALO_ROWS)
    mix = _dot(att_ref[...], wo_ref[0:fw, :]) + _dot(pool.astype(BF16), wo_ref[fw:, :])
    o_ref[...] = _layer_norm(alpha * y_ref[...] + mix, g_ref[...], b_ref[...])


def _even_mix(att, u, y, wpool, pscale, wo, g, b, *, alpha, seq_len, tm):
    rows, d = y.shape
    fw, pw = att.shape[1], u.shape[1]
    halo_blocks = tm // HALO_ROWS
    row = lambda w: pl.BlockSpec((tm, w), lambda i: (i, 0))
    return pl.pallas_call(
        functools.partial(_even_mix_kernel, alpha=alpha, blocks_per_seq=seq_len // tm),
        out_shape=jax.ShapeDtypeStruct((rows, d), F32),
        grid=(rows // tm,),
        in_specs=[row(fw), row(pw),
                  pl.BlockSpec((HALO_ROWS, pw), lambda i: (jnp.maximum(i * halo_blocks - 1, 0), 0)),
                  row(d), _const_spec(wpool.shape), _const_spec(pscale.shape), _const_spec(wo.shape),
                  _const_spec(g.shape), _const_spec(b.shape)],
        out_specs=row(d),
        scratch_shapes=[pltpu.VMEM((HALO_ROWS + tm, pw), F32)],
        compiler_params=_params("parallel"),
        name="even_mix",
    )(att, u, u, y, wpool, pscale, wo, g, b)


def _pool_sample_kernel(ctx_ref, wpool_ref, pscale_ref, o_ref, *, pos0, n_new):
    gw = wpool_ref.shape[1]
    for i in range(n_new):
        n_avail = float(pos0 + i + 1)
        outs = []
        for g, w in enumerate(POOL_WINDOWS):
            lanes = slice(g * gw, (g + 1) * gw)
            cur = ctx_ref[POOL_BUF + i, :, lanes]
            win = cur
            for j in range(1, w):
                win = win + ctx_ref[POOL_BUF + i - j, :, lanes]
            mean = win / min(float(w), n_avail)
            outs.append(_dot((mean - cur).astype(BF16), wpool_ref[g]))
        o_ref[i] = jnp.concatenate(outs, axis=-1) * pscale_ref[...]


def _pool_sample(ctx_tm, wpool, pscale, *, pos0, n_new):
    steps, nb, c = ctx_tm.shape
    return pl.pallas_call(
        functools.partial(_pool_sample_kernel, pos0=pos0, n_new=n_new),
        out_shape=jax.ShapeDtypeStruct((n_new, nb, c), F32),
        grid=(1,),
        in_specs=[_const_spec(ctx_tm.shape), _const_spec(wpool.shape), _const_spec(pscale.shape)],
        out_specs=_const_spec((n_new, nb, c)),
        compiler_params=_params("arbitrary"),
        name="pool_sample",
    )(ctx_tm, wpool, pscale)


def _pair_mix_kernel(a_ref, p_ref, y_ref, wo_ref, g_ref, b_ref, o_ref, *, alpha):
    fw = a_ref.shape[1]
    mix = _dot(a_ref[...], wo_ref[0:fw, :]) + _dot(p_ref[...].astype(BF16), wo_ref[fw:, :])
    o_ref[...] = _layer_norm(alpha * y_ref[...] + mix, g_ref[...], b_ref[...])


def _pair_mix(a, p, y, wo, g, b, *, alpha, tm):
    rows, d = y.shape
    row = lambda w: pl.BlockSpec((tm, w), lambda i: (i, 0))
    return pl.pallas_call(
        functools.partial(_pair_mix_kernel, alpha=alpha),
        out_shape=jax.ShapeDtypeStruct((rows, d), F32),
        grid=(rows // tm,),
        in_specs=[row(a.shape[1]), row(p.shape[1]), row(d), _const_spec(wo.shape),
                  _const_spec(g.shape), _const_spec(b.shape)],
        out_specs=row(d),
        compiler_params=_params("parallel"),
        name="pair_mix",
    )(a, p, y, wo, g, b)


def _sample_new_tokens(qbd_ref, knew_ref, vnew_ref, lnew_ref, m_ref, l_ref, acc_ref, carry_ref, heads):
    qbd = qbd_ref[...]
    lf = lnew_ref[...]
    lane = lax.broadcasted_iota(jnp.int32, lf.shape, 1)
    tok = lax.broadcasted_iota(jnp.int32, lf.shape, 0) // heads
    upto = jnp.where(lax.broadcasted_iota(jnp.int32, (LANES, LANES), 0)
                     <= lax.broadcasted_iota(jnp.int32, (LANES, LANES), 1), 1.0, 0.0).astype(BF16)
    hi, mid, lo = _split3(lf)
    pre = _dot(hi, upto) + _dot(mid, upto) + _dot(lo, upto)
    n_col = jnp.sum(jnp.where(lane == tok, pre, 0.0), axis=1, keepdims=True)
    kn = knew_ref[...].astype(BF16)
    s = _dot_nt(qbd, kn) + n_col - pre[:, 0:SUBLANES]
    key = lax.broadcasted_iota(jnp.int32, s.shape, 1)
    qtok = lax.broadcasted_iota(jnp.int32, s.shape, 0) // heads
    s = jnp.where(key <= qtok, s, NEG_INF)
    m = jnp.max(s, axis=1, keepdims=True)
    p = jnp.exp(s - m)
    m_ref[...] = jnp.broadcast_to(m, m_ref.shape)
    l_ref[...] = jnp.broadcast_to(jnp.sum(p, axis=1, keepdims=True), l_ref.shape)
    acc_ref[...] = _dot(p.astype(BF16), vnew_ref[...].astype(BF16))
    carry_ref[...] = jnp.broadcast_to(n_col, carry_ref.shape)


def _sample_pages(qbd_ref, k_refs, v_refs, lf_refs, m_ref, l_ref, acc_ref, carry_ref, heads):
    pages_per_step = len(k_refs)
    n_rows = qbd_ref.shape[0]
    qbd = qbd_ref[...]
    src = lax.broadcasted_iota(jnp.int32, (PAGE_SIZE, PAGE_SIZE), 0)
    dst = lax.broadcasted_iota(jnp.int32, (PAGE_SIZE, PAGE_SIZE), 1)
    after = jnp.where(src > dst, 1.0, 0.0).astype(BF16)
    reps = n_rows // heads
    lf_all = jnp.concatenate([r[...] for r in lf_refs], axis=0)
    hi, mid, lo = _split3(lf_all)
    within_all = _dot(hi, after) + _dot(mid, after) + _dot(lo, after)
    total_all = jnp.sum(lf_all, axis=1, keepdims=True)
    carry = carry_ref[...]
    s_parts = []
    for t in range(pages_per_step):
        page = slice(t * heads, (t + 1) * heads)
        bias = carry + jnp.tile(within_all[page], (reps, 1))
        s_parts.append(_dot(qbd, k_refs[t][...].astype(BF16)) + bias)
        carry = carry + jnp.tile(total_all[page], (reps, 1))
    carry_ref[...] = carry
    s = jnp.concatenate(s_parts, axis=1)
    m_prev = m_ref[...]
    m_new = jnp.maximum(m_prev, jnp.max(s, axis=1, keepdims=True))
    p = jnp.exp(s - jnp.tile(m_new, (1, pages_per_step)))
    scale = jnp.exp(m_prev - m_new)
    m_ref[...] = m_new
    l_ref[...] = scale * l_ref[...] + jnp.sum(p, axis=1, keepdims=True)
    pb = p.astype(BF16)
    pv = _dot_nt(pb[:, 0:PAGE_SIZE], v_refs[0][...].astype(BF16))
    for t in range(1, pages_per_step):
        pv = pv + _dot_nt(pb[:, t * PAGE_SIZE:(t + 1) * PAGE_SIZE], v_refs[t][...].astype(BF16))
    acc_ref[...] = jnp.tile(scale, (1, acc_ref.shape[1] // LANES)) * acc_ref[...] + pv


def _sample_finish(o_ref, l_ref, acc_ref, heads):
    full = acc_ref[...] / jnp.tile(l_ref[...], (1, acc_ref.shape[1] // LANES))
    row_head = lax.broadcasted_iota(jnp.int32, full.shape, 0) % heads
    lane_head = lax.broadcasted_iota(jnp.int32, full.shape, 1) // D_HEAD
    own = jnp.where(row_head == lane_head, full, 0.0)
    o_ref[...] = jnp.sum(own.reshape(full.shape[0] // heads, heads, full.shape[1]), axis=1).astype(o_ref.dtype)


def _fox_sample_kernel(*refs, pages_per_step, heads):
    qbd_ref, knew_ref, vnew_ref, lnew_ref = refs[1:5]
    k_refs = refs[5:5 + pages_per_step]
    v_refs = refs[5 + pages_per_step:5 + 2 * pages_per_step]
    lf_refs = refs[5 + 2 * pages_per_step:5 + 3 * pages_per_step]
    o_ref, m_ref, l_ref, acc_ref, carry_ref = refs[5 + 3 * pages_per_step:]
    state = (m_ref, l_ref, acc_ref, carry_ref, heads)

    @pl.when(pl.program_id(1) == 0)
    def _():
        _sample_new_tokens(qbd_ref, knew_ref, vnew_ref, lnew_ref, *state)

    _sample_pages(qbd_ref, k_refs, v_refs, lf_refs, *state)

    @pl.when(pl.program_id(1) == pl.num_programs(1) - 1)
    def _():
        _sample_finish(o_ref, l_ref, acc_ref, heads)


def _fox_sample(qbd, knew, vnew, lnew, cache_k, cache_v, cache_lft, page_table, *, heads, n_new,
                pages_per_step):
    nb, n_rows, width = qbd.shape
    n_pages = page_table.shape[1]
    assert n_pages % pages_per_step == 0
    pt = page_table.reshape(-1)

    def page_map(t):
        def index(b, s, pt_ref):
            logical = n_pages - 1 - (s * pages_per_step + t)
            return (pt_ref[b * n_pages + logical], 0, 0)
        return index

    batch3 = lambda shape: pl.BlockSpec((None,) + shape, lambda b, s, pt_ref: (b, 0, 0))
    in_specs = [batch3((n_rows, width)), batch3((SUBLANES, width)), batch3((SUBLANES, width)),
                batch3((n_rows, LANES))]
    in_specs += [pl.BlockSpec((None, width, PAGE_SIZE), page_map(t)) for t in range(pages_per_step)]
    in_specs += [pl.BlockSpec((None, width, PAGE_SIZE), page_map(t)) for t in range(pages_per_step)]
    in_specs += [pl.BlockSpec((None, heads, PAGE_SIZE), page_map(t)) for t in range(pages_per_step)]
    return pl.pallas_call(
        functools.partial(_fox_sample_kernel, pages_per_step=pages_per_step, heads=heads),
        out_shape=jax.ShapeDtypeStruct((nb, n_new, width), BF16),
        grid_spec=pltpu.PrefetchScalarGridSpec(
            num_scalar_prefetch=1,
            grid=(nb, n_pages // pages_per_step),
            in_specs=in_specs,
            out_specs=batch3((n_new, width)),
            scratch_shapes=[pltpu.VMEM((n_rows, LANES), F32), pltpu.VMEM((n_rows, LANES), F32),
                            pltpu.VMEM((n_rows, width), F32), pltpu.VMEM((n_rows, LANES), F32)],
        ),
        compiler_params=_params("parallel", "arbitrary"),
        name="fox_sample",
    )(pt, qbd, knew, vnew, lnew, *([cache_k] * pages_per_step), *([cache_v] * pages_per_step),
      *([cache_lft] * pages_per_step))


def _odd_proj_kernel(*refs, tm):
    n_groups = len(DIL_CONFIGS)
    y_ref, w_ref = refs[0], refs[1]
    de_refs = refs[2:2 + 3 * n_groups]
    tail_refs = refs[2 + 3 * n_groups:2 + 5 * n_groups]
    stage_refs = refs[2 + 5 * n_groups:]
    yb = y_ref[...].astype(BF16)
    n_lane_groups = de_refs[0].shape[1]
    gw = n_lane_groups * LANES
    stage = 0
    for g, (window, dil) in enumerate(DIL_CONFIGS):
        for part in range(3):
            col = (g * 3 + part) * gw
            res = _dot(yb, w_ref[:, col:col + gw])
            if part == 0:
                res = res * (D_HEAD ** -0.5 * LOG2_E)
            dst = de_refs[g * 3 + part]
            if dil == 1:
                for a in range(n_lane_groups):
                    dst[0, a] = res[:, a * LANES:(a + 1) * LANES].astype(BF16)
            else:
                s_ref = stage_refs[stage]
                stage += 1
                for a in range(n_lane_groups):
                    s_ref[a] = res[:, a * LANES:(a + 1) * LANES]
                for r in range(dil):
                    for a in range(n_lane_groups):
                        dst[r, a] = s_ref[a, pl.ds(r, tm // dil, stride=dil), :].astype(BF16)
            if part > 0:
                keep = min(window, tm)
                tail_refs[g * 2 + part - 1][...] = res[tm - keep:, :].T


def _odd_proj(y3, w, *, tm):
    batch, seq_len, d = y3.shape
    gw = w.shape[1] // (3 * len(DIL_CONFIGS))
    n_lane_groups = gw // LANES
    n_tiles = seq_len // tm
    out_shape, out_specs, n_stage = [], [], 0
    for window, dil in DIL_CONFIGS:
        assert tm % (dil * 16) == 0 and (window % tm == 0 or tm % window == 0)
        for _ in range(3):
            out_shape.append(jax.ShapeDtypeStruct((batch, dil, n_lane_groups, seq_len // dil, LANES), BF16))
            out_specs.append(pl.BlockSpec((None, dil, n_lane_groups, tm // dil, LANES),
                                          lambda b, i: (b, 0, 0, i, 0)))
        n_stage += 3 if dil > 1 else 0
    for window, dil in DIL_CONFIGS:
        keep = min(window, tm)
        first = n_tiles - window // keep
        for _ in range(2):
            out_shape.append(jax.ShapeDtypeStruct((batch, gw, window), F32))
            out_specs.append(pl.BlockSpec((None, gw, keep),
                                          lambda b, i, first=first: (b, 0, jnp.maximum(i - first, 0))))
    return pl.pallas_call(
        functools.partial(_odd_proj_kernel, tm=tm),
        out_shape=tuple(out_shape),
        grid=(batch, n_tiles),
        in_specs=[pl.BlockSpec((None, tm, d), lambda b, i: (b, i, 0)),
                  pl.BlockSpec(w.shape, lambda b, i: (0, 0), pipeline_mode=pl.Buffered(1))],
        out_specs=tuple(out_specs),
        scratch_shapes=[pltpu.VMEM((n_lane_groups, tm, LANES), F32)] * n_stage,
        compiler_params=_params("parallel", "arbitrary"),
        name="odd_proj",
    )(y3, w)


def _bmm_nt(a, b):
    return lax.dot_general(a, b, (((2,), (2,)), ((0,), (0,))), preferred_element_type=F32)


def _bmm(a, b):
    return lax.dot_general(a, b, (((2,), (1,)), ((0,), (0,))), preferred_element_type=F32)


def _dil_attn_kernel(q_ref, kh_ref, kc_ref, vh_ref, vc_ref, bfirst_ref, bprev_ref, o_ref, lse_ref,
                     *, dil, heads, qb):
    nblk = q_ref.shape[2] // qb
    units = [(r, j) for r in range(dil) for j in range(nblk)]
    even_lane = _lane_is_even_head((1, qb, LANES))

    def cur_blocks(ref, g):
        return jnp.concatenate([ref[r, g].reshape(nblk, qb, LANES) for r in range(dil)], axis=0)

    def prev_blocks(cur, halo_ref, g):
        parts = []
        for r in range(dil):
            parts.append(halo_ref[r, g][None])
            if nblk > 1:
                parts.append(cur[r * nblk:(r + 1) * nblk - 1])
        return jnp.concatenate(parts, axis=0)

    for g in range(heads // HEADS_PER_LANE_GROUP):
        q2, kc, vc = cur_blocks(q_ref, g), cur_blocks(kc_ref, g), cur_blocks(vc_ref, g)
        k2 = jnp.concatenate([prev_blocks(kc, kh_ref, g), kc], axis=1)
        v2 = jnp.concatenate([prev_blocks(vc, vh_ref, g), vc], axis=1)
        outs, lses = [], []
        for sub in range(HEADS_PER_LANE_GROUP):
            h = g * HEADS_PER_LANE_GROUP + sub
            qh = jnp.where(even_lane if sub == 0 else ~even_lane, q2, jnp.zeros_like(q2))
            bias = jnp.stack([bfirst_ref[h] if j == 0 else bprev_ref[h] for _, j in units])
            s = _bmm_nt(qh, k2) + bias
            m = jnp.max(s, axis=2, keepdims=True)
            p = jnp.exp2(s - m)
            den = jnp.sum(p, axis=2, keepdims=True)
            pv = _bmm(p.astype(BF16), v2)
            outs.append(pv / den)
            lses.append(jnp.broadcast_to(m * LN_2 + jnp.log(den), pv.shape))
        o2 = jnp.where(even_lane, outs[0], outs[1])
        l2 = jnp.where(even_lane, lses[0], lses[1])
        for u, (r, j) in enumerate(units):
            rows = pl.ds(j * qb, qb) if dil == 1 else pl.ds(r + dil * qb * j, qb, stride=dil)
            o_ref[g, rows, :] = o2[u]
            lse_ref[g, rows, :] = l2[u]


def _dil_attention(q, k, v, bias_first, bias, *, heads, qb, units):
    batch, dil, n_lane_groups, slots, _ = q.shape
    nblk = units // dil
    span = qb * nblk
    cur = pl.BlockSpec((None, dil, n_lane_groups, span, LANES), lambda b, i: (b, 0, 0, i, 0))
    halo = pl.BlockSpec((None, dil, n_lane_groups, qb, LANES),
                        lambda b, i: (b, 0, 0, jnp.maximum(i * nblk - 1, 0), 0))
    nat = pl.BlockSpec((None, n_lane_groups, span * dil, LANES), lambda b, i: (b, 0, i, 0))
    out_sds = jax.ShapeDtypeStruct((batch, n_lane_groups, slots * dil, LANES), F32)
    return pl.pallas_call(
        functools.partial(_dil_attn_kernel, dil=dil, heads=heads, qb=qb),
        out_shape=(out_sds, out_sds),
        grid=(batch, slots // span),
        in_specs=[cur, halo, cur, halo, cur,
                  pl.BlockSpec((None, heads, qb, 2 * qb), lambda b, i: (jnp.minimum(i, 1), 0, 0, 0)),
                  _const_spec(bias.shape)],
        out_specs=(nat, nat),
        compiler_params=_params("parallel", "arbitrary"),
        name=f"dil_attention_d{dil}",
    )(q, k, k, v, v, bias_first, bias)


def _odd_mix_kernel(*refs, alpha):
    n = len(DIL_CONFIGS)
    o_refs, lse_refs = refs[:n], refs[n:2 * n]
    y_ref, wo_ref, g_ref, b_ref, out_ref = refs[2 * n:]
    merged = []
    for a in range(o_refs[0].shape[0]):
        lses = [r[a] for r in lse_refs]
        top = functools.reduce(jnp.maximum, lses)
        wts = [jnp.exp(l - top) for l in lses]
        num = functools.reduce(jnp.add, [w * r[a] for w, r in zip(wts, o_refs)])
        merged.append((num / functools.reduce(jnp.add, wts)).astype(BF16))
    mix = _dot(jnp.concatenate(merged, axis=-1), wo_ref[...])
    out_ref[...] = _layer_norm(alpha * y_ref[...] + mix, g_ref[...], b_ref[...])


def _odd_mix(outs, lses, y3, wo, g, b, *, alpha, tm):
    batch, seq_len, d = y3.shape
    n_lane_groups = outs[0].shape[1]
    part = pl.BlockSpec((None, n_lane_groups, tm, LANES), lambda b, i: (b, 0, i, 0))
    row = pl.BlockSpec((None, tm, d), lambda b, i: (b, i, 0))
    return pl.pallas_call(
        functools.partial(_odd_mix_kernel, alpha=alpha),
        out_shape=jax.ShapeDtypeStruct((batch, seq_len, d), F32),
        grid=(batch, seq_len // tm),
        in_specs=[part] * (2 * len(outs)) + [row, _const_spec(wo.shape), _const_spec(g.shape),
                                             _const_spec(b.shape)],
        out_specs=row,
        compiler_params=_params("parallel", "parallel"),
        name="odd_mix",
    )(*outs, *lses, y3, wo, g, b)


def _proj_kernel(x_ref, w_ref, o_ref):
    o_ref[...] = _dot(x_ref[...].astype(BF16), w_ref[...])


def _proj(x, w, *, tn):
    rows, d = x.shape
    n = w.shape[1]
    return pl.pallas_call(
        _proj_kernel,
        out_shape=jax.ShapeDtypeStruct((rows, n), F32),
        grid=(n // tn,),
        in_specs=[_const_spec((rows, d)), pl.BlockSpec((d, tn), lambda j: (0, j))],
        out_specs=pl.BlockSpec((rows, tn), lambda j: (0, j)),
        compiler_params=_params("parallel"),
        name="proj",
    )(x, w)


def _dil_sample_kernel(qbd_ref, kc_ref, vc_ref, kn_ref, vn_ref, tabc_ref, tabn_ref,
                       o_ref, lse_ref, ko_ref, vo_ref, *, n_new, heads):
    width, buf_len = kc_ref.shape
    qbd = qbd_ref[...]
    chunk = min(buf_len, 512)
    starts = range(0, buf_len, chunk)
    sc = [_dot(qbd, kc_ref[:, c0:c0 + chunk].astype(BF16)) + tabc_ref[:, c0:c0 + chunk] for c0 in starts]
    sn = _dot_nt(qbd, kn_ref[...].astype(BF16)) + tabn_ref[...]
    m = functools.reduce(jnp.maximum, [jnp.max(s, axis=1, keepdims=True) for s in sc + [sn]])
    pn = jnp.exp(sn - m)
    den = jnp.sum(pn, axis=1, keepdims=True)
    pv = _dot(pn.astype(BF16), vn_ref[...].astype(BF16))
    for s, c0 in zip(sc, starts):
        p = jnp.exp(s - m)
        den = den + jnp.sum(p, axis=1, keepdims=True)
        pv = pv + _dot_nt(p.astype(BF16), vc_ref[:, c0:c0 + chunk].astype(BF16))
    full = pv / den
    lse = jnp.broadcast_to(m + jnp.log(den), full.shape)
    own = (lax.broadcasted_iota(jnp.int32, full.shape, 0) % heads
           == lax.broadcasted_iota(jnp.int32, full.shape, 1) // D_HEAD)
    o_ref[...] = jnp.sum(jnp.where(own, full, 0.0).reshape(n_new, heads, width), axis=1)
    lse_ref[...] = jnp.sum(jnp.where(own, lse, 0.0).reshape(n_new, heads, width), axis=1)
    rows = 64
    is_new = lax.broadcasted_iota(jnp.int32, (rows, LANES), 1) >= LANES - n_new
    zero_rows = jnp.zeros((LANES - SUBLANES, width), F32)
    for src_ref, new_ref, dst_ref in ((kc_ref, kn_ref, ko_ref), (vc_ref, vn_ref, vo_ref)):
        tail = jnp.concatenate([zero_rows, new_ref[...]], axis=0).T
        for r0 in range(0, width, rows):
            rolled = pltpu.roll(src_ref[r0:r0 + rows, :], buf_len - n_new, axis=1)
            if buf_len > LANES:
                dst_ref[r0:r0 + rows, 0:buf_len - LANES] = rolled[:, 0:buf_len - LANES]
            dst_ref[r0:r0 + rows, buf_len - LANES:] = jnp.where(is_new, tail[r0:r0 + rows, :],
                                                                rolled[:, buf_len - LANES:])


def _dil_sample(qbd, kc, vc, kn, vn, tab_c, tab_n, *, n_new, heads):
    nb, width, buf_len = kc.shape
    n_rows = qbd.shape[1]
    b3 = lambda *shape: pl.BlockSpec((None,) + shape, lambda b: (b,) + (0,) * len(shape))
    small = jax.ShapeDtypeStruct((nb, n_new, width), F32)
    big = jax.ShapeDtypeStruct((nb, width, buf_len), F32)
    return pl.pallas_call(
        functools.partial(_dil_sample_kernel, n_new=n_new, heads=heads),
        out_shape=(small, small, big, big),
        grid=(nb,),
        in_specs=[b3(n_rows, width), b3(width, buf_len), b3(width, buf_len),
                  b3(SUBLANES, width), b3(SUBLANES, width),
                  _const_spec(tab_c.shape), _const_spec(tab_n.shape)],
        out_specs=(b3(n_new, width), b3(n_new, width), b3(width, buf_len), b3(width, buf_len)),
        compiler_params=_params("parallel"),
        name=f"dil_sample_l{buf_len}",
    )(qbd, kc, vc, kn, vn, tab_c, tab_n)


def _rel_bucket(dist):
    exact = REL_BUCKETS // 2
    d = jnp.maximum(dist, 1).astype(F32)
    large = exact + (jnp.log(d / exact) / math.log(REL_MAX_DIST / exact) * (REL_BUCKETS - exact)).astype(jnp.int32)
    large = jnp.minimum(large, REL_BUCKETS - 1)
    return jnp.where(dist < exact, dist, large)


def _group_bias(rel_bias, g, window, dil, heads):
    dist = jnp.arange(window // dil + 1) * dil
    onehot = _rel_bucket(dist)[:, None] == jnp.arange(REL_BUCKETS)[None, :]
    cols = rel_bias[:, g * heads:(g + 1) * heads].astype(F32)
    return jnp.sum(jnp.where(onehot[:, :, None], cols[None], 0.0), axis=1).T


def _toeplitz(w, n):
    heads, period = w.shape
    flat = jnp.tile(w, (1, n))[:, :n * (period - 1)]
    return flat.reshape(heads, n, period - 1)[:, :, :n]


def _prompt_bias_tiles(bias, qb):
    heads, n_keys = bias.shape
    assert n_keys == qb + 1
    neg = jnp.full((heads, qb), NEG_INF, F32)
    cur = _toeplitz(jnp.concatenate([bias[:, 0:qb], neg], axis=1), qb)
    prev = _toeplitz(jnp.concatenate([bias[:, qb:qb + 1], neg, bias[:, 1:qb]], axis=1), qb)
    cur, prev = jnp.swapaxes(cur, 1, 2), jnp.swapaxes(prev, 1, 2)
    both = jnp.concatenate([prev, cur], axis=2)
    no_prev = jnp.concatenate([jnp.full_like(prev, NEG_INF), cur], axis=2)
    return jnp.stack([no_prev, both]), both


def _sample_bias_tables(bias, dil, n_new, buf_len):
    heads, n_keys = bias.shape
    n_back = n_keys - 1
    assert n_back * dil == buf_len
    oldest_first = bias[:, :0:-1]
    gaps = jnp.full((heads, n_back, dil - 1), NEG_INF, F32)
    spread = jnp.concatenate([oldest_first[:, :, None], gaps], axis=2).reshape(heads, buf_len)
    neg_col = jnp.full((heads,), NEG_INF, F32)
    tab_c, tab_n = [], []
    for i in range(n_new):
        tab_c.append(jnp.concatenate([jnp.full((heads, i), NEG_INF, F32), spread[:, :buf_len - i]], axis=1))
        in_set = lambda j: 0 <= j <= i and (i - j) % dil == 0
        tab_n.append(jnp.stack([bias[:, (i - j) // dil] if in_set(j) else neg_col
                                for j in range(n_new - SUBLANES, n_new)], axis=1))
    return jnp.concatenate(tab_c, axis=0), jnp.concatenate(tab_n, axis=0)


PROMPT_ROW_TILE = 512
ATTN_BLOCK = 512
DIL_SLOT_BLOCK = 128
DIL_UNITS_PER_STEP = 16
PAGES_PER_STEP = 32


def _pad_axis(x, axis, size):
    pad = [(0, 0)] * x.ndim
    pad[axis] = (0, size - x.shape[axis])
    return jnp.pad(x, pad)


def token_major(xt, lead):
    return jnp.transpose(xt.reshape(lead, -1, D_HEAD, xt.shape[-1]), (0, 3, 1, 2))


def kernel(x_prompt, x_sample, cache_fox_k, cache_fox_v, cache_fox_logf, state_pool, cache_dil0_k, cache_dil0_v, cache_dil1_k, cache_dil1_v, cache_dil2_k, cache_dil2_v, page_table, p_prompt, p_sample, w_in_even, b_fgate, pool_w, pool_scale, w_out_even, w_in_odd, w_out_odd, rel_bias, ffn1_wg, ffn1_wu, ffn1_wd, ffn2_wg, ffn2_wu, ffn2_wd, ln_g, ln_b, ple_wg, ple_bg, ple_wp):
    depth = ffn1_wg.shape[0]
    alpha = (2 * depth) ** 0.25
    batch, seq_len, d_model = x_prompt.shape
    nb, n_new, _ = x_sample.shape
    past_len = page_table.shape[1] * PAGE_SIZE
    fox_heads = cache_fox_k.shape[-2]
    fox_w = fox_heads * D_HEAD
    dil_heads = cache_dil0_k.shape[-2]
    dil_w = dil_heads * D_HEAD
    dil_caches_k = (cache_dil0_k, cache_dil1_k, cache_dil2_k)
    dil_caches_v = (cache_dil0_v, cache_dil1_v, cache_dil2_v)
    rows_p, rows_s = batch * seq_len, nb * n_new
    tm = PROMPT_ROW_TILE
    bf = lambda w: w.astype(BF16)
    vec = lambda a: a.reshape(1, -1)

    yp = x_prompt.reshape(rows_p, d_model)
    ys = x_sample.reshape(rows_s, d_model)
    outs = {name: [] for name in ("fkp", "fvp", "flp", "fks", "fvs", "fls", "plp", "pls")}
    dkp, dvp, dks, dvs = ([[] for _ in DIL_CONFIGS] for _ in range(4))

    for i in range(depth):
        w1 = (ffn1_wg, ffn1_wu, ffn1_wd, vec(ln_g[i, 0]), vec(ln_b[i, 0]))
        yp = _ffn(yp, *w1, layer=i, alpha=alpha, tm=tm)
        ys = _ffn(ys, *w1, layer=i, alpha=alpha, tm=rows_s)
        g1, b1 = vec(ln_g[i, 1]), vec(ln_b[i, 1])
        if i % 2 == 0:
            e = i // 2
            w_in = w_in_even[e]
            wqkv = bf(w_in[:, :3 * fox_w])
            wf = bf(_pad_axis(w_in[:, 3 * fox_w:3 * fox_w + fox_heads], 1, LANES))
            bfg = _pad_axis(vec(b_fgate[e]), 1, LANES)
            wu = bf(w_in[:, 3 * fox_w + fox_heads:])
            wpool, pscale, wo = bf(pool_w[e]), vec(pool_scale[e]), bf(w_out_even[e])
            qa, ka, va, kt, vt, lft, u = _even_proj(yp.reshape(batch, seq_len, d_model), wqkv, wf, bfg, wu,
                                                    heads=fox_heads, tm=tm, prompt=True)
            att = _fox_attention(qa, ka, va, heads=fox_heads, tq=ATTN_BLOCK).reshape(rows_p, fox_w)
            u = u.reshape(rows_p, -1)
            yp_next = _even_mix(att, u, yp, wpool, pscale, wo, g1, b1, alpha=alpha, seq_len=seq_len, tm=tm)
            outs["fkp"].append(token_major(kt, batch))
            outs["fvp"].append(token_major(vt, batch))
            outs["flp"].append(jnp.swapaxes(lft, 1, 2))
            outs["plp"].append(u.reshape(batch, seq_len, -1)[:, seq_len - POOL_BUF:])
            qs, ks, vs, lfts, us = _even_proj(ys[None], wqkv, wf, bfg, wu, heads=fox_heads, tm=rows_s,
                                              prompt=False)
            head_mask = (jnp.arange(fox_w)[None, :] // D_HEAD == jnp.arange(fox_heads)[:, None]).astype(BF16)
            qbd = (qs.reshape(nb, n_new, 1, fox_w) * head_mask).reshape(nb, n_new * fox_heads, fox_w)
            knew = _pad_axis(ks.reshape(nb, n_new, fox_w), 1, SUBLANES)
            vnew = _pad_axis(vs.reshape(nb, n_new, fox_w), 1, SUBLANES)
            lf3 = jnp.swapaxes(lfts.reshape(fox_heads, nb, n_new), 0, 1)
            lnew = jnp.broadcast_to(lf3[:, None], (nb, n_new, fox_heads, n_new)).reshape(nb, n_new * fox_heads, n_new)
            lnew = _pad_axis(lnew, 2, LANES)
            n_phys = cache_fox_k.shape[1]
            page_t = lambda c: jnp.transpose(c, (0, 2, 3, 1)).reshape(n_phys, fox_w, PAGE_SIZE)
            att_s = _fox_sample(qbd, knew, vnew, lnew, page_t(cache_fox_k[e]), page_t(cache_fox_v[e]),
                                jnp.swapaxes(cache_fox_logf[e], 1, 2), page_table,
                                heads=fox_heads, n_new=n_new, pages_per_step=PAGES_PER_STEP)
            ctx_s = jnp.concatenate([state_pool[e].astype(F32), us.reshape(nb, n_new, -1)], axis=1)
            pool_s = _pool_sample(jnp.swapaxes(ctx_s, 0, 1), wpool, pscale, pos0=past_len, n_new=n_new)
            pool_s = jnp.swapaxes(pool_s, 0, 1).reshape(rows_s, -1)
            ys_next = _pair_mix(att_s.reshape(rows_s, fox_w), pool_s, ys, wo, g1, b1, alpha=alpha, tm=rows_s)
            outs["fks"].append(ks.reshape(nb, n_new, fox_heads, D_HEAD))
            outs["fvs"].append(vs.reshape(nb, n_new, fox_heads, D_HEAD))
            outs["fls"].append(jnp.swapaxes(lf3, 1, 2))
            outs["pls"].append(ctx_s[:, -POOL_BUF:])
        else:
            o = i // 2
            w_in, wo = bf(w_in_odd[o]), bf(w_out_odd[o])
            biases = [_group_bias(rel_bias, g, window, dil, dil_heads) for g, (window, dil) in enumerate(DIL_CONFIGS)]
            proj = _odd_proj(yp.reshape(batch, seq_len, d_model), w_in, tm=tm)
            n_groups = len(DIL_CONFIGS)
            o_parts, lse_parts = [], []
            for g, (window, dil) in enumerate(DIL_CONFIGS):
                qd, kd, vd = proj[3 * g:3 * g + 3]
                bias_first, bias_both = _prompt_bias_tiles(biases[g] * LOG2_E, DIL_SLOT_BLOCK)
                og, lg = _dil_attention(qd, kd, vd, bias_first, bias_both, heads=dil_heads,
                                        qb=DIL_SLOT_BLOCK, units=DIL_UNITS_PER_STEP)
                o_parts.append(og)
                lse_parts.append(lg)
                dkp[g].append(token_major(proj[3 * n_groups + 2 * g], batch))
                dvp[g].append(token_major(proj[3 * n_groups + 2 * g + 1], batch))
            yp_next = _odd_mix(o_parts, lse_parts, yp.reshape(batch, seq_len, d_model), wo, g1, b1,
                               alpha=alpha, tm=tm).reshape(rows_p, d_model)
            lane_major = lambda a: jnp.swapaxes(a.reshape(rows_s, dil_w // LANES, LANES), 0, 1)[None]
            proj_s = _proj(ys, w_in, tn=3 * dil_w).reshape(nb, n_new, n_groups, 3, dil_w)
            head_mask = (jnp.arange(dil_w)[None, :] // D_HEAD == jnp.arange(dil_heads)[:, None]).astype(F32)
            o_parts, lse_parts = [], []
            for g, (window, dil) in enumerate(DIL_CONFIGS):
                qg = proj_s[:, :, g, 0] * (D_HEAD ** -0.5)
                qbd = (qg[:, :, None, :] * head_mask).astype(BF16).reshape(nb, n_new * dil_heads, dil_w)
                k_new, v_new = proj_s[:, :, g, 1], proj_s[:, :, g, 2]
                buf_len = dil_caches_k[g].shape[2]
                pos_minor = lambda c: jnp.transpose(c, (0, 2, 3, 1)).reshape(nb, dil_w, buf_len)
                rows_last = lambda x: jnp.pad(x, ((0, 0), (SUBLANES - n_new, 0), (0, 0)))
                tab_c, tab_n = _sample_bias_tables(biases[g], dil, n_new, buf_len)
                og, lg, k_roll, v_roll = _dil_sample(
                    qbd, pos_minor(dil_caches_k[g][o]), pos_minor(dil_caches_v[g][o]),
                    rows_last(k_new), rows_last(v_new), tab_c, tab_n, n_new=n_new, heads=dil_heads)
                o_parts.append(lane_major(og))
                lse_parts.append(lane_major(lg))
                dks[g].append(token_major(k_roll, nb))
                dvs[g].append(token_major(v_roll, nb))
            ys_next = _odd_mix(o_parts, lse_parts, ys[None], wo, g1, b1, alpha=alpha,
                               tm=rows_s).reshape(rows_s, d_model)
        yp, ys = yp_next, ys_next
        w2 = (ffn2_wg, ffn2_wu, ffn2_wd, vec(ln_g[i, 2]), vec(ln_b[i, 2]))
        ple_w = (ple_wg, vec(ple_bg[i]), ple_wp)
        yp = _ffn(yp, *w2, layer=i, alpha=alpha, tm=tm, ple=(p_prompt.reshape(depth, rows_p, -1),) + ple_w)
        ys = _ffn(ys, *w2, layer=i, alpha=alpha, tm=rows_s, ple=(p_sample.reshape(depth, rows_s, -1),) + ple_w)

    stack = lambda parts: jnp.stack(parts)
    result = [yp.reshape(batch, seq_len, d_model), ys.reshape(nb, n_new, d_model)]
    result += [stack(outs[name]) for name in ("fkp", "fvp", "flp", "fks", "fvs", "fls", "plp", "pls")]
    for g in range(len(DIL_CONFIGS)):
        result += [stack(dkp[g]), stack(dvp[g])]
    for g in range(len(DIL_CONFIGS)):
        result += [stack(dks[g]), stack(dvs[g])]
    return tuple(result)
```

```python
import functools
import math

import jax
import jax.numpy as jnp
import numpy as np
from jax import lax
from jax.experimental import pallas as pl
from jax.experimental.pallas import tpu as pltpu

F32 = jnp.float32
BF16 = jnp.bfloat16

D_HEAD = 64
POOL_WINDOWS = (2, 4, 8, 16)
POOL_BUF = max(POOL_WINDOWS) - 1
DIL_CONFIGS = ((128, 1), (512, 4), (2048, 16))
REL_BUCKETS = 32
REL_MAX_DIST = 2048
LN_EPS = 1e-5
NEG_INF = -1e30
LOG2_E = math.log2(math.e)
LN_2 = math.log(2.0)
PAGE_SIZE = 128

LANES = 128
SUBLANES = 8
VMEM_LIMIT_BYTES = 56 * 1024 * 1024

HEADS_PER_LANE_GROUP = LANES // D_HEAD
HALO_ROWS = 16


def _params(*semantics):
    return pltpu.CompilerParams(dimension_semantics=semantics, vmem_limit_bytes=VMEM_LIMIT_BYTES)


def _dot(a, b):
    return jnp.dot(a, b, preferred_element_type=F32)


def _dot_nt(a, b):
    return lax.dot_general(a, b, (((1,), (1,)), ((), ())), preferred_element_type=F32)


def _layer_norm(z, g, b):
    mu = jnp.mean(z, axis=-1, keepdims=True)
    zc = z - mu
    var = jnp.mean(zc * zc, axis=-1, keepdims=True)
    return zc * lax.rsqrt(var + LN_EPS) * g + b


def _log_sigmoid(x):
    return jnp.minimum(x, 0.0) - jnp.log1p(jnp.exp(-jnp.abs(x)))


def _split3(x):
    hi = x.astype(BF16)
    r1 = x - hi.astype(F32)
    mid = r1.astype(BF16)
    lo = (r1 - mid.astype(F32)).astype(BF16)
    return hi, mid, lo


def _lane_is_even_head(shape):
    return lax.broadcasted_iota(jnp.int32, shape, len(shape) - 1) % LANES < D_HEAD


def _const_spec(shape):
    zeros = (0,) * len(shape)
    return pl.BlockSpec(shape, lambda *_: zeros)


def _ffn_kernel(*refs, alpha, ff_chunk, with_ple):
    if with_ple:
        (x_ref, wg_ref, wu_ref, wd_ref, g_ref, b_ref, p_ref, pwg_ref, pbg_ref, pwp_ref,
         o_ref, h_ref) = refs
    else:
        x_ref, wg_ref, wu_ref, wd_ref, g_ref, b_ref, o_ref, h_ref = refs
    x = x_ref[...]
    xb = x.astype(BF16)
    d_ff = wg_ref.shape[1]
    chunks = [slice(c * ff_chunk, (c + 1) * ff_chunk) for c in range(d_ff // ff_chunk)]
    for cols in chunks:
        gate = _dot(xb, wg_ref[:, cols].astype(BF16))
        up = _dot(xb, wu_ref[:, cols].astype(BF16))
        h_ref[:, cols] = (gate * jax.nn.sigmoid(gate) * up).astype(BF16)
    down = functools.reduce(jnp.add, [_dot(h_ref[:, cols], wd_ref[cols, :].astype(BF16)) for cols in chunks])
    y = _layer_norm(alpha * x + 0.5 * down, g_ref[...], b_ref[...])
    if with_ple:
        gate = jax.nn.sigmoid(_dot(y.astype(BF16), pwg_ref[...].astype(BF16)) + pbg_ref[...])
        y = y + gate * _dot(p_ref[...].astype(BF16), pwp_ref[...].astype(BF16))
    o_ref[...] = y


def _ffn(x, wg, wu, wd, g, b, *, layer, alpha, tm, ple=None):
    rows, d = x.shape
    d_ff = wg.shape[2]
    ff_chunk = 256
    assert rows % tm == 0 and d_ff % ff_chunk == 0
    row_spec = pl.BlockSpec((tm, d), lambda i: (i, 0))
    resident = lambda shape: pl.BlockSpec((None,) + shape, lambda i: (layer, 0, 0), pipeline_mode=pl.Buffered(1))
    in_specs = [row_spec, resident((d, d_ff)), resident((d, d_ff)), resident((d_ff, d)),
                _const_spec((1, d)), _const_spec((1, d))]
    args = [x, wg, wu, wd, g, b]
    if ple is not None:
        p, pwg, pbg, pwp = ple
        in_specs += [pl.BlockSpec((None, tm, p.shape[2]), lambda i: (layer, i, 0)), resident(pwg.shape[1:]),
                     _const_spec((1, d)), resident(pwp.shape[1:])]
        args += [p, pwg, pbg, pwp]
    return pl.pallas_call(
        functools.partial(_ffn_kernel, alpha=alpha, ff_chunk=ff_chunk, with_ple=ple is not None),
        out_shape=jax.ShapeDtypeStruct((rows, d), F32),
        grid=(rows // tm,),
        in_specs=in_specs,
        out_specs=row_spec,
        scratch_shapes=[pltpu.VMEM((tm, d_ff), BF16)],
        compiler_params=_params("parallel"),
        name="ffn_ple" if ple is not None else "ffn",
    )(*args)


N_SPLIT = 3
DECAY_LANE = D_HEAD


def _head_slots(x):
    low = lax.broadcasted_iota(jnp.int32, (x.shape[0], LANES), 1) < D_HEAD
    slots = []
    for g in range(x.shape[1] // LANES):
        pair = x[:, g * LANES:(g + 1) * LANES]
        slots.append(jnp.where(low, pair, 0.0))
        slots.append(jnp.where(low, pltpu.roll(pair, D_HEAD, axis=1), 0.0))
    return slots


def _even_proj_kernel(*refs, prompt):
    if prompt:
        (y_ref, wqkv_ref, wf_ref, bf_ref, wu_ref, place_q_ref, place_k_ref, ones_q_ref, ones_k_ref, ones_v_ref,
         qa_ref, ka_ref, va_ref, kt_ref, vt_ref, logft_ref, u_ref, carry_ref) = refs
    else:
        y_ref, wqkv_ref, wf_ref, bf_ref, wu_ref, q_ref, k_ref, v_ref, logft_ref, u_ref = refs
    yb = y_ref[...].astype(BF16)
    hw = wqkv_ref.shape[1] // 3
    heads = logft_ref.shape[0]
    q = _dot(yb, wqkv_ref[:, 0:hw]) * (D_HEAD ** -0.5 * (LOG2_E if prompt else 1.0))
    k = _dot(yb, wqkv_ref[:, hw:2 * hw])
    v = _dot(yb, wqkv_ref[:, 2 * hw:3 * hw])
    logf = _log_sigmoid(_dot(yb, wf_ref[...]) + bf_ref[...])
    logft_ref[...] = logf.T[:heads, :]
    u_ref[...] = _dot(yb, wu_ref[...])
    if not prompt:
        q_ref[...] = q.astype(BF16)
        k_ref[...] = k
        v_ref[...] = v
        return
    tm = logf.shape[0]
    kt_ref[...] = k.T
    vt_ref[...] = v.T
    @pl.when(pl.program_id(1) == 0)
    def _():
        carry_ref[...] = jnp.zeros_like(carry_ref)

    tri = jnp.where(lax.broadcasted_iota(jnp.int32, (tm, tm), 0) >= lax.broadcasted_iota(jnp.int32, (tm, tm), 1),
                    1.0, 0.0).astype(BF16)
    c = functools.reduce(jnp.add, [_dot(tri, part) for part in _split3(logf)]) + carry_ref[0:1, :]
    carry_ref[...] = jnp.broadcast_to(c[tm - 1:tm, :], carry_ref.shape)
    c_terms = jnp.concatenate(_split3(c * LOG2_E), axis=1)
    decay_q = _dot(c_terms, place_q_ref[...]) + ones_q_ref[...]
    decay_k = _dot(c_terms, place_k_ref[...]) + ones_k_ref[...]
    for h, (qs, ks, vs) in enumerate(zip(_head_slots(q), _head_slots(k), _head_slots(v))):
        lanes = slice(h * LANES, (h + 1) * LANES)
        qa_ref[:, lanes] = (qs + decay_q[:, lanes]).astype(BF16)
        ka_ref[:, lanes] = (ks + decay_k[:, lanes]).astype(BF16)
        va_ref[:, lanes] = (vs + ones_v_ref[:, lanes]).astype(BF16)


def _decay_placement(heads):
    place_q = np.zeros((N_SPLIT * LANES, heads * LANES), np.float32)
    place_k = np.zeros_like(place_q)
    ones_q = np.zeros((1, heads * LANES), np.float32)
    ones_k = np.zeros_like(ones_q)
    ones_v = np.zeros_like(ones_q)
    for h in range(heads):
        base = h * LANES + DECAY_LANE
        for j in range(N_SPLIT):
            place_q[j * LANES + h, base + j] = 1.0
            place_k[j * LANES + h, base + N_SPLIT + j] = -1.0
        ones_q[0, base + N_SPLIT:base + 2 * N_SPLIT] = 1.0
        ones_k[0, base:base + N_SPLIT] = 1.0
        ones_v[0, base:(h + 1) * LANES] = 1.0
    return (jnp.asarray(place_q, BF16), jnp.asarray(place_k, BF16), jnp.asarray(ones_q), jnp.asarray(ones_k),
            jnp.asarray(ones_v))


def _even_proj(y3, wqkv, wf, bf, wu, *, heads, tm, prompt):
    batch, seq_len, d = y3.shape
    hw = wqkv.shape[1] // 3
    pw = wu.shape[1]
    row = lambda w: pl.BlockSpec((None, tm, w), lambda b, i: (b, i, 0))
    col = lambda h: pl.BlockSpec((None, h, tm), lambda b, i: (b, 0, i))
    nat = lambda w, dt: jax.ShapeDtypeStruct((batch, seq_len, w), dt)
    args = [y3, wqkv, wf, bf, wu]
    tail_shape = (jax.ShapeDtypeStruct((batch, heads, seq_len), F32), nat(pw, F32))
    tail_specs = (col(heads), row(pw))
    if prompt:
        args += list(_decay_placement(heads))
        slot_w = heads * LANES
        kv_t = jax.ShapeDtypeStruct((batch, hw, seq_len), F32)
        out_shape = (nat(slot_w, BF16),) * 3 + (kv_t, kv_t) + tail_shape
        out_specs = (row(slot_w),) * 3 + (col(hw), col(hw)) + tail_specs
        scratch = [pltpu.VMEM((SUBLANES, LANES), F32)]
    else:
        out_shape = (nat(hw, BF16), nat(hw, F32), nat(hw, F32)) + tail_shape
        out_specs = (row(hw),) * 3 + tail_specs
        scratch = []
    return pl.pallas_call(
        functools.partial(_even_proj_kernel, prompt=prompt),
        out_shape=out_shape,
        grid=(batch, seq_len // tm),
        in_specs=[row(d)] + [_const_spec(a.shape) for a in args[1:]],
        out_specs=out_specs,
        scratch_shapes=scratch,
        compiler_params=_params("parallel", "arbitrary"),
        name="even_proj",
    )(*args)


SOFTMAX_ROWS = 64
ATTN_BUFFERS = 2


def _attn_step(masked, q_ref, k_ref, v_ref, m_ref, acc_ref, s_ref, p_ref, scale_ref, heads):
    tq, tk = q_ref.shape[0], k_ref.shape[0]
    for h in range(heads):
        lanes = slice(h * LANES, (h + 1) * LANES)
        buf = h % s_ref.shape[0]
        s_ref[buf] = _dot_nt(q_ref[:, lanes], k_ref[:, lanes])
        for r0 in range(0, tq, SOFTMAX_ROWS):
            rows = slice(r0, r0 + SOFTMAX_ROWS)
            s = s_ref[buf, rows, :]
            if masked:
                causal = (lax.broadcasted_iota(jnp.int32, s.shape, 1)
                          <= r0 + lax.broadcasted_iota(jnp.int32, s.shape, 0))
                s = jnp.where(causal, s, NEG_INF)
            m_prev = m_ref[h, rows, :]
            m_new = jnp.maximum(m_prev, jnp.max(s, axis=1, keepdims=True))
            p_ref[buf, rows, :] = jnp.exp2(s - jnp.tile(m_new, (1, tk // LANES))).astype(BF16)
            m_ref[h, rows, :] = m_new
            scale_ref[buf, rows, :] = jnp.exp2(m_prev - m_new)
        acc_ref[h] = scale_ref[buf] * acc_ref[h] + _dot(p_ref[buf], v_ref[:, lanes])


def _attn_finish(o_ref, acc_ref, heads):
    low = lax.broadcasted_iota(jnp.int32, (o_ref.shape[0], LANES), 1) < D_HEAD
    for g in range(heads // HEADS_PER_LANE_GROUP):
        normed = []
        for sub in range(HEADS_PER_LANE_GROUP):
            acc = acc_ref[g * HEADS_PER_LANE_GROUP + sub]
            normed.append(acc / pltpu.roll(acc, D_HEAD, axis=1))
        out = jnp.where(low, normed[0], pltpu.roll(normed[1], D_HEAD, axis=1))
        o_ref[:, g * LANES:(g + 1) * LANES] = out.astype(o_ref.dtype)


def _fox_attn_kernel(qi_ref, ki_ref, q_ref, k_ref, v_ref, o_ref, m_ref, acc_ref, s_ref, p_ref, scale_ref,
                     *, heads):
    pair = pl.program_id(1)
    qi = qi_ref[pair]
    ki = ki_ref[pair]
    host = (q_ref, k_ref, v_ref, m_ref, acc_ref, s_ref, p_ref, scale_ref, heads)

    @pl.when(ki == 0)
    def _():
        m_ref[...] = jnp.full_like(m_ref, NEG_INF)
        acc_ref[...] = jnp.zeros_like(acc_ref)

    @pl.when(ki < qi)
    def _():
        _attn_step(False, *host)

    @pl.when(ki == qi)
    def _():
        _attn_step(True, *host)
        _attn_finish(o_ref, acc_ref, heads)


def _fox_attention(q, k, v, *, heads, tq):
    batch, seq_len, slot_w = q.shape
    nq = seq_len // tq
    qi_tab = np.concatenate([np.full(i + 1, i) for i in range(nq)]).astype(np.int32)
    ki_tab = np.concatenate([np.arange(i + 1) for i in range(nq)]).astype(np.int32)
    q_map = lambda b, p, qi, ki: (b, qi[p], 0)
    k_map = lambda b, p, qi, ki: (b, ki[p], 0)
    return pl.pallas_call(
        functools.partial(_fox_attn_kernel, heads=heads),
        out_shape=jax.ShapeDtypeStruct((batch, seq_len, heads * D_HEAD), BF16),
        grid_spec=pltpu.PrefetchScalarGridSpec(
            num_scalar_prefetch=2,
            grid=(batch, len(qi_tab)),
            in_specs=[pl.BlockSpec((None, tq, slot_w), q_map), pl.BlockSpec((None, tq, slot_w), k_map),
                      pl.BlockSpec((None, tq, slot_w), k_map)],
            out_specs=pl.BlockSpec((None, tq, heads * D_HEAD), q_map),
            scratch_shapes=[pltpu.VMEM((heads, tq, LANES), F32), pltpu.VMEM((heads, tq, LANES), F32),
                            pltpu.VMEM((ATTN_BUFFERS, tq, tq), F32), pltpu.VMEM((ATTN_BUFFERS, tq, tq), BF16),
                            pltpu.VMEM((ATTN_BUFFERS, tq, LANES), F32)],
        ),
        compiler_params=_params("parallel", "arbitrary"),
        name="fox_attention",
    )(jnp.asarray(qi_tab), jnp.asarray(ki_tab), q, k, v)


def _pool_groups(ctx_ref, n_avail, w_pool_ref, scale_ref, rows, halo):
    gw = w_pool_ref.shape[1]
    outs = []
    for g, w in enumerate(POOL_WINDOWS):
        lanes = slice(g * gw, (g + 1) * gw)
        cur = ctx_ref[pl.ds(halo, rows), lanes]
        win = cur
        for j in range(1, w):
            win = win + ctx_ref[pl.ds(halo - j, rows), lanes]
        mean = win / jnp.minimum(float(w), n_avail)
        outs.append(_dot((mean - cur).astype(BF16), w_pool_ref[g]))
    return jnp.concatenate(outs, axis=-1) * scale_ref[...]


def _even_mix_kernel(att_ref, u_ref, halo_ref, y_ref, wpool_ref, pscale_ref, wo_ref, g_ref, b_ref,
                     o_ref, ctx_ref, *, alpha, blocks_per_seq):
    tm = u_ref.shape[0]
    fw = att_ref.shape[1]
    blk = pl.program_id(0) % blocks_per_seq
    ctx_ref[0:HALO_ROWS, :] = jnp.where(blk == 0, 0.0, halo_ref[...])
    ctx_ref[HALO_ROWS:HALO_ROWS + tm, :] = u_ref[...]
    pos = blk * tm + lax.broadcasted_iota(jnp.int32, (tm, 1), 0)
    n_avail = (pos + 1).astype(F32)
    pool = _pool_groups(ctx_ref, n_avail, wpool_ref, pscale_ref, tm, HALO_ROWS)
    mix = _dot(att_ref[...], wo_ref[0:fw, :]) + _dot(pool.astype(BF16), wo_ref[fw:, :])
    o_ref[...] = _layer_norm(alpha * y_ref[...] + mix, g_ref[...], b_ref[...])


def _even_mix(att, u, y, wpool, pscale, wo, g, b, *, alpha, seq_len, tm):
    rows, d = y.shape
    fw, pw = att.shape[1], u.shape[1]
    halo_blocks = tm // HALO_ROWS
    row = lambda w: pl.BlockSpec((tm, w), lambda i: (i, 0))
    return pl.pallas_call(
        functools.partial(_even_mix_kernel, alpha=alpha, blocks_per_seq=seq_len // tm),
        out_shape=jax.ShapeDtypeStruct((rows, d), F32),
        grid=(rows // tm,),
        in_specs=[row(fw), row(pw),
                  pl.BlockSpec((HALO_ROWS, pw), lambda i: (jnp.maximum(i * halo_blocks - 1, 0), 0)),
                  row(d), _const_spec(wpool.shape), _const_spec(pscale.shape), _const_spec(wo.shape),
                  _const_spec(g.shape), _const_spec(b.shape)],
        out_specs=row(d),
        scratch_shapes=[pltpu.VMEM((HALO_ROWS + tm, pw), F32)],
        compiler_params=_params("parallel"),
        name="even_mix",
    )(att, u, u, y, wpool, pscale, wo, g, b)


def _pool_sample_kernel(ctx_ref, wpool_ref, pscale_ref, o_ref, *, pos0, n_new):
    gw = wpool_ref.shape[1]
    for i in range(n_new):
        n_avail = float(pos0 + i + 1)
        outs = []
        for g, w in enumerate(POOL_WINDOWS):
            lanes = slice(g * gw, (g + 1) * gw)
            cur = ctx_ref[POOL_BUF + i, :, lanes]
            win = cur
            for j in range(1, w):
                win = win + ctx_ref[POOL_BUF + i - j, :, lanes]
            mean = win / min(float(w), n_avail)
            outs.append(_dot((mean - cur).astype(BF16), wpool_ref[g]))
        o_ref[i] = jnp.concatenate(outs, axis=-1) * pscale_ref[...]


def _pool_sample(ctx_tm, wpool, pscale, *, pos0, n_new):
    steps, nb, c = ctx_tm.shape
    return pl.pallas_call(
        functools.partial(_pool_sample_kernel, pos0=pos0, n_new=n_new),
        out_shape=jax.ShapeDtypeStruct((n_new, nb, c), F32),
        grid=(1,),
        in_specs=[_const_spec(ctx_tm.shape), _const_spec(wpool.shape), _const_spec(pscale.shape)],
        out_specs=_const_spec((n_new, nb, c)),
        compiler_params=_params("arbitrary"),
        name="pool_sample",
    )(ctx_tm, wpool, pscale)


def _pair_mix_kernel(a_ref, p_ref, y_ref, wo_ref, g_ref, b_ref, o_ref, *, alpha):
    fw = a_ref.shape[1]
    mix = _dot(a_ref[...], wo_ref[0:fw, :]) + _dot(p_ref[...].astype(BF16), wo_ref[fw:, :])
    o_ref[...] = _layer_norm(alpha * y_ref[...] + mix, g_ref[...], b_ref[...])


def _pair_mix(a, p, y, wo, g, b, *, alpha, tm):
    rows, d = y.shape
    row = lambda w: pl.BlockSpec((tm, w), lambda i: (i, 0))
    return pl.pallas_call(
        functools.partial(_pair_mix_kernel, alpha=alpha),
        out_shape=jax.ShapeDtypeStruct((rows, d), F32),
        grid=(rows // tm,),
        in_specs=[row(a.shape[1]), row(p.shape[1]), row(d), _const_spec(wo.shape),
                  _const_spec(g.shape), _const_spec(b.shape)],
        out_specs=row(d),
        compiler_params=_params("parallel"),
        name="pair_mix",
    )(a, p, y, wo, g, b)


def _sample_new_tokens(qbd_ref, knew_ref, vnew_ref, lnew_ref, m_ref, l_ref, acc_ref, carry_ref, heads):
    qbd = qbd_ref[...]
    lf = lnew_ref[...]
    lane = lax.broadcasted_iota(jnp.int32, lf.shape, 1)
    tok = lax.broadcasted_iota(jnp.int32, lf.shape, 0) // heads
    upto = jnp.where(lax.broadcasted_iota(jnp.int32, (LANES, LANES), 0)
                     <= lax.broadcasted_iota(jnp.int32, (LANES, LANES), 1), 1.0, 0.0).astype(BF16)
    hi, mid, lo = _split3(lf)
    pre = _dot(hi, upto) + _dot(mid, upto) + _dot(lo, upto)
    n_col = jnp.sum(jnp.where(lane == tok, pre, 0.0), axis=1, keepdims=True)
    kn = knew_ref[...].astype(BF16)
    s = _dot_nt(qbd, kn) + n_col - pre[:, 0:SUBLANES]
    key = lax.broadcasted_iota(jnp.int32, s.shape, 1)
    qtok = lax.broadcasted_iota(jnp.int32, s.shape, 0) // heads
    s = jnp.where(key <= qtok, s, NEG_INF)
    m = jnp.max(s, axis=1, keepdims=True)
    p = jnp.exp(s - m)
    m_ref[...] = jnp.broadcast_to(m, m_ref.shape)
    l_ref[...] = jnp.broadcast_to(jnp.sum(p, axis=1, keepdims=True), l_ref.shape)
    acc_ref[...] = _dot(p.astype(BF16), vnew_ref[...].astype(BF16))
    carry_ref[...] = jnp.broadcast_to(n_col, carry_ref.shape)


def _sample_pages(qbd_ref, k_refs, v_refs, lf_refs, m_ref, l_ref, acc_ref, carry_ref, heads):
    pages_per_step = len(k_refs)
    n_rows = qbd_ref.shape[0]
    qbd = qbd_ref[...]
    src = lax.broadcasted_iota(jnp.int32, (PAGE_SIZE, PAGE_SIZE), 0)
    dst = lax.broadcasted_iota(jnp.int32, (PAGE_SIZE, PAGE_SIZE), 1)
    after = jnp.where(src > dst, 1.0, 0.0).astype(BF16)
    reps = n_rows // heads
    lf_all = jnp.concatenate([r[...] for r in lf_refs], axis=0)
    hi, mid, lo = _split3(lf_all)
    within_all = _dot(hi, after) + _dot(mid, after) + _dot(lo, after)
    total_all = jnp.sum(lf_all, axis=1, keepdims=True)
    carry = carry_ref[...]
    s_parts = []
    for t in range(pages_per_step):
        page = slice(t * heads, (t + 1) * heads)
        bias = carry + jnp.tile(within_all[page], (reps, 1))
        s_parts.append(_dot(qbd, k_refs[t][...].astype(BF16)) + bias)
        carry = carry + jnp.tile(total_all[page], (reps, 1))
    carry_ref[...] = carry
    s = jnp.concatenate(s_parts, axis=1)
    m_prev = m_ref[...]
    m_new = jnp.maximum(m_prev, jnp.max(s, axis=1, keepdims=True))
    p = jnp.exp(s - jnp.tile(m_new, (1, pages_per_step)))
    scale = jnp.exp(m_prev - m_new)
    m_ref[...] = m_new
    l_ref[...] = scale * l_ref[...] + jnp.sum(p, axis=1, keepdims=True)
    pb = p.astype(BF16)
    pv = _dot_nt(pb[:, 0:PAGE_SIZE], v_refs[0][...].astype(BF16))
    for t in range(1, pages_per_step):
        pv = pv + _dot_nt(pb[:, t * PAGE_SIZE:(t + 1) * PAGE_SIZE], v_refs[t][...].astype(BF16))
    acc_ref[...] = jnp.tile(scale, (1, acc_ref.shape[1] // LANES)) * acc_ref[...] + pv


def _sample_finish(o_ref, l_ref, acc_ref, heads):
    full = acc_ref[...] / jnp.tile(l_ref[...], (1, acc_ref.shape[1] // LANES))
    row_head = lax.broadcasted_iota(jnp.int32, full.shape, 0) % heads
    lane_head = lax.broadcasted_iota(jnp.int32, full.shape, 1) // D_HEAD
    own = jnp.where(row_head == lane_head, full, 0.0)
    o_ref[...] = jnp.sum(own.reshape(full.shape[0] // heads, heads, full.shape[1]), axis=1).astype(o_ref.dtype)


def _fox_sample_kernel(*refs, pages_per_step, heads):
    qbd_ref, knew_ref, vnew_ref, lnew_ref = refs[1:5]
    k_refs = refs[5:5 + pages_per_step]
    v_refs = refs[5 + pages_per_step:5 + 2 * pages_per_step]
    lf_refs = refs[5 + 2 * pages_per_step:5 + 3 * pages_per_step]
    o_ref, m_ref, l_ref, acc_ref, carry_ref = refs[5 + 3 * pages_per_step:]
    state = (m_ref, l_ref, acc_ref, carry_ref, heads)

    @pl.when(pl.program_id(1) == 0)
    def _():
        _sample_new_tokens(qbd_ref, knew_ref, vnew_ref, lnew_ref, *state)

    _sample_pages(qbd_ref, k_refs, v_refs, lf_refs, *state)

    @pl.when(pl.program_id(1) == pl.num_programs(1) - 1)
    def _():
        _sample_finish(o_ref, l_ref, acc_ref, heads)


def _fox_sample(qbd, knew, vnew, lnew, cache_k, cache_v, cache_lft, page_table, *, heads, n_new,
                pages_per_step):
    nb, n_rows, width = qbd.shape
    n_pages = page_table.shape[1]
    assert n_pages % pages_per_step == 0
    pt = page_table.reshape(-1)

    def page_map(t):
        def index(b, s, pt_ref):
            logical = n_pages - 1 - (s * pages_per_step + t)
            return (pt_ref[b * n_pages + logical], 0, 0)
        return index

    batch3 = lambda shape: pl.BlockSpec((None,) + shape, lambda b, s, pt_ref: (b, 0, 0))
    in_specs = [batch3((n_rows, width)), batch3((SUBLANES, width)), batch3((SUBLANES, width)),
                batch3((n_rows, LANES))]
    in_specs += [pl.BlockSpec((None, width, PAGE_SIZE), page_map(t)) for t in range(pages_per_step)]
    in_specs += [pl.BlockSpec((None, width, PAGE_SIZE), page_map(t)) for t in range(pages_per_step)]
    in_specs += [pl.BlockSpec((None, heads, PAGE_SIZE), page_map(t)) for t in range(pages_per_step)]
    return pl.pallas_call(
        functools.partial(_fox_sample_kernel, pages_per_step=pages_per_step, heads=heads),
        out_shape=jax.ShapeDtypeStruct((nb, n_new, width), BF16),
        grid_spec=pltpu.PrefetchScalarGridSpec(
            num_scalar_prefetch=1,
            grid=(nb, n_pages // pages_per_step),
            in_specs=in_specs,
            out_specs=batch3((n_new, width)),
            scratch_shapes=[pltpu.VMEM((n_rows, LANES), F32), pltpu.VMEM((n_rows, LANES), F32),
                            pltpu.VMEM((n_rows, width), F32), pltpu.VMEM((n_rows, LANES), F32)],
        ),
        compiler_params=_params("parallel", "arbitrary"),
        name="fox_sample",
    )(pt, qbd, knew, vnew, lnew, *([cache_k] * pages_per_step), *([cache_v] * pages_per_step),
      *([cache_lft] * pages_per_step))


def _odd_proj_kernel(*refs, tm):
    n_groups = len(DIL_CONFIGS)
    y_ref, w_ref = refs[0], refs[1]
    de_refs = refs[2:2 + 3 * n_groups]
    tail_refs = refs[2 + 3 * n_groups:2 + 5 * n_groups]
    stage_refs = refs[2 + 5 * n_groups:]
    yb = y_ref[...].astype(BF16)
    n_lane_groups = de_refs[0].shape[1]
    gw = n_lane_groups * LANES
    stage = 0
    for g, (window, dil) in enumerate(DIL_CONFIGS):
        for part in range(3):
            col = (g * 3 + part) * gw
            res = _dot(yb, w_ref[:, col:col + gw])
            if part == 0:
                res = res * (D_HEAD ** -0.5 * LOG2_E)
            dst = de_refs[g * 3 + part]
            if dil == 1:
                for a in range(n_lane_groups):
                    dst[0, a] = res[:, a * LANES:(a + 1) * LANES].astype(BF16)
            else:
                s_ref = stage_refs[stage]
                stage += 1
                for a in range(n_lane_groups):
                    s_ref[a] = res[:, a * LANES:(a + 1) * LANES]
                for r in range(dil):
                    for a in range(n_lane_groups):
                        dst[r, a] = s_ref[a, pl.ds(r, tm // dil, stride=dil), :].astype(BF16)
            if part > 0:
                keep = min(window, tm)
                tail_refs[g * 2 + part - 1][...] = res[tm - keep:, :].T


def _odd_proj(y3, w, *, tm):
    batch, seq_len, d = y3.shape
    gw = w.shape[1] // (3 * len(DIL_CONFIGS))
    n_lane_groups = gw // LANES
    n_tiles = seq_len // tm
    out_shape, out_specs, n_stage = [], [], 0
    for window, dil in DIL_CONFIGS:
        assert tm % (dil * 16) == 0 and (window % tm == 0 or tm % window == 0)
        for _ in range(3):
            out_shape.append(jax.ShapeDtypeStruct((batch, dil, n_lane_groups, seq_len // dil, LANES), BF16))
            out_specs.append(pl.BlockSpec((None, dil, n_lane_groups, tm // dil, LANES),
                                          lambda b, i: (b, 0, 0, i, 0)))
        n_stage += 3 if dil > 1 else 0
    for window, dil in DIL_CONFIGS:
        keep = min(window, tm)
        first = n_tiles - window // keep
        for _ in range(2):
            out_shape.append(jax.ShapeDtypeStruct((batch, gw, window), F32))
            out_specs.append(pl.BlockSpec((None, gw, keep),
                                          lambda b, i, first=first: (b, 0, jnp.maximum(i - first, 0))))
    return pl.pallas_call(
        functools.partial(_odd_proj_kernel, tm=tm),
        out_shape=tuple(out_shape),
        grid=(batch, n_tiles),
        in_specs=[pl.BlockSpec((None, tm, d), lambda b, i: (b, i, 0)),
                  pl.BlockSpec(w.shape, lambda b, i: (0, 0), pipeline_mode=pl.Buffered(1))],
        out_specs=tuple(out_specs),
        scratch_shapes=[pltpu.VMEM((n_lane_groups, tm, LANES), F32)] * n_stage,
        compiler_params=_params("parallel", "arbitrary"),
        name="odd_proj",
    )(y3, w)


def _bmm_nt(a, b):
    return lax.dot_general(a, b, (((2,), (2,)), ((0,), (0,))), preferred_element_type=F32)


def _bmm(a, b):
    return lax.dot_general(a, b, (((2,), (1,)), ((0,), (0,))), preferred_element_type=F32)


def _dil_attn_kernel(q_ref, kh_ref, kc_ref, vh_ref, vc_ref, bfirst_ref, bprev_ref, o_ref, lse_ref,
                     *, dil, heads, qb):
    nblk = q_ref.shape[2] // qb
    units = [(r, j) for r in range(dil) for j in range(nblk)]
    even_lane = _lane_is_even_head((1, qb, LANES))

    def cur_blocks(ref, g):
        return jnp.concatenate([ref[r, g].reshape(nblk, qb, LANES) for r in range(dil)], axis=0)

    def prev_blocks(cur, halo_ref, g):
        parts = []
        for r in range(dil):
            parts.append(halo_ref[r, g][None])
            if nblk > 1:
                parts.append(cur[r * nblk:(r + 1) * nblk - 1])
        return jnp.concatenate(parts, axis=0)

    for g in range(heads // HEADS_PER_LANE_GROUP):
        q2, kc, vc = cur_blocks(q_ref, g), cur_blocks(kc_ref, g), cur_blocks(vc_ref, g)
        k2 = jnp.concatenate([prev_blocks(kc, kh_ref, g), kc], axis=1)
        v2 = jnp.concatenate([prev_blocks(vc, vh_ref, g), vc], axis=1)
        outs, lses = [], []
        for sub in range(HEADS_PER_LANE_GROUP):
            h = g * HEADS_PER_LANE_GROUP + sub
            qh = jnp.where(even_lane if sub == 0 else ~even_lane, q2, jnp.zeros_like(q2))
            bias = jnp.stack([bfirst_ref[h] if j == 0 else bprev_ref[h] for _, j in units])
            s = _bmm_nt(qh, k2) + bias
            m = jnp.max(s, axis=2, keepdims=True)
            p = jnp.exp2(s - m)
            den = jnp.sum(p, axis=2, keepdims=True)
            pv = _bmm(p.astype(BF16), v2)
            outs.append(pv / den)
            lses.append(jnp.broadcast_to(m * LN_2 + jnp.log(den), pv.shape))
        o2 = jnp.where(even_lane, outs[0], outs[1])
        l2 = jnp.where(even_lane, lses[0], lses[1])
        for u, (r, j) in enumerate(units):
            rows = pl.ds(j * qb, qb) if dil == 1 else pl.ds(r + dil * qb * j, qb, stride=dil)
            o_ref[g, rows, :] = o2[u]
            lse_ref[g, rows, :] = l2[u]


def _dil_attention(q, k, v, bias_first, bias, *, heads, qb, units):
    batch, dil, n_lane_groups, slots, _ = q.shape
    nblk = units // dil
    span = qb * nblk
    cur = pl.BlockSpec((None, dil, n_lane_groups, span, LANES), lambda b, i: (b, 0, 0, i, 0))
    halo = pl.BlockSpec((None, dil, n_lane_groups, qb, LANES),
                        lambda b, i: (b, 0, 0, jnp.maximum(i * nblk - 1, 0), 0))
    nat = pl.BlockSpec((None, n_lane_groups, span * dil, LANES), lambda b, i: (b, 0, i, 0))
    out_sds = jax.ShapeDtypeStruct((batch, n_lane_groups, slots * dil, LANES), F32)
    return pl.pallas_call(
        functools.partial(_dil_attn_kernel, dil=dil, heads=heads, qb=qb),
        out_shape=(out_sds, out_sds),
        grid=(batch, slots // span),
        in_specs=[cur, halo, cur, halo, cur,
                  pl.BlockSpec((None, heads, qb, 2 * qb), lambda b, i: (jnp.minimum(i, 1), 0, 0, 0)),
                  _const_spec(bias.shape)],
        out_specs=(nat, nat),
        compiler_params=_params("parallel", "arbitrary"),
        name=f"dil_attention_d{dil}",
    )(q, k, k, v, v, bias_first, bias)


def _odd_mix_kernel(*refs, alpha):
    n = len(DIL_CONFIGS)
    o_refs, lse_refs = refs[:n], refs[n:2 * n]
    y_ref, wo_ref, g_ref, b_ref, out_ref = refs[2 * n:]
    merged = []
    for a in range(o_refs[0].shape[0]):
        lses = [r[a] for r in lse_refs]
        top = functools.reduce(jnp.maximum, lses)
        wts = [jnp.exp(l - top) for l in lses]
        num = functools.reduce(jnp.add, [w * r[a] for w, r in zip(wts, o_refs)])
        merged.append((num / functools.reduce(jnp.add, wts)).astype(BF16))
    mix = _dot(jnp.concatenate(merged, axis=-1), wo_ref[...])
    out_ref[...] = _layer_norm(alpha * y_ref[...] + mix, g_ref[...], b_ref[...])


def _odd_mix(outs, lses, y3, wo, g, b, *, alpha, tm):
    batch, seq_len, d = y3.shape
    n_lane_groups = outs[0].shape[1]
    part = pl.BlockSpec((None, n_lane_groups, tm, LANES), lambda b, i: (b, 0, i, 0))
    row = pl.BlockSpec((None, tm, d), lambda b, i: (b, i, 0))
    return pl.pallas_call(
        functools.partial(_odd_mix_kernel, alpha=alpha),
        out_shape=jax.ShapeDtypeStruct((batch, seq_len, d), F32),
        grid=(batch, seq_len // tm),
        in_specs=[part] * (2 * len(outs)) + [row, _const_spec(wo.shape), _const_spec(g.shape),
                                             _const_spec(b.shape)],
        out_specs=row,
        compiler_params=_params("parallel", "parallel"),
        name="odd_mix",
    )(*outs, *lses, y3, wo, g, b)


def _proj_kernel(x_ref, w_ref, o_ref):
    o_ref[...] = _dot(x_ref[...].astype(BF16), w_ref[...])


def _proj(x, w, *, tn):
    rows, d = x.shape
    n = w.shape[1]
    return pl.pallas_call(
        _proj_kernel,
        out_shape=jax.ShapeDtypeStruct((rows, n), F32),
        grid=(n // tn,),
        in_specs=[_const_spec((rows, d)), pl.BlockSpec((d, tn), lambda j: (0, j))],
        out_specs=pl.BlockSpec((rows, tn), lambda j: (0, j)),
        compiler_params=_params("parallel"),
        name="proj",
    )(x, w)


def _dil_sample_kernel(qbd_ref, kc_ref, vc_ref, kn_ref, vn_ref, tabc_ref, tabn_ref,
                       o_ref, lse_ref, ko_ref, vo_ref, *, n_new, heads):
    width, buf_len = kc_ref.shape
    qbd = qbd_ref[...]
    chunk = min(buf_len, 512)
    starts = range(0, buf_len, chunk)
    sc = [_dot(qbd, kc_ref[:, c0:c0 + chunk].astype(BF16)) + tabc_ref[:, c0:c0 + chunk] for c0 in starts]
    sn = _dot_nt(qbd, kn_ref[...].astype(BF16)) + tabn_ref[...]
    m = functools.reduce(jnp.maximum, [jnp.max(s, axis=1, keepdims=True) for s in sc + [sn]])
    pn = jnp.exp(sn - m)
    den = jnp.sum(pn, axis=1, keepdims=True)
    pv = _dot(pn.astype(BF16), vn_ref[...].astype(BF16))
    for s, c0 in zip(sc, starts):
        p = jnp.exp(s - m)
        den = den + jnp.sum(p, axis=1, keepdims=True)
        pv = pv + _dot_nt(p.astype(BF16), vc_ref[:, c0:c0 + chunk].astype(BF16))
    full = pv / den
    lse = jnp.broadcast_to(m + jnp.log(den), full.shape)
    own = (lax.broadcasted_iota(jnp.int32, full.shape, 0) % heads
           == lax.broadcasted_iota(jnp.int32, full.shape, 1) // D_HEAD)
    o_ref[...] = jnp.sum(jnp.where(own, full, 0.0).reshape(n_new, heads, width), axis=1)
    lse_ref[...] = jnp.sum(jnp.where(own, lse, 0.0).reshape(n_new, heads, width), axis=1)
    rows = 64
    is_new = lax.broadcasted_iota(jnp.int32, (rows, LANES), 1) >= LANES - n_new
    zero_rows = jnp.zeros((LANES - SUBLANES, width), F32)
    for src_ref, new_ref, dst_ref in ((kc_ref, kn_ref, ko_ref), (vc_ref, vn_ref, vo_ref)):
        tail = jnp.concatenate([zero_rows, new_ref[...]], axis=0).T
        for r0 in range(0, width, rows):
            rolled = pltpu.roll(src_ref[r0:r0 + rows, :], buf_len - n_new, axis=1)
            if buf_len > LANES:
                dst_ref[r0:r0 + rows, 0:buf_len - LANES] = rolled[:, 0:buf_len - LANES]
            dst_ref[r0:r0 + rows, buf_len - LANES:] = jnp.where(is_new, tail[r0:r0 + rows, :],
                                                                rolled[:, buf_len - LANES:])


def _dil_sample(qbd, kc, vc, kn, vn, tab_c, tab_n, *, n_new, heads):
    nb, width, buf_len = kc.shape
    n_rows = qbd.shape[1]
    b3 = lambda *shape: pl.BlockSpec((None,) + shape, lambda b: (b,) + (0,) * len(shape))
    small = jax.ShapeDtypeStruct((nb, n_new, width), F32)
    big = jax.ShapeDtypeStruct((nb, width, buf_len), F32)
    return pl.pallas_call(
        functools.partial(_dil_sample_kernel, n_new=n_new, heads=heads),
        out_shape=(small, small, big, big),
        grid=(nb,),
        in_specs=[b3(n_rows, width), b3(width, buf_len), b3(width, buf_len),
                  b3(SUBLANES, width), b3(SUBLANES, width),
                  _const_spec(tab_c.shape), _const_spec(tab_n.shape)],
        out_specs=(b3(n_new, width), b3(n_new, width), b3(width, buf_len), b3(width, buf_len)),
        compiler_params=_params("parallel"),
        name=f"dil_sample_l{buf_len}",
    )(qbd, kc, vc, kn, vn, tab_c, tab_n)


def _rel_bucket(dist):
    exact = REL_BUCKETS // 2
    d = jnp.maximum(dist, 1).astype(F32)
    large = exact + (jnp.log(d / exact) / math.log(REL_MAX_DIST / exact) * (REL_BUCKETS - exact)).astype(jnp.int32)
    large = jnp.minimum(large, REL_BUCKETS - 1)
    return jnp.where(dist < exact, dist, large)


def _group_bias(rel_bias, g, window, dil, heads):
    dist = jnp.arange(window // dil + 1) * dil
    onehot = _rel_bucket(dist)[:, None] == jnp.arange(REL_BUCKETS)[None, :]
    cols = rel_bias[:, g * heads:(g + 1) * heads].astype(F32)
    return jnp.sum(jnp.where(onehot[:, :, None], cols[None], 0.0), axis=1).T


def _toeplitz(w, n):
    heads, period = w.shape
    flat = jnp.tile(w, (1, n))[:, :n * (period - 1)]
    return flat.reshape(heads, n, period - 1)[:, :, :n]


def _prompt_bias_tiles(bias, qb):
    heads, n_keys = bias.shape
    assert n_keys == qb + 1
    neg = jnp.full((heads, qb), NEG_INF, F32)
    cur = _toeplitz(jnp.concatenate([bias[:, 0:qb], neg], axis=1), qb)
    prev = _toeplitz(jnp.concatenate([bias[:, qb:qb + 1], neg, bias[:, 1:qb]], axis=1), qb)
    cur, prev = jnp.swapaxes(cur, 1, 2), jnp.swapaxes(prev, 1, 2)
    both = jnp.concatenate([prev, cur], axis=2)
    no_prev = jnp.concatenate([jnp.full_like(prev, NEG_INF), cur], axis=2)
    return jnp.stack([no_prev, both]), both


def _sample_bias_tables(bias, dil, n_new, buf_len):
    heads, n_keys = bias.shape
    n_back = n_keys - 1
    assert n_back * dil == buf_len
    oldest_first = bias[:, :0:-1]
    gaps = jnp.full((heads, n_back, dil - 1), NEG_INF, F32)
    spread = jnp.concatenate([oldest_first[:, :, None], gaps], axis=2).reshape(heads, buf_len)
    tab_c = [jnp.concatenate([jnp.full((heads, i), NEG_INF, F32), spread[:, :buf_len - i]], axis=1)
             for i in range(n_new)]
    pick = np.zeros((n_keys, n_new, SUBLANES), np.float32)
    for i in range(n_new):
        for j in range(i + 1):
            if (i - j) % dil == 0:
                pick[(i - j) // dil, i, SUBLANES - n_new + j] = 1.0
    picked = jnp.einsum("hk,kic->ihc", bias, jnp.asarray(pick), precision=lax.Precision.HIGHEST)
    tab_n = jnp.where(jnp.asarray(pick.sum(axis=0) > 0)[:, None, :], picked, NEG_INF)
    return jnp.concatenate(tab_c, axis=0), tab_n.reshape(n_new * heads, SUBLANES)


PROMPT_ROW_TILE = 512
ATTN_BLOCK = 512
DIL_SLOT_BLOCK = 128
DIL_UNITS_PER_STEP = 16
PAGES_PER_STEP = 32


def _pad_axis(x, axis, size):
    pad = [(0, 0)] * x.ndim
    pad[axis] = (0, size - x.shape[axis])
    return jnp.pad(x, pad)


def token_major(xt, lead):
    return jnp.transpose(xt.reshape(lead, -1, D_HEAD, xt.shape[-1]), (0, 3, 1, 2))


def kernel(x_prompt, x_sample, cache_fox_k, cache_fox_v, cache_fox_logf, state_pool, cache_dil0_k, cache_dil0_v, cache_dil1_k, cache_dil1_v, cache_dil2_k, cache_dil2_v, page_table, p_prompt, p_sample, w_in_even, b_fgate, pool_w, pool_scale, w_out_even, w_in_odd, w_out_odd, rel_bias, ffn1_wg, ffn1_wu, ffn1_wd, ffn2_wg, ffn2_wu, ffn2_wd, ln_g, ln_b, ple_wg, ple_bg, ple_wp):
    depth = ffn1_wg.shape[0]
    alpha = (2 * depth) ** 0.25
    batch, seq_len, d_model = x_prompt.shape
    nb, n_new, _ = x_sample.shape
    past_len = page_table.shape[1] * PAGE_SIZE
    fox_heads = cache_fox_k.shape[-2]
    fox_w = fox_heads * D_HEAD
    dil_heads = cache_dil0_k.shape[-2]
    dil_w = dil_heads * D_HEAD
    dil_caches_k = (cache_dil0_k, cache_dil1_k, cache_dil2_k)
    dil_caches_v = (cache_dil0_v, cache_dil1_v, cache_dil2_v)
    rows_p, rows_s = batch * seq_len, nb * n_new
    tm = PROMPT_ROW_TILE
    bf = lambda w: w.astype(BF16)
    vec = lambda a: a.reshape(1, -1)

    yp = x_prompt.reshape(rows_p, d_model)
    ys = x_sample.reshape(rows_s, d_model)
    outs = {name: [] for name in ("fkp", "fvp", "flp", "fks", "fvs", "fls", "plp", "pls")}
    dkp, dvp, dks, dvs = ([[] for _ in DIL_CONFIGS] for _ in range(4))

    for i in range(depth):
        w1 = (ffn1_wg, ffn1_wu, ffn1_wd, vec(ln_g[i, 0]), vec(ln_b[i, 0]))
        yp = _ffn(yp, *w1, layer=i, alpha=alpha, tm=tm)
        ys = _ffn(ys, *w1, layer=i, alpha=alpha, tm=rows_s)
        g1, b1 = vec(ln_g[i, 1]), vec(ln_b[i, 1])
        if i % 2 == 0:
            e = i // 2
            w_in = w_in_even[e]
            wqkv = bf(w_in[:, :3 * fox_w])
            wf = bf(_pad_axis(w_in[:, 3 * fox_w:3 * fox_w + fox_heads], 1, LANES))
            bfg = _pad_axis(vec(b_fgate[e]), 1, LANES)
            wu = bf(w_in[:, 3 * fox_w + fox_heads:])
            wpool, pscale, wo = bf(pool_w[e]), vec(pool_scale[e]), bf(w_out_even[e])
            qa, ka, va, kt, vt, lft, u = _even_proj(yp.reshape(batch, seq_len, d_model), wqkv, wf, bfg, wu,
                                                    heads=fox_heads, tm=tm, prompt=True)
            att = _fox_attention(qa, ka, va, heads=fox_heads, tq=ATTN_BLOCK).reshape(rows_p, fox_w)
            u = u.reshape(rows_p, -1)
            yp_next = _even_mix(att, u, yp, wpool, pscale, wo, g1, b1, alpha=alpha, seq_len=seq_len, tm=tm)
            outs["fkp"].append(token_major(kt, batch))
            outs["fvp"].append(token_major(vt, batch))
            outs["flp"].append(jnp.swapaxes(lft, 1, 2))
            outs["plp"].append(u.reshape(batch, seq_len, -1)[:, seq_len - POOL_BUF:])
            qs, ks, vs, lfts, us = _even_proj(ys[None], wqkv, wf, bfg, wu, heads=fox_heads, tm=rows_s,
                                              prompt=False)
            head_mask = (jnp.arange(fox_w)[None, :] // D_HEAD == jnp.arange(fox_heads)[:, None]).astype(BF16)
            qbd = (qs.reshape(nb, n_new, 1, fox_w) * head_mask).reshape(nb, n_new * fox_heads, fox_w)
            knew = _pad_axis(ks.reshape(nb, n_new, fox_w), 1, SUBLANES)
            vnew = _pad_axis(vs.reshape(nb, n_new, fox_w), 1, SUBLANES)
            lf3 = jnp.swapaxes(lfts.reshape(fox_heads, nb, n_new), 0, 1)
            lnew = jnp.broadcast_to(lf3[:, None], (nb, n_new, fox_heads, n_new)).reshape(nb, n_new * fox_heads, n_new)
            lnew = _pad_axis(lnew, 2, LANES)
            n_phys = cache_fox_k.shape[1]
            page_t = lambda c: jnp.transpose(c, (0, 2, 3, 1)).reshape(n_phys, fox_w, PAGE_SIZE)
            att_s = _fox_sample(qbd, knew, vnew, lnew, page_t(cache_fox_k[e]), page_t(cache_fox_v[e]),
                                jnp.swapaxes(cache_fox_logf[e], 1, 2), page_table,
                                heads=fox_heads, n_new=n_new, pages_per_step=PAGES_PER_STEP)
            ctx_s = jnp.concatenate([state_pool[e].astype(F32), us.reshape(nb, n_new, -1)], axis=1)
            pool_s = _pool_sample(jnp.swapaxes(ctx_s, 0, 1), wpool, pscale, pos0=past_len, n_new=n_new)
            pool_s = jnp.swapaxes(pool_s, 0, 1).reshape(rows_s, -1)
            ys_next = _pair_mix(att_s.reshape(rows_s, fox_w), pool_s, ys, wo, g1, b1, alpha=alpha, tm=rows_s)
            outs["fks"].append(ks.reshape(nb, n_new, fox_heads, D_HEAD))
            outs["fvs"].append(vs.reshape(nb, n_new, fox_heads, D_HEAD))
            outs["fls"].append(jnp.swapaxes(lf3, 1, 2))
            outs["pls"].append(ctx_s[:, -POOL_BUF:])
        else:
            o = i // 2
            w_in, wo = bf(w_in_odd[o]), bf(w_out_odd[o])
            biases = [_group_bias(rel_bias, g, window, dil, dil_heads) for g, (window, dil) in enumerate(DIL_CONFIGS)]
            proj = _odd_proj(yp.reshape(batch, seq_len, d_model), w_in, tm=tm)
            n_groups = len(DIL_CONFIGS)
            o_parts, lse_parts = [], []
            for g, (window, dil) in enumerate(DIL_CONFIGS):
                qd, kd, vd = proj[3 * g:3 * g + 3]
                bias_first, bias_both = _prompt_bias_tiles(biases[g] * LOG2_E, DIL_SLOT_BLOCK)
                og, lg = _dil_attention(qd, kd, vd, bias_first, bias_both, heads=dil_heads,
                                        qb=DIL_SLOT_BLOCK, units=DIL_UNITS_PER_STEP)
                o_parts.append(og)
                lse_parts.append(lg)
                dkp[g].append(token_major(proj[3 * n_groups + 2 * g], batch))
                dvp[g].append(token_major(proj[3 * n_groups + 2 * g + 1], batch))
            yp_next = _odd_mix(o_parts, lse_parts, yp.reshape(batch, seq_len, d_model), wo, g1, b1,
                               alpha=alpha, tm=tm).reshape(rows_p, d_model)
            lane_major = lambda a: jnp.swapaxes(a.reshape(rows_s, dil_w // LANES, LANES), 0, 1)[None]
            proj_s = _proj(ys, w_in, tn=3 * dil_w).reshape(nb, n_new, n_groups, 3, dil_w)
            head_mask = (jnp.arange(dil_w)[None, :] // D_HEAD == jnp.arange(dil_heads)[:, None]).astype(F32)
            o_parts, lse_parts = [], []
            for g, (window, dil) in enumerate(DIL_CONFIGS):
                qg = proj_s[:, :, g, 0] * (D_HEAD ** -0.5)
                qbd = (qg[:, :, None, :] * head_mask).astype(BF16).reshape(nb, n_new * dil_heads, dil_w)
                k_new, v_new = proj_s[:, :, g, 1], proj_s[:, :, g, 2]
                buf_len = dil_caches_k[g].shape[2]
                pos_minor = lambda c: jnp.transpose(c, (0, 2, 3, 1)).reshape(nb, dil_w, buf_len)
                rows_last = lambda x: jnp.pad(x, ((0, 0), (SUBLANES - n_new, 0), (0, 0)))
                tab_c, tab_n = _sample_bias_tables(biases[g], dil, n_new, buf_len)
                og, lg, k_roll, v_roll = _dil_sample(
                    qbd, pos_minor(dil_caches_k[g][o]), pos_minor(dil_caches_v[g][o]),
                    rows_last(k_new), rows_last(v_new), tab_c, tab_n, n_new=n_new, heads=dil_heads)
                o_parts.append(lane_major(og))
                lse_parts.append(lane_major(lg))
                dks[g].append(token_major(k_roll, nb))
                dvs[g].append(token_major(v_roll, nb))
            ys_next = _odd_mix(o_parts, lse_parts, ys[None], wo, g1, b1, alpha=alpha,
                               tm=rows_s).reshape(rows_s, d_model)
        yp, ys = yp_next, ys_next
        w2 = (ffn2_wg, ffn2_wu, ffn2_wd, vec(ln_g[i, 2]), vec(ln_b[i, 2]))
        ple_w = (ple_wg, vec(ple_bg[i]), ple_wp)
        yp = _ffn(yp, *w2, layer=i, alpha=alpha, tm=tm, ple=(p_prompt.reshape(depth, rows_p, -1),) + ple_w)
        ys = _ffn(ys, *w2, layer=i, alpha=alpha, tm=rows_s, ple=(p_sample.reshape(depth, rows_s, -1),) + ple_w)

    stack = lambda parts: jnp.stack(parts)
    result = [yp.reshape(batch, seq_len, d_model), ys.reshape(nb, n_new, d_model)]
    result += [stack(outs[name]) for name in ("fkp", "fvp", "flp", "fks", "fvs", "fls", "plp", "pls")]
    for g in range(len(DIL_CONFIGS)):
        result += [stack(dkp[g]), stack(dvp[g])]
    for g in range(len(DIL_CONFIGS)):
        result += [stack(dks[g]), stack(dvs[g])]
    return tuple(result)
```

```python
import functools
import math

import jax
import jax.numpy as jnp
import numpy as np
from jax import lax
from jax.experimental import pallas as pl
from jax.experimental.pallas import tpu as pltpu

F32 = jnp.float32
BF16 = jnp.bfloat16

D_HEAD = 64
POOL_WINDOWS = (2, 4, 8, 16)
POOL_BUF = max(POOL_WINDOWS) - 1
DIL_CONFIGS = ((128, 1), (512, 4), (2048, 16))
REL_BUCKETS = 32
REL_MAX_DIST = 2048
LN_EPS = 1e-5
NEG_INF = -1e30
LOG2_E = math.log2(math.e)
LN_2 = math.log(2.0)
PAGE_SIZE = 128

LANES = 128
SUBLANES = 8
VMEM_LIMIT_BYTES = 56 * 1024 * 1024

HEADS_PER_LANE_GROUP = LANES // D_HEAD
HALO_ROWS = 16


def _params(*semantics):
    return pltpu.CompilerParams(dimension_semantics=semantics, vmem_limit_bytes=VMEM_LIMIT_BYTES)


def _dot(a, b):
    return jnp.dot(a, b, preferred_element_type=F32)


def _dot_nt(a, b):
    return lax.dot_general(a, b, (((1,), (1,)), ((), ())), preferred_element_type=F32)


def _layer_norm(z, g, b):
    mu = jnp.mean(z, axis=-1, keepdims=True)
    zc = z - mu
    var = jnp.mean(zc * zc, axis=-1, keepdims=True)
    return zc * lax.rsqrt(var + LN_EPS) * g + b


def _log_sigmoid(x):
    return jnp.minimum(x, 0.0) - jnp.log1p(jnp.exp(-jnp.abs(x)))


def _split3(x):
    hi = x.astype(BF16)
    r1 = x - hi.astype(F32)
    mid = r1.astype(BF16)
    lo = (r1 - mid.astype(F32)).astype(BF16)
    return hi, mid, lo


def _lane_is_even_head(shape):
    return lax.broadcasted_iota(jnp.int32, shape, len(shape) - 1) % LANES < D_HEAD


def _const_spec(shape):
    zeros = (0,) * len(shape)
    return pl.BlockSpec(shape, lambda *_: zeros)


def _ffn_kernel(*refs, alpha, ff_chunk, with_ple):
    if with_ple:
        (x_ref, wg_ref, wu_ref, wd_ref, g_ref, b_ref, p_ref, pwg_ref, pbg_ref, pwp_ref,
         o_ref, h_ref) = refs
    else:
        x_ref, wg_ref, wu_ref, wd_ref, g_ref, b_ref, o_ref, h_ref = refs
    x = x_ref[...]
    xb = x.astype(BF16)
    d_ff = wg_ref.shape[1]
    chunks = [slice(c * ff_chunk, (c + 1) * ff_chunk) for c in range(d_ff // ff_chunk)]
    for cols in chunks:
        gate = _dot(xb, wg_ref[:, cols].astype(BF16))
        up = _dot(xb, wu_ref[:, cols].astype(BF16))
        h_ref[:, cols] = (gate * jax.nn.sigmoid(gate) * up).astype(BF16)
    down = functools.reduce(jnp.add, [_dot(h_ref[:, cols], wd_ref[cols, :].astype(BF16)) for cols in chunks])
    y = _layer_norm(alpha * x + 0.5 * down, g_ref[...], b_ref[...])
    if with_ple:
        gate = jax.nn.sigmoid(_dot(y.astype(BF16), pwg_ref[...].astype(BF16)) + pbg_ref[...])
        y = y + gate * _dot(p_ref[...].astype(BF16), pwp_ref[...].astype(BF16))
    o_ref[...] = y


def _ffn(x, wg, wu, wd, g, b, *, layer, alpha, tm, ple=None):
    rows, d = x.shape
    d_ff = wg.shape[2]
    ff_chunk = 256
    assert rows % tm == 0 and d_ff % ff_chunk == 0
    row_spec = pl.BlockSpec((tm, d), lambda i: (i, 0))
    resident = lambda shape: pl.BlockSpec((None,) + shape, lambda i: (layer, 0, 0), pipeline_mode=pl.Buffered(1))
    in_specs = [row_spec, resident((d, d_ff)), resident((d, d_ff)), resident((d_ff, d)),
                _const_spec((1, d)), _const_spec((1, d))]
    args = [x, wg, wu, wd, g, b]
    if ple is not None:
        p, pwg, pbg, pwp = ple
        in_specs += [pl.BlockSpec((None, tm, p.shape[2]), lambda i: (layer, i, 0)), resident(pwg.shape[1:]),
                     _const_spec((1, d)), resident(pwp.shape[1:])]
        args += [p, pwg, pbg, pwp]
    return pl.pallas_call(
        functools.partial(_ffn_kernel, alpha=alpha, ff_chunk=ff_chunk, with_ple=ple is not None),
        out_shape=jax.ShapeDtypeStruct((rows, d), F32),
        grid=(rows // tm,),
        in_specs=in_specs,
        out_specs=row_spec,
        scratch_shapes=[pltpu.VMEM((tm, d_ff), BF16)],
        compiler_params=_params("parallel"),
        name="ffn_ple" if ple is not None else "ffn",
    )(*args)


N_SPLIT = 3
DECAY_LANE = D_HEAD


def _head_slots(x):
    low = lax.broadcasted_iota(jnp.int32, (x.shape[0], LANES), 1) < D_HEAD
    slots = []
    for g in range(x.shape[1] // LANES):
        pair = x[:, g * LANES:(g + 1) * LANES]
        slots.append(jnp.where(low, pair, 0.0))
        slots.append(jnp.where(low, pltpu.roll(pair, D_HEAD, axis=1), 0.0))
    return slots


def _even_proj_kernel(*refs, prompt):
    if prompt:
        (y_ref, wqkv_ref, wf_ref, bf_ref, wu_ref, place_q_ref, place_k_ref, ones_q_ref, ones_k_ref, ones_v_ref,
         qa_ref, ka_ref, va_ref, kt_ref, vt_ref, logft_ref, u_ref, carry_ref) = refs
    else:
        y_ref, wqkv_ref, wf_ref, bf_ref, wu_ref, q_ref, k_ref, v_ref, logft_ref, u_ref = refs
    yb = y_ref[...].astype(BF16)
    hw = wqkv_ref.shape[1] // 3
    heads = logft_ref.shape[0]
    q = _dot(yb, wqkv_ref[:, 0:hw]) * (D_HEAD ** -0.5 * (LOG2_E if prompt else 1.0))
    k = _dot(yb, wqkv_ref[:, hw:2 * hw])
    v = _dot(yb, wqkv_ref[:, 2 * hw:3 * hw])
    logf = _log_sigmoid(_dot(yb, wf_ref[...]) + bf_ref[...])
    logft_ref[...] = logf.T[:heads, :]
    u_ref[...] = _dot(yb, wu_ref[...])
    if not prompt:
        q_ref[...] = q.astype(BF16)
        k_ref[...] = k
        v_ref[...] = v
        return
    tm = logf.shape[0]
    kt_ref[...] = k.T
    vt_ref[...] = v.T
    @pl.when(pl.program_id(1) == 0)
    def _():
        carry_ref[...] = jnp.zeros_like(carry_ref)

    tri = jnp.where(lax.broadcasted_iota(jnp.int32, (tm, tm), 0) >= lax.broadcasted_iota(jnp.int32, (tm, tm), 1),
                    1.0, 0.0).astype(BF16)
    c = functools.reduce(jnp.add, [_dot(tri, part) for part in _split3(logf)]) + carry_ref[0:1, :]
    carry_ref[...] = jnp.broadcast_to(c[tm - 1:tm, :], carry_ref.shape)
    c_terms = jnp.concatenate(_split3(c * LOG2_E), axis=1)
    decay_q = _dot(c_terms, place_q_ref[...]) + ones_q_ref[...]
    decay_k = _dot(c_terms, place_k_ref[...]) + ones_k_ref[...]
    for h, (qs, ks, vs) in enumerate(zip(_head_slots(q), _head_slots(k), _head_slots(v))):
        lanes = slice(h * LANES, (h + 1) * LANES)
        qa_ref[:, lanes] = (qs + decay_q[:, lanes]).astype(BF16)
        ka_ref[:, lanes] = (ks + decay_k[:, lanes]).astype(BF16)
        va_ref[:, lanes] = (vs + ones_v_ref[:, lanes]).astype(BF16)


def _decay_placement(heads):
    place_q = np.zeros((N_SPLIT * LANES, heads * LANES), np.float32)
    place_k = np.zeros_like(place_q)
    ones_q = np.zeros((1, heads * LANES), np.float32)
    ones_k = np.zeros_like(ones_q)
    ones_v = np.zeros_like(ones_q)
    for h in range(heads):
        base = h * LANES + DECAY_LANE
        for j in range(N_SPLIT):
            place_q[j * LANES + h, base + j] = 1.0
            place_k[j * LANES + h, base + N_SPLIT + j] = -1.0
        ones_q[0, base + N_SPLIT:base + 2 * N_SPLIT] = 1.0
        ones_k[0, base:base + N_SPLIT] = 1.0
        ones_v[0, base:(h + 1) * LANES] = 1.0
    return (jnp.asarray(place_q, BF16), jnp.asarray(place_k, BF16), jnp.asarray(ones_q), jnp.asarray(ones_k),
            jnp.asarray(ones_v))


def _even_proj(y3, wqkv, wf, bf, wu, *, heads, tm, prompt):
    batch, seq_len, d = y3.shape
    hw = wqkv.shape[1] // 3
    pw = wu.shape[1]
    row = lambda w: pl.BlockSpec((None, tm, w), lambda b, i: (b, i, 0))
    col = lambda h: pl.BlockSpec((None, h, tm), lambda b, i: (b, 0, i))
    nat = lambda w, dt: jax.ShapeDtypeStruct((batch, seq_len, w), dt)
    args = [y3, wqkv, wf, bf, wu]
    tail_shape = (jax.ShapeDtypeStruct((batch, heads, seq_len), F32), nat(pw, F32))
    tail_specs = (col(heads), row(pw))
    if prompt:
        args += list(_decay_placement(heads))
        slot_w = heads * LANES
        kv_t = jax.ShapeDtypeStruct((batch, hw, seq_len), F32)
        out_shape = (nat(slot_w, BF16),) * 3 + (kv_t, kv_t) + tail_shape
        out_specs = (row(slot_w),) * 3 + (col(hw), col(hw)) + tail_specs
        scratch = [pltpu.VMEM((SUBLANES, LANES), F32)]
    else:
        out_shape = (nat(hw, BF16), nat(hw, F32), nat(hw, F32)) + tail_shape
        out_specs = (row(hw),) * 3 + tail_specs
        scratch = []
    return pl.pallas_call(
        functools.partial(_even_proj_kernel, prompt=prompt),
        out_shape=out_shape,
        grid=(batch, seq_len // tm),
        in_specs=[row(d)] + [_const_spec(a.shape) for a in args[1:]],
        out_specs=out_specs,
        scratch_shapes=scratch,
        compiler_params=_params("parallel", "arbitrary"),
        name="even_proj",
    )(*args)


SOFTMAX_ROWS = 64
ATTN_BUFFERS = 2


def _attn_step(masked, q_ref, k_ref, v_ref, m_ref, acc_ref, s_ref, p_ref, scale_ref, heads):
    tq, tk = q_ref.shape[0], k_ref.shape[0]
    n_buf = s_ref.shape[0]

    def scores(h):
        lanes = slice(h * LANES, (h + 1) * LANES)
        s_ref[h % n_buf] = _dot_nt(q_ref[:, lanes], k_ref[:, lanes])

    scores(0)
    for h in range(heads):
        lanes = slice(h * LANES, (h + 1) * LANES)
        buf = h % n_buf
        if h + 1 < heads:
            scores(h + 1)
        for r0 in range(0, tq, SOFTMAX_ROWS):
            rows = slice(r0, r0 + SOFTMAX_ROWS)
            s = s_ref[buf, rows, :]
            if masked:
                causal = (lax.broadcasted_iota(jnp.int32, s.shape, 1)
                          <= r0 + lax.broadcasted_iota(jnp.int32, s.shape, 0))
                s = jnp.where(causal, s, NEG_INF)
            m_prev = m_ref[h, rows, :]
            m_new = jnp.maximum(m_prev, jnp.max(s, axis=1, keepdims=True))
            p_ref[buf, rows, :] = jnp.exp2(s - jnp.tile(m_new, (1, tk // LANES))).astype(BF16)
            m_ref[h, rows, :] = m_new
            scale_ref[buf, rows, :] = jnp.exp2(m_prev - m_new)
        acc_ref[h] = scale_ref[buf] * acc_ref[h] + _dot(p_ref[buf], v_ref[:, lanes])


def _attn_finish(o_ref, acc_ref, heads):
    low = lax.broadcasted_iota(jnp.int32, (o_ref.shape[0], LANES), 1) < D_HEAD
    for g in range(heads // HEADS_PER_LANE_GROUP):
        normed = []
        for sub in range(HEADS_PER_LANE_GROUP):
            acc = acc_ref[g * HEADS_PER_LANE_GROUP + sub]
            normed.append(acc / pltpu.roll(acc, D_HEAD, axis=1))
        out = jnp.where(low, normed[0], pltpu.roll(normed[1], D_HEAD, axis=1))
        o_ref[:, g * LANES:(g + 1) * LANES] = out.astype(o_ref.dtype)


def _fox_attn_kernel(qi_ref, ki_ref, q_ref, k_ref, v_ref, o_ref, m_ref, acc_ref, s_ref, p_ref, scale_ref,
                     *, heads):
    pair = pl.program_id(1)
    qi = qi_ref[pair]
    ki = ki_ref[pair]
    host = (q_ref, k_ref, v_ref, m_ref, acc_ref, s_ref, p_ref, scale_ref, heads)

    @pl.when(ki == 0)
    def _():
        m_ref[...] = jnp.full_like(m_ref, NEG_INF)
        acc_ref[...] = jnp.zeros_like(acc_ref)

    @pl.when(ki < qi)
    def _():
        _attn_step(False, *host)

    @pl.when(ki == qi)
    def _():
        _attn_step(True, *host)
        _attn_finish(o_ref, acc_ref, heads)


def _fox_attention(q, k, v, *, heads, tq):
    batch, seq_len, slot_w = q.shape
    nq = seq_len // tq
    qi_tab = np.concatenate([np.full(i + 1, i) for i in range(nq)]).astype(np.int32)
    ki_tab = np.concatenate([np.arange(i + 1) for i in range(nq)]).astype(np.int32)
    q_map = lambda b, p, qi, ki: (b, qi[p], 0)
    k_map = lambda b, p, qi, ki: (b, ki[p], 0)
    return pl.pallas_call(
        functools.partial(_fox_attn_kernel, heads=heads),
        out_shape=jax.ShapeDtypeStruct((batch, seq_len, heads * D_HEAD), BF16),
        grid_spec=pltpu.PrefetchScalarGridSpec(
            num_scalar_prefetch=2,
            grid=(batch, len(qi_tab)),
            in_specs=[pl.BlockSpec((None, tq, slot_w), q_map), pl.BlockSpec((None, tq, slot_w), k_map),
                      pl.BlockSpec((None, tq, slot_w), k_map)],
            out_specs=pl.BlockSpec((None, tq, heads * D_HEAD), q_map),
            scratch_shapes=[pltpu.VMEM((heads, tq, LANES), F32), pltpu.VMEM((heads, tq, LANES), F32),
                            pltpu.VMEM((ATTN_BUFFERS, tq, tq), F32), pltpu.VMEM((ATTN_BUFFERS, tq, tq), BF16),
                            pltpu.VMEM((ATTN_BUFFERS, tq, LANES), F32)],
        ),
        compiler_params=_params("parallel", "arbitrary"),
        name="fox_attention",
    )(jnp.asarray(qi_tab), jnp.asarray(ki_tab), q, k, v)


def _pool_groups(ctx_ref, n_avail, w_pool_ref, scale_ref, rows, halo):
    gw = w_pool_ref.shape[1]
    outs = []
    for g, w in enumerate(POOL_WINDOWS):
        lanes = slice(g * gw, (g + 1) * gw)
        cur = ctx_ref[pl.ds(halo, rows), lanes]
        win = cur
        for j in range(1, w):
            win = win + ctx_ref[pl.ds(halo - j, rows), lanes]
        mean = win / jnp.minimum(float(w), n_avail)
        outs.append(_dot((mean - cur).astype(BF16), w_pool_ref[g]))
    return jnp.concatenate(outs, axis=-1) * scale_ref[...]


def _even_mix_kernel(att_ref, u_ref, halo_ref, y_ref, wpool_ref, pscale_ref, wo_ref, g_ref, b_ref,
                     o_ref, ctx_ref, *, alpha, blocks_per_seq):
    tm = u_ref.shape[0]
    fw = att_ref.shape[1]
    blk = pl.program_id(0) % blocks_per_seq
    ctx_ref[0:HALO_ROWS, :] = jnp.where(blk == 0, 0.0, halo_ref[...])
    ctx_ref[HALO_ROWS:HALO_ROWS + tm, :] = u_ref[...]
    pos = blk * tm + lax.broadcasted_iota(jnp.int32, (tm, 1), 0)
    n_avail = (pos + 1).astype(F32)
    pool = _pool_groups(ctx_ref, n_avail, wpool_ref, pscale_ref, tm, HALO_ROWS)
    mix = _dot(att_ref[...], wo_ref[0:fw, :]) + _dot(pool.astype(BF16), wo_ref[fw:, :])
    o_ref[...] = _layer_norm(alpha * y_ref[...] + mix, g_ref[...], b_ref[...])


def _even_mix(att, u, y, wpool, pscale, wo, g, b, *, alpha, seq_len, tm):
    rows, d = y.shape
    fw, pw = att.shape[1], u.shape[1]
    halo_blocks = tm // HALO_ROWS
    row = lambda w: pl.BlockSpec((tm, w), lambda i: (i, 0))
    return pl.pallas_call(
        functools.partial(_even_mix_kernel, alpha=alpha, blocks_per_seq=seq_len // tm),
        out_shape=jax.ShapeDtypeStruct((rows, d), F32),
        grid=(rows // tm,),
        in_specs=[row(fw), row(pw),
                  pl.BlockSpec((HALO_ROWS, pw), lambda i: (jnp.maximum(i * halo_blocks - 1, 0), 0)),
                  row(d), _const_spec(wpool.shape), _const_spec(pscale.shape), _const_spec(wo.shape),
                  _const_spec(g.shape), _const_spec(b.shape)],
        out_specs=row(d),
        scratch_shapes=[pltpu.VMEM((HALO_ROWS + tm, pw), F32)],
        compiler_params=_params("parallel"),
        name="even_mix",
    )(att, u, u, y, wpool, pscale, wo, g, b)


def _pool_sample_kernel(ctx_ref, wpool_ref, pscale_ref, o_ref, *, pos0, n_new):
    gw = wpool_ref.shape[1]
    for i in range(n_new):
        n_avail = float(pos0 + i + 1)
        outs = []
        for g, w in enumerate(POOL_WINDOWS):
            lanes = slice(g * gw, (g + 1) * gw)
            cur = ctx_ref[POOL_BUF + i, :, lanes]
            win = cur
            for j in range(1, w):
                win = win + ctx_ref[POOL_BUF + i - j, :, lanes]
            mean = win / min(float(w), n_avail)
            outs.append(_dot((mean - cur).astype(BF16), wpool_ref[g]))
        o_ref[i] = jnp.concatenate(outs, axis=-1) * pscale_ref[...]


def _pool_sample(ctx_tm, wpool, pscale, *, pos0, n_new):
    steps, nb, c = ctx_tm.shape
    return pl.pallas_call(
        functools.partial(_pool_sample_kernel, pos0=pos0, n_new=n_new),
        out_shape=jax.ShapeDtypeStruct((n_new, nb, c), F32),
        grid=(1,),
        in_specs=[_const_spec(ctx_tm.shape), _const_spec(wpool.shape), _const_spec(pscale.shape)],
        out_specs=_const_spec((n_new, nb, c)),
        compiler_params=_params("arbitrary"),
        name="pool_sample",
    )(ctx_tm, wpool, pscale)


def _pair_mix_kernel(a_ref, p_ref, y_ref, wo_ref, g_ref, b_ref, o_ref, *, alpha):
    fw = a_ref.shape[1]
    mix = _dot(a_ref[...], wo_ref[0:fw, :]) + _dot(p_ref[...].astype(BF16), wo_ref[fw:, :])
    o_ref[...] = _layer_norm(alpha * y_ref[...] + mix, g_ref[...], b_ref[...])


def _pair_mix(a, p, y, wo, g, b, *, alpha, tm):
    rows, d = y.shape
    row = lambda w: pl.BlockSpec((tm, w), lambda i: (i, 0))
    return pl.pallas_call(
        functools.partial(_pair_mix_kernel, alpha=alpha),
        out_shape=jax.ShapeDtypeStruct((rows, d), F32),
        grid=(rows // tm,),
        in_specs=[row(a.shape[1]), row(p.shape[1]), row(d), _const_spec(wo.shape),
                  _const_spec(g.shape), _const_spec(b.shape)],
        out_specs=row(d),
        compiler_params=_params("parallel"),
        name="pair_mix",
    )(a, p, y, wo, g, b)


def _sample_new_tokens(qbd_ref, knew_ref, vnew_ref, lnew_ref, m_ref, l_ref, acc_ref, carry_ref, heads):
    qbd = qbd_ref[...]
    lf = lnew_ref[...]
    lane = lax.broadcasted_iota(jnp.int32, lf.shape, 1)
    tok = lax.broadcasted_iota(jnp.int32, lf.shape, 0) // heads
    upto = jnp.where(lax.broadcasted_iota(jnp.int32, (LANES, LANES), 0)
                     <= lax.broadcasted_iota(jnp.int32, (LANES, LANES), 1), 1.0, 0.0).astype(BF16)
    hi, mid, lo = _split3(lf)
    pre = _dot(hi, upto) + _dot(mid, upto) + _dot(lo, upto)
    n_col = jnp.sum(jnp.where(lane == tok, pre, 0.0), axis=1, keepdims=True)
    kn = knew_ref[...].astype(BF16)
    s = _dot_nt(qbd, kn) + n_col - pre[:, 0:SUBLANES]
    key = lax.broadcasted_iota(jnp.int32, s.shape, 1)
    qtok = lax.broadcasted_iota(jnp.int32, s.shape, 0) // heads
    s = jnp.where(key <= qtok, s, NEG_INF)
    m = jnp.max(s, axis=1, keepdims=True)
    p = jnp.exp(s - m)
    m_ref[...] = jnp.broadcast_to(m, m_ref.shape)
    l_ref[...] = jnp.broadcast_to(jnp.sum(p, axis=1, keepdims=True), l_ref.shape)
    acc_ref[...] = _dot(p.astype(BF16), vnew_ref[...].astype(BF16))
    carry_ref[...] = jnp.broadcast_to(n_col, carry_ref.shape)


def _sample_pages(qbd_ref, k_refs, v_refs, lf_refs, m_ref, l_ref, acc_ref, carry_ref, heads):
    pages_per_step = len(k_refs)
    n_rows = qbd_ref.shape[0]
    qbd = qbd_ref[...]
    src = lax.broadcasted_iota(jnp.int32, (PAGE_SIZE, PAGE_SIZE), 0)
    dst = lax.broadcasted_iota(jnp.int32, (PAGE_SIZE, PAGE_SIZE), 1)
    after = jnp.where(src > dst, 1.0, 0.0).astype(BF16)
    reps = n_rows // heads
    lf_all = jnp.concatenate([r[...] for r in lf_refs], axis=0)
    hi, mid, lo = _split3(lf_all)
    within_all = _dot(hi, after) + _dot(mid, after) + _dot(lo, after)
    total_all = jnp.sum(lf_all, axis=1, keepdims=True)
    carry = carry_ref[...]
    s_parts = []
    for t in range(pages_per_step):
        page = slice(t * heads, (t + 1) * heads)
        bias = carry + jnp.tile(within_all[page], (reps, 1))
        s_parts.append(_dot(qbd, k_refs[t][...].astype(BF16)) + bias)
        carry = carry + jnp.tile(total_all[page], (reps, 1))
    carry_ref[...] = carry
    s = jnp.concatenate(s_parts, axis=1)
    m_prev = m_ref[...]
    m_new = jnp.maximum(m_prev, jnp.max(s, axis=1, keepdims=True))
    p = jnp.exp(s - jnp.tile(m_new, (1, pages_per_step)))
    scale = jnp.exp(m_prev - m_new)
    m_ref[...] = m_new
    l_ref[...] = scale * l_ref[...] + jnp.sum(p, axis=1, keepdims=True)
    pb = p.astype(BF16)
    pv = _dot_nt(pb[:, 0:PAGE_SIZE], v_refs[0][...].astype(BF16))
    for t in range(1, pages_per_step):
        pv = pv + _dot_nt(pb[:, t * PAGE_SIZE:(t + 1) * PAGE_SIZE], v_refs[t][...].astype(BF16))
    acc_ref[...] = jnp.tile(scale, (1, acc_ref.shape[1] // LANES)) * acc_ref[...] + pv


def _sample_finish(o_ref, l_ref, acc_ref, heads):
    full = acc_ref[...] / jnp.tile(l_ref[...], (1, acc_ref.shape[1] // LANES))
    row_head = lax.broadcasted_iota(jnp.int32, full.shape, 0) % heads
    lane_head = lax.broadcasted_iota(jnp.int32, full.shape, 1) // D_HEAD
    own = jnp.where(row_head == lane_head, full, 0.0)
    o_ref[...] = jnp.sum(own.reshape(full.shape[0] // heads, heads, full.shape[1]), axis=1).astype(o_ref.dtype)


def _fox_sample_kernel(*refs, pages_per_step, heads):
    qbd_ref, knew_ref, vnew_ref, lnew_ref = refs[1:5]
    k_refs = refs[5:5 + pages_per_step]
    v_refs = refs[5 + pages_per_step:5 + 2 * pages_per_step]
    lf_refs = refs[5 + 2 * pages_per_step:5 + 3 * pages_per_step]
    o_ref, m_ref, l_ref, acc_ref, carry_ref = refs[5 + 3 * pages_per_step:]
    state = (m_ref, l_ref, acc_ref, carry_ref, heads)

    @pl.when(pl.program_id(1) == 0)
    def _():
        _sample_new_tokens(qbd_ref, knew_ref, vnew_ref, lnew_ref, *state)

    _sample_pages(qbd_ref, k_refs, v_refs, lf_refs, *state)

    @pl.when(pl.program_id(1) == pl.num_programs(1) - 1)
    def _():
        _sample_finish(o_ref, l_ref, acc_ref, heads)


def _fox_sample(qbd, knew, vnew, lnew, cache_k, cache_v, cache_lft, page_table, *, heads, n_new,
                pages_per_step):
    nb, n_rows, width = qbd.shape
    n_pages = page_table.shape[1]
    assert n_pages % pages_per_step == 0
    pt = page_table.reshape(-1)

    def page_map(t):
        def index(b, s, pt_ref):
            logical = n_pages - 1 - (s * pages_per_step + t)
            return (pt_ref[b * n_pages + logical], 0, 0)
        return index

    batch3 = lambda shape: pl.BlockSpec((None,) + shape, lambda b, s, pt_ref: (b, 0, 0))
    in_specs = [batch3((n_rows, width)), batch3((SUBLANES, width)), batch3((SUBLANES, width)),
                batch3((n_rows, LANES))]
    in_specs += [pl.BlockSpec((None, width, PAGE_SIZE), page_map(t)) for t in range(pages_per_step)]
    in_specs += [pl.BlockSpec((None, width, PAGE_SIZE), page_map(t)) for t in range(pages_per_step)]
    in_specs += [pl.BlockSpec((None, heads, PAGE_SIZE), page_map(t)) for t in range(pages_per_step)]
    return pl.pallas_call(
        functools.partial(_fox_sample_kernel, pages_per_step=pages_per_step, heads=heads),
        out_shape=jax.ShapeDtypeStruct((nb, n_new, width), BF16),
        grid_spec=pltpu.PrefetchScalarGridSpec(
            num_scalar_prefetch=1,
            grid=(nb, n_pages // pages_per_step),
            in_specs=in_specs,
            out_specs=batch3((n_new, width)),
            scratch_shapes=[pltpu.VMEM((n_rows, LANES), F32), pltpu.VMEM((n_rows, LANES), F32),
                            pltpu.VMEM((n_rows, width), F32), pltpu.VMEM((n_rows, LANES), F32)],
        ),
        compiler_params=_params("parallel", "arbitrary"),
        name="fox_sample",
    )(pt, qbd, knew, vnew, lnew, *([cache_k] * pages_per_step), *([cache_v] * pages_per_step),
      *([cache_lft] * pages_per_step))


def _odd_proj_kernel(*refs, tm):
    n_groups = len(DIL_CONFIGS)
    y_ref, w_ref = refs[0], refs[1]
    de_refs = refs[2:2 + 3 * n_groups]
    tail_refs = refs[2 + 3 * n_groups:2 + 5 * n_groups]
    stage_refs = refs[2 + 5 * n_groups:]
    yb = y_ref[...].astype(BF16)
    n_lane_groups = de_refs[0].shape[1]
    gw = n_lane_groups * LANES
    staged = []
    for g, (window, dil) in enumerate(DIL_CONFIGS):
        for part in range(3):
            col = (g * 3 + part) * gw
            res = _dot(yb, w_ref[:, col:col + gw])
            if part == 0:
                res = res * (D_HEAD ** -0.5 * LOG2_E)
            dst = de_refs[g * 3 + part]
            if dil == 1:
                for a in range(n_lane_groups):
                    dst[0, a] = res[:, a * LANES:(a + 1) * LANES].astype(BF16)
            else:
                s_ref = stage_refs[len(staged)]
                staged.append((s_ref, dst, dil))
                for a in range(n_lane_groups):
                    s_ref[a] = res[:, a * LANES:(a + 1) * LANES]
            if part > 0:
                keep = min(window, tm)
                tail_refs[g * 2 + part - 1][...] = res[tm - keep:, :].T
    for s_ref, dst, dil in staged:
        for r in range(dil):
            for a in range(n_lane_groups):
                dst[r, a] = s_ref[a, pl.ds(r, tm // dil, stride=dil), :].astype(BF16)


def _odd_proj(y3, w, *, tm):
    batch, seq_len, d = y3.shape
    gw = w.shape[1] // (3 * len(DIL_CONFIGS))
    n_lane_groups = gw // LANES
    n_tiles = seq_len // tm
    out_shape, out_specs, n_stage = [], [], 0
    for window, dil in DIL_CONFIGS:
        assert tm % (dil * 16) == 0 and (window % tm == 0 or tm % window == 0)
        for _ in range(3):
            out_shape.append(jax.ShapeDtypeStruct((batch, dil, n_lane_groups, seq_len // dil, LANES), BF16))
            out_specs.append(pl.BlockSpec((None, dil, n_lane_groups, tm // dil, LANES),
                                          lambda b, i: (b, 0, 0, i, 0)))
        n_stage += 3 if dil > 1 else 0
    for window, dil in DIL_CONFIGS:
        keep = min(window, tm)
        first = n_tiles - window // keep
        for _ in range(2):
            out_shape.append(jax.ShapeDtypeStruct((batch, gw, window), F32))
            out_specs.append(pl.BlockSpec((None, gw, keep),
                                          lambda b, i, first=first: (b, 0, jnp.maximum(i - first, 0))))
    return pl.pallas_call(
        functools.partial(_odd_proj_kernel, tm=tm),
        out_shape=tuple(out_shape),
        grid=(batch, n_tiles),
        in_specs=[pl.BlockSpec((None, tm, d), lambda b, i: (b, i, 0)),
                  pl.BlockSpec(w.shape, lambda b, i: (0, 0), pipeline_mode=pl.Buffered(1))],
        out_specs=tuple(out_specs),
        scratch_shapes=[pltpu.VMEM((n_lane_groups, tm, LANES), F32)] * n_stage,
        compiler_params=_params("parallel", "arbitrary"),
        name="odd_proj",
    )(y3, w)


def _bmm_nt(a, b):
    return lax.dot_general(a, b, (((2,), (2,)), ((0,), (0,))), preferred_element_type=F32)


def _bmm(a, b):
    return lax.dot_general(a, b, (((2,), (1,)), ((0,), (0,))), preferred_element_type=F32)


def _dil_attn_kernel(q_ref, kh_ref, kc_ref, vh_ref, vc_ref, bfirst_ref, bprev_ref, o_ref, lse_ref,
                     *, dil, heads, qb):
    nblk = q_ref.shape[2] // qb
    units = [(r, j) for r in range(dil) for j in range(nblk)]
    even_lane = _lane_is_even_head((1, qb, LANES))

    def cur_blocks(ref, g):
        return jnp.concatenate([ref[r, g].reshape(nblk, qb, LANES) for r in range(dil)], axis=0)

    def prev_blocks(cur, halo_ref, g):
        parts = []
        for r in range(dil):
            parts.append(halo_ref[r, g][None])
            if nblk > 1:
                parts.append(cur[r * nblk:(r + 1) * nblk - 1])
        return jnp.concatenate(parts, axis=0)

    for g in range(heads // HEADS_PER_LANE_GROUP):
        q2, kc, vc = cur_blocks(q_ref, g), cur_blocks(kc_ref, g), cur_blocks(vc_ref, g)
        k2 = jnp.concatenate([prev_blocks(kc, kh_ref, g), kc], axis=1)
        v2 = jnp.concatenate([prev_blocks(vc, vh_ref, g), vc], axis=1)
        outs, lses = [], []
        raw = [_bmm_nt(jnp.where(even_lane if sub == 0 else ~even_lane, q2, jnp.zeros_like(q2)), k2)
               for sub in range(HEADS_PER_LANE_GROUP)]
        for sub in range(HEADS_PER_LANE_GROUP):
            h = g * HEADS_PER_LANE_GROUP + sub
            bias = jnp.stack([bfirst_ref[h] if j == 0 else bprev_ref[h] for _, j in units])
            s = raw[sub] + bias
            m = jnp.max(s, axis=2, keepdims=True)
            p = jnp.exp2(s - m)
            den = jnp.sum(p, axis=2, keepdims=True)
            pv = _bmm(p.astype(BF16), v2)
            outs.append(pv / den)
            lses.append(jnp.broadcast_to(m * LN_2 + jnp.log(den), pv.shape))
        o2 = jnp.where(even_lane, outs[0], outs[1])
        l2 = jnp.where(even_lane, lses[0], lses[1])
        for u, (r, j) in enumerate(units):
            rows = pl.ds(j * qb, qb) if dil == 1 else pl.ds(r + dil * qb * j, qb, stride=dil)
            o_ref[g, rows, :] = o2[u]
            lse_ref[g, rows, :] = l2[u]


def _dil_attention(q, k, v, bias_first, bias, *, heads, qb, units):
    batch, dil, n_lane_groups, slots, _ = q.shape
    nblk = units // dil
    span = qb * nblk
    cur = pl.BlockSpec((None, dil, n_lane_groups, span, LANES), lambda b, i: (b, 0, 0, i, 0))
    halo = pl.BlockSpec((None, dil, n_lane_groups, qb, LANES),
                        lambda b, i: (b, 0, 0, jnp.maximum(i * nblk - 1, 0), 0))
    nat = pl.BlockSpec((None, n_lane_groups, span * dil, LANES), lambda b, i: (b, 0, i, 0))
    out_sds = jax.ShapeDtypeStruct((batch, n_lane_groups, slots * dil, LANES), F32)
    return pl.pallas_call(
        functools.partial(_dil_attn_kernel, dil=dil, heads=heads, qb=qb),
        out_shape=(out_sds, out_sds),
        grid=(batch, slots // span),
        in_specs=[cur, halo, cur, halo, cur,
                  pl.BlockSpec((None, heads, qb, 2 * qb), lambda b, i: (jnp.minimum(i, 1), 0, 0, 0)),
                  _const_spec(bias.shape)],
        out_specs=(nat, nat),
        compiler_params=_params("parallel", "arbitrary"),
        name=f"dil_attention_d{dil}",
    )(q, k, k, v, v, bias_first, bias)


def _odd_mix_kernel(*refs, alpha):
    n = len(DIL_CONFIGS)
    o_refs, lse_refs = refs[:n], refs[n:2 * n]
    y_ref, wo_ref, g_ref, b_ref, out_ref = refs[2 * n:]
    merged = []
    for a in range(o_refs[0].shape[0]):
        lses = [r[a] for r in lse_refs]
        top = functools.reduce(jnp.maximum, lses)
        wts = [jnp.exp(l - top) for l in lses]
        num = functools.reduce(jnp.add, [w * r[a] for w, r in zip(wts, o_refs)])
        merged.append((num / functools.reduce(jnp.add, wts)).astype(BF16))
    mix = _dot(jnp.concatenate(merged, axis=-1), wo_ref[...])
    out_ref[...] = _layer_norm(alpha * y_ref[...] + mix, g_ref[...], b_ref[...])


def _odd_mix(outs, lses, y3, wo, g, b, *, alpha, tm):
    batch, seq_len, d = y3.shape
    n_lane_groups = outs[0].shape[1]
    part = pl.BlockSpec((None, n_lane_groups, tm, LANES), lambda b, i: (b, 0, i, 0))
    row = pl.BlockSpec((None, tm, d), lambda b, i: (b, i, 0))
    return pl.pallas_call(
        functools.partial(_odd_mix_kernel, alpha=alpha),
        out_shape=jax.ShapeDtypeStruct((batch, seq_len, d), F32),
        grid=(batch, seq_len // tm),
        in_specs=[part] * (2 * len(outs)) + [row, _const_spec(wo.shape), _const_spec(g.shape),
                                             _const_spec(b.shape)],
        out_specs=row,
        compiler_params=_params("parallel", "parallel"),
        name="odd_mix",
    )(*outs, *lses, y3, wo, g, b)


def _proj_kernel(x_ref, w_ref, o_ref):
    o_ref[...] = _dot(x_ref[...].astype(BF16), w_ref[...])


def _proj(x, w, *, tn):
    rows, d = x.shape
    n = w.shape[1]
    return pl.pallas_call(
        _proj_kernel,
        out_shape=jax.ShapeDtypeStruct((rows, n), F32),
        grid=(n // tn,),
        in_specs=[_const_spec((rows, d)), pl.BlockSpec((d, tn), lambda j: (0, j))],
        out_specs=pl.BlockSpec((rows, tn), lambda j: (0, j)),
        compiler_params=_params("parallel"),
        name="proj",
    )(x, w)


def _dil_sample_kernel(qbd_ref, kc_ref, vc_ref, kn_ref, vn_ref, tabc_ref, tabn_ref,
                       o_ref, lse_ref, ko_ref, vo_ref, *, n_new, heads):
    width, buf_len = kc_ref.shape
    qbd = qbd_ref[...]
    chunk = min(buf_len, 512)
    starts = range(0, buf_len, chunk)
    sc = [_dot(qbd, kc_ref[:, c0:c0 + chunk].astype(BF16)) + tabc_ref[:, c0:c0 + chunk] for c0 in starts]
    sn = _dot_nt(qbd, kn_ref[...].astype(BF16)) + tabn_ref[...]
    m = functools.reduce(jnp.maximum, [jnp.max(s, axis=1, keepdims=True) for s in sc + [sn]])
    pn = jnp.exp(sn - m)
    den = jnp.sum(pn, axis=1, keepdims=True)
    pv = _dot(pn.astype(BF16), vn_ref[...].astype(BF16))
    for s, c0 in zip(sc, starts):
        p = jnp.exp(s - m)
        den = den + jnp.sum(p, axis=1, keepdims=True)
        pv = pv + _dot_nt(p.astype(BF16), vc_ref[:, c0:c0 + chunk].astype(BF16))
    full = pv / den
    lse = jnp.broadcast_to(m + jnp.log(den), full.shape)
    own = (lax.broadcasted_iota(jnp.int32, full.shape, 0) % heads
           == lax.broadcasted_iota(jnp.int32, full.shape, 1) // D_HEAD)
    o_ref[...] = jnp.sum(jnp.where(own, full, 0.0).reshape(n_new, heads, width), axis=1)
    lse_ref[...] = jnp.sum(jnp.where(own, lse, 0.0).reshape(n_new, heads, width), axis=1)
    rows = 64
    is_new = lax.broadcasted_iota(jnp.int32, (rows, LANES), 1) >= LANES - n_new
    zero_rows = jnp.zeros((LANES - SUBLANES, width), F32)
    for src_ref, new_ref, dst_ref in ((kc_ref, kn_ref, ko_ref), (vc_ref, vn_ref, vo_ref)):
        tail = jnp.concatenate([zero_rows, new_ref[...]], axis=0).T
        for r0 in range(0, width, rows):
            rolled = pltpu.roll(src_ref[r0:r0 + rows, :], buf_len - n_new, axis=1)
            if buf_len > LANES:
                dst_ref[r0:r0 + rows, 0:buf_len - LANES] = rolled[:, 0:buf_len - LANES]
            dst_ref[r0:r0 + rows, buf_len - LANES:] = jnp.where(is_new, tail[r0:r0 + rows, :],
                                                                rolled[:, buf_len - LANES:])


def _dil_sample(qbd, kc, vc, kn, vn, tab_c, tab_n, *, n_new, heads):
    nb, width, buf_len = kc.shape
    n_rows = qbd.shape[1]
    b3 = lambda *shape: pl.BlockSpec((None,) + shape, lambda b: (b,) + (0,) * len(shape))
    small = jax.ShapeDtypeStruct((nb, n_new, width), F32)
    big = jax.ShapeDtypeStruct((nb, width, buf_len), F32)
    return pl.pallas_call(
        functools.partial(_dil_sample_kernel, n_new=n_new, heads=heads),
        out_shape=(small, small, big, big),
        grid=(nb,),
        in_specs=[b3(n_rows, width), b3(width, buf_len), b3(width, buf_len),
                  b3(SUBLANES, width), b3(SUBLANES, width),
                  _const_spec(tab_c.shape), _const_spec(tab_n.shape)],
        out_specs=(b3(n_new, width), b3(n_new, width), b3(width, buf_len), b3(width, buf_len)),
        compiler_params=_params("parallel"),
        name=f"dil_sample_l{buf_len}",
    )(qbd, kc, vc, kn, vn, tab_c, tab_n)


def _rel_bucket(dist):
    exact = REL_BUCKETS // 2
    d = jnp.maximum(dist, 1).astype(F32)
    large = exact + (jnp.log(d / exact) / math.log(REL_MAX_DIST / exact) * (REL_BUCKETS - exact)).astype(jnp.int32)
    large = jnp.minimum(large, REL_BUCKETS - 1)
    return jnp.where(dist < exact, dist, large)


def _group_bias(rel_bias, g, window, dil, heads):
    dist = jnp.arange(window // dil + 1) * dil
    onehot = _rel_bucket(dist)[:, None] == jnp.arange(REL_BUCKETS)[None, :]
    cols = rel_bias[:, g * heads:(g + 1) * heads].astype(F32)
    return jnp.sum(jnp.where(onehot[:, :, None], cols[None], 0.0), axis=1).T


def _toeplitz(w, n):
    heads, period = w.shape
    flat = jnp.tile(w, (1, n))[:, :n * (period - 1)]
    return flat.reshape(heads, n, period - 1)[:, :, :n]


def _prompt_bias_tiles(bias, qb):
    heads, n_keys = bias.shape
    assert n_keys == qb + 1
    neg = jnp.full((heads, qb), NEG_INF, F32)
    cur = _toeplitz(jnp.concatenate([bias[:, 0:qb], neg], axis=1), qb)
    prev = _toeplitz(jnp.concatenate([bias[:, qb:qb + 1], neg, bias[:, 1:qb]], axis=1), qb)
    cur, prev = jnp.swapaxes(cur, 1, 2), jnp.swapaxes(prev, 1, 2)
    both = jnp.concatenate([prev, cur], axis=2)
    no_prev = jnp.concatenate([jnp.full_like(prev, NEG_INF), cur], axis=2)
    return jnp.stack([no_prev, both]), both


def _sample_bias_tables(bias, dil, n_new, buf_len):
    heads, n_keys = bias.shape
    n_back = n_keys - 1
    assert n_back * dil == buf_len
    oldest_first = bias[:, :0:-1]
    gaps = jnp.full((heads, n_back, dil - 1), NEG_INF, F32)
    spread = jnp.concatenate([oldest_first[:, :, None], gaps], axis=2).reshape(heads, buf_len)
    tab_c = [jnp.concatenate([jnp.full((heads, i), NEG_INF, F32), spread[:, :buf_len - i]], axis=1)
             for i in range(n_new)]
    pick = np.zeros((n_keys, n_new, SUBLANES), np.float32)
    for i in range(n_new):
        for j in range(i + 1):
            if (i - j) % dil == 0:
                pick[(i - j) // dil, i, SUBLANES - n_new + j] = 1.0
    picked = jnp.einsum("hk,kic->ihc", bias, jnp.asarray(pick), precision=lax.Precision.HIGHEST)
    tab_n = jnp.where(jnp.asarray(pick.sum(axis=0) > 0)[:, None, :], picked, NEG_INF)
    return jnp.concatenate(tab_c, axis=0), tab_n.reshape(n_new * heads, SUBLANES)


PROMPT_ROW_TILE = 512
ATTN_BLOCK = 512
DIL_SLOT_BLOCK = 128
DIL_UNITS_PER_STEP = 16
PAGES_PER_STEP = 32


def _pad_axis(x, axis, size):
    pad = [(0, 0)] * x.ndim
    pad[axis] = (0, size - x.shape[axis])
    return jnp.pad(x, pad)


def token_major(xt, lead):
    return jnp.transpose(xt.reshape(lead, -1, D_HEAD, xt.shape[-1]), (0, 3, 1, 2))


def kernel(x_prompt, x_sample, cache_fox_k, cache_fox_v, cache_fox_logf, state_pool, cache_dil0_k, cache_dil0_v, cache_dil1_k, cache_dil1_v, cache_dil2_k, cache_dil2_v, page_table, p_prompt, p_sample, w_in_even, b_fgate, pool_w, pool_scale, w_out_even, w_in_odd, w_out_odd, rel_bias, ffn1_wg, ffn1_wu, ffn1_wd, ffn2_wg, ffn2_wu, ffn2_wd, ln_g, ln_b, ple_wg, ple_bg, ple_wp):
    depth = ffn1_wg.shape[0]
    alpha = (2 * depth) ** 0.25
    batch, seq_len, d_model = x_prompt.shape
    nb, n_new, _ = x_sample.shape
    past_len = page_table.shape[1] * PAGE_SIZE
    fox_heads = cache_fox_k.shape[-2]
    fox_w = fox_heads * D_HEAD
    dil_heads = cache_dil0_k.shape[-2]
    dil_w = dil_heads * D_HEAD
    dil_caches_k = (cache_dil0_k, cache_dil1_k, cache_dil2_k)
    dil_caches_v = (cache_dil0_v, cache_dil1_v, cache_dil2_v)
    rows_p, rows_s = batch * seq_len, nb * n_new
    tm = PROMPT_ROW_TILE
    bf = lambda w: w.astype(BF16)
    vec = lambda a: a.reshape(1, -1)

    yp = x_prompt.reshape(rows_p, d_model)
    ys = x_sample.reshape(rows_s, d_model)
    outs = {name: [] for name in ("fkp", "fvp", "flp", "fks", "fvs", "fls", "plp", "pls")}
    dkp, dvp, dks, dvs = ([[] for _ in DIL_CONFIGS] for _ in range(4))

    for i in range(depth):
        w1 = (ffn1_wg, ffn1_wu, ffn1_wd, vec(ln_g[i, 0]), vec(ln_b[i, 0]))
        yp = _ffn(yp, *w1, layer=i, alpha=alpha, tm=tm)
        ys = _ffn(ys, *w1, layer=i, alpha=alpha, tm=rows_s)
        g1, b1 = vec(ln_g[i, 1]), vec(ln_b[i, 1])
        if i % 2 == 0:
            e = i // 2
            w_in = w_in_even[e]
            wqkv = bf(w_in[:, :3 * fox_w])
            wf = bf(_pad_axis(w_in[:, 3 * fox_w:3 * fox_w + fox_heads], 1, LANES))
            bfg = _pad_axis(vec(b_fgate[e]), 1, LANES)
            wu = bf(w_in[:, 3 * fox_w + fox_heads:])
            wpool, pscale, wo = bf(pool_w[e]), vec(pool_scale[e]), bf(w_out_even[e])
            qa, ka, va, kt, vt, lft, u = _even_proj(yp.reshape(batch, seq_len, d_model), wqkv, wf, bfg, wu,
                                                    heads=fox_heads, tm=tm, prompt=True)
            att = _fox_attention(qa, ka, va, heads=fox_heads, tq=ATTN_BLOCK).reshape(rows_p, fox_w)
            u = u.reshape(rows_p, -1)
            yp_next = _even_mix(att, u, yp, wpool, pscale, wo, g1, b1, alpha=alpha, seq_len=seq_len, tm=tm)
            outs["fkp"].append(token_major(kt, batch))
            outs["fvp"].append(token_major(vt, batch))
            outs["flp"].append(jnp.swapaxes(lft, 1, 2))
            outs["plp"].append(u.reshape(batch, seq_len, -1)[:, seq_len - POOL_BUF:])
            qs, ks, vs, lfts, us = _even_proj(ys[None], wqkv, wf, bfg, wu, heads=fox_heads, tm=rows_s,
                                              prompt=False)
            head_mask = (jnp.arange(fox_w)[None, :] // D_HEAD == jnp.arange(fox_heads)[:, None]).astype(BF16)
            qbd = (qs.reshape(nb, n_new, 1, fox_w) * head_mask).reshape(nb, n_new * fox_heads, fox_w)
            knew = _pad_axis(ks.reshape(nb, n_new, fox_w), 1, SUBLANES)
            vnew = _pad_axis(vs.reshape(nb, n_new, fox_w), 1, SUBLANES)
            lf3 = jnp.swapaxes(lfts.reshape(fox_heads, nb, n_new), 0, 1)
            lnew = jnp.broadcast_to(lf3[:, None], (nb, n_new, fox_heads, n_new)).reshape(nb, n_new * fox_heads, n_new)
            lnew = _pad_axis(lnew, 2, LANES)
            n_phys = cache_fox_k.shape[1]
            page_t = lambda c: jnp.transpose(c, (0, 2, 3, 1)).reshape(n_phys, fox_w, PAGE_SIZE)
            att_s = _fox_sample(qbd, knew, vnew, lnew, page_t(cache_fox_k[e]), page_t(cache_fox_v[e]),
                                jnp.swapaxes(cache_fox_logf[e], 1, 2), page_table,
                                heads=fox_heads, n_new=n_new, pages_per_step=PAGES_PER_STEP)
            ctx_s = jnp.concatenate([state_pool[e].astype(F32), us.reshape(nb, n_new, -1)], axis=1)
            pool_s = _pool_sample(jnp.swapaxes(ctx_s, 0, 1), wpool, pscale, pos0=past_len, n_new=n_new)
            pool_s = jnp.swapaxes(pool_s, 0, 1).reshape(rows_s, -1)
            ys_next = _pair_mix(att_s.reshape(rows_s, fox_w), pool_s, ys, wo, g1, b1, alpha=alpha, tm=rows_s)
            outs["fks"].append(ks.reshape(nb, n_new, fox_heads, D_HEAD))
            outs["fvs"].append(vs.reshape(nb, n_new, fox_heads, D_HEAD))
            outs["fls"].append(jnp.swapaxes(lf3, 1, 2))
            outs["pls"].append(ctx_s[:, -POOL_BUF:])
        else:
            o = i // 2
            w_in, wo = bf(w_in_odd[o]), bf(w_out_odd[o])
            biases = [_group_bias(rel_bias, g, window, dil, dil_heads) for g, (window, dil) in enumerate(DIL_CONFIGS)]
            proj = _odd_proj(yp.reshape(batch, seq_len, d_model), w_in, tm=tm)
            n_groups = len(DIL_CONFIGS)
            o_parts, lse_parts = [], []
            for g, (window, dil) in enumerate(DIL_CONFIGS):
                qd, kd, vd = proj[3 * g:3 * g + 3]
                bias_first, bias_both = _prompt_bias_tiles(biases[g] * LOG2_E, DIL_SLOT_BLOCK)
                og, lg = _dil_attention(qd, kd, vd, bias_first, bias_both, heads=dil_heads,
                                        qb=DIL_SLOT_BLOCK, units=DIL_UNITS_PER_STEP)
                o_parts.append(og)
                lse_parts.append(lg)
                dkp[g].append(token_major(proj[3 * n_groups + 2 * g], batch))
                dvp[g].append(token_major(proj[3 * n_groups + 2 * g + 1], batch))
            yp_next = _odd_mix(o_parts, lse_parts, yp.reshape(batch, seq_len, d_model), wo, g1, b1,
                               alpha=alpha, tm=tm).reshape(rows_p, d_model)
            lane_major = lambda a: jnp.swapaxes(a.reshape(rows_s, dil_w // LANES, LANES), 0, 1)[None]
            proj_s = _proj(ys, w_in, tn=3 * dil_w).reshape(nb, n_new, n_groups, 3, dil_w)
            head_mask = (jnp.arange(dil_w)[None, :] // D_HEAD == jnp.arange(dil_heads)[:, None]).astype(F32)
            o_parts, lse_parts = [], []
            for g, (window, dil) in enumerate(DIL_CONFIGS):
                qg = proj_s[:, :, g, 0] * (D_HEAD ** -0.5)
                qbd = (qg[:, :, None, :] * head_mask).astype(BF16).reshape(nb, n_new * dil_heads, dil_w)
                k_new, v_new = proj_s[:, :, g, 1], proj_s[:, :, g, 2]
                buf_len = dil_caches_k[g].shape[2]
                pos_minor = lambda c: jnp.transpose(c, (0, 2, 3, 1)).reshape(nb, dil_w, buf_len)
                rows_last = lambda x: jnp.pad(x, ((0, 0), (SUBLANES - n_new, 0), (0, 0)))
                tab_c, tab_n = _sample_bias_tables(biases[g], dil, n_new, buf_len)
                og, lg, k_roll, v_roll = _dil_sample(
                    qbd, pos_minor(dil_caches_k[g][o]), pos_minor(dil_caches_v[g][o]),
                    rows_last(k_new), rows_last(v_new), tab_c, tab_n, n_new=n_new, heads=dil_heads)
                o_parts.append(lane_major(og))
                lse_parts.append(lane_major(lg))
                dks[g].append(token_major(k_roll, nb))
                dvs[g].append(token_major(v_roll, nb))
            ys_next = _odd_mix(o_parts, lse_parts, ys[None], wo, g1, b1, alpha=alpha,
                               tm=rows_s).reshape(rows_s, d_model)
        yp, ys = yp_next, ys_next
        w2 = (ffn2_wg, ffn2_wu, ffn2_wd, vec(ln_g[i, 2]), vec(ln_b[i, 2]))
        ple_w = (ple_wg, vec(ple_bg[i]), ple_wp)
        yp = _ffn(yp, *w2, layer=i, alpha=alpha, tm=tm, ple=(p_prompt.reshape(depth, rows_p, -1),) + ple_w)
        ys = _ffn(ys, *w2, layer=i, alpha=alpha, tm=rows_s, ple=(p_sample.reshape(depth, rows_s, -1),) + ple_w)

    stack = lambda parts: jnp.stack(parts)
    result = [yp.reshape(batch, seq_len, d_model), ys.reshape(nb, n_new, d_model)]
    result += [stack(outs[name]) for name in ("fkp", "fvp", "flp", "fks", "fvs", "fls", "plp", "pls")]
    for g in range(len(DIL_CONFIGS)):
        result += [stack(dkp[g]), stack(dvp[g])]
    for g in range(len(DIL_CONFIGS)):
        result += [stack(dks[g]), stack(dvs[g])]
    return tuple(result)
```

```python
import functools
import math

import jax
import jax.numpy as jnp
import numpy as np
from jax import lax
from jax.experimental import pallas as pl
from jax.experimental.pallas import tpu as pltpu

F32 = jnp.float32
BF16 = jnp.bfloat16

D_HEAD = 64
POOL_WINDOWS = (2, 4, 8, 16)
POOL_BUF = max(POOL_WINDOWS) - 1
DIL_CONFIGS = ((128, 1), (512, 4), (2048, 16))
REL_BUCKETS = 32
REL_MAX_DIST = 2048
LN_EPS = 1e-5
NEG_INF = -1e30
LOG2_E = math.log2(math.e)
LN_2 = math.log(2.0)
PAGE_SIZE = 128

LANES = 128
SUBLANES = 8
VMEM_LIMIT_BYTES = 56 * 1024 * 1024

HEADS_PER_LANE_GROUP = LANES // D_HEAD
HALO_ROWS = 16


def _params(*semantics):
    return pltpu.CompilerParams(dimension_semantics=semantics, vmem_limit_bytes=VMEM_LIMIT_BYTES)


def _dot(a, b):
    return jnp.dot(a, b, preferred_element_type=F32)


def _dot_nt(a, b):
    return lax.dot_general(a, b, (((1,), (1,)), ((), ())), preferred_element_type=F32)


def _layer_norm(z, g, b):
    mu = jnp.mean(z, axis=-1, keepdims=True)
    zc = z - mu
    var = jnp.mean(zc * zc, axis=-1, keepdims=True)
    return zc * lax.rsqrt(var + LN_EPS) * g + b


def _log_sigmoid(x):
    return jnp.minimum(x, 0.0) - jnp.log1p(jnp.exp(-jnp.abs(x)))


def _split3(x):
    hi = x.astype(BF16)
    r1 = x - hi.astype(F32)
    mid = r1.astype(BF16)
    lo = (r1 - mid.astype(F32)).astype(BF16)
    return hi, mid, lo


def _lane_is_even_head(shape):
    return lax.broadcasted_iota(jnp.int32, shape, len(shape) - 1) % LANES < D_HEAD


def _const_spec(shape):
    zeros = (0,) * len(shape)
    return pl.BlockSpec(shape, lambda *_: zeros)


def _ffn_kernel(*refs, alpha, ff_chunk, with_ple):
    if with_ple:
        (x_ref, wg_ref, wu_ref, wd_ref, g_ref, b_ref, p_ref, pwg_ref, pbg_ref, pwp_ref,
         o_ref, h_ref) = refs
    else:
        x_ref, wg_ref, wu_ref, wd_ref, g_ref, b_ref, o_ref, h_ref = refs
    x = x_ref[...]
    xb = x.astype(BF16)
    d_ff = wg_ref.shape[1]
    chunks = [slice(c * ff_chunk, (c + 1) * ff_chunk) for c in range(d_ff // ff_chunk)]
    for cols in chunks:
        gate = _dot(xb, wg_ref[:, cols].astype(BF16))
        up = _dot(xb, wu_ref[:, cols].astype(BF16))
        h_ref[:, cols] = (gate * jax.nn.sigmoid(gate) * up).astype(BF16)
    down = functools.reduce(jnp.add, [_dot(h_ref[:, cols], wd_ref[cols, :].astype(BF16)) for cols in chunks])
    y = _layer_norm(alpha * x + 0.5 * down, g_ref[...], b_ref[...])
    if with_ple:
        gate = jax.nn.sigmoid(_dot(y.astype(BF16), pwg_ref[...].astype(BF16)) + pbg_ref[...])
        y = y + gate * _dot(p_ref[...].astype(BF16), pwp_ref[...].astype(BF16))
    o_ref[...] = y


def _ffn(x, wg, wu, wd, g, b, *, layer, alpha, tm, ple=None):
    rows, d = x.shape
    d_ff = wg.shape[2]
    ff_chunk = 256
    assert rows % tm == 0 and d_ff % ff_chunk == 0
    row_spec = pl.BlockSpec((tm, d), lambda i: (i, 0))
    resident = lambda shape: pl.BlockSpec((None,) + shape, lambda i: (layer, 0, 0), pipeline_mode=pl.Buffered(1))
    in_specs = [row_spec, resident((d, d_ff)), resident((d, d_ff)), resident((d_ff, d)),
                _const_spec((1, d)), _const_spec((1, d))]
    args = [x, wg, wu, wd, g, b]
    if ple is not None:
        p, pwg, pbg, pwp = ple
        in_specs += [pl.BlockSpec((None, tm, p.shape[2]), lambda i: (layer, i, 0)), resident(pwg.shape[1:]),
                     _const_spec((1, d)), resident(pwp.shape[1:])]
        args += [p, pwg, pbg, pwp]
    return pl.pallas_call(
        functools.partial(_ffn_kernel, alpha=alpha, ff_chunk=ff_chunk, with_ple=ple is not None),
        out_shape=jax.ShapeDtypeStruct((rows, d), F32),
        grid=(rows // tm,),
        in_specs=in_specs,
        out_specs=row_spec,
        scratch_shapes=[pltpu.VMEM((tm, d_ff), BF16)],
        compiler_params=_params("parallel"),
        name="ffn_ple" if ple is not None else "ffn",
    )(*args)


N_SPLIT = 3
DECAY_LANE = D_HEAD


def _head_slots(x):
    low = lax.broadcasted_iota(jnp.int32, (x.shape[0], LANES), 1) < D_HEAD
    slots = []
    for g in range(x.shape[1] // LANES):
        pair = x[:, g * LANES:(g + 1) * LANES]
        slots.append(jnp.where(low, pair, 0.0))
        slots.append(jnp.where(low, pltpu.roll(pair, D_HEAD, axis=1), 0.0))
    return slots


def _even_proj_kernel(*refs, prompt):
    if prompt:
        (y_ref, wqkv_ref, wf_ref, bf_ref, wu_ref, place_q_ref, place_k_ref, ones_q_ref, ones_k_ref, ones_v_ref,
         qa_ref, ka_ref, va_ref, kt_ref, vt_ref, logft_ref, u_ref, carry_ref) = refs
    else:
        y_ref, wqkv_ref, wf_ref, bf_ref, wu_ref, q_ref, k_ref, v_ref, logft_ref, u_ref = refs
    if prompt:
        @pl.when(pl.program_id(1) == 0)
        def _():
            carry_ref[...] = jnp.zeros_like(carry_ref)

    yb = y_ref[...].astype(BF16)
    hw = wqkv_ref.shape[1] // 3
    heads = logft_ref.shape[0]
    q = _dot(yb, wqkv_ref[:, 0:hw]) * (D_HEAD ** -0.5 * (LOG2_E if prompt else 1.0))
    k = _dot(yb, wqkv_ref[:, hw:2 * hw])
    v = _dot(yb, wqkv_ref[:, 2 * hw:3 * hw])
    logf = _log_sigmoid(_dot(yb, wf_ref[...]) + bf_ref[...])
    logft_ref[...] = logf.T[:heads, :]
    u_ref[...] = _dot(yb, wu_ref[...])
    if not prompt:
        q_ref[...] = q.astype(BF16)
        k_ref[...] = k
        v_ref[...] = v
        return
    tm = logf.shape[0]
    kt_ref[...] = k.T
    vt_ref[...] = v.T
    tri = jnp.where(lax.broadcasted_iota(jnp.int32, (tm, tm), 0) >= lax.broadcasted_iota(jnp.int32, (tm, tm), 1),
                    1.0, 0.0).astype(BF16)
    c = functools.reduce(jnp.add, [_dot(tri, part) for part in _split3(logf)]) + carry_ref[0:1, :]
    carry_ref[...] = jnp.broadcast_to(c[tm - 1:tm, :], carry_ref.shape)
    c_terms = jnp.concatenate(_split3(c * LOG2_E), axis=1)
    decay_q = _dot(c_terms, place_q_ref[...]) + ones_q_ref[...]
    decay_k = _dot(c_terms, place_k_ref[...]) + ones_k_ref[...]
    for h, (qs, ks, vs) in enumerate(zip(_head_slots(q), _head_slots(k), _head_slots(v))):
        lanes = slice(h * LANES, (h + 1) * LANES)
        qa_ref[:, lanes] = (qs + decay_q[:, lanes]).astype(BF16)
        ka_ref[:, lanes] = (ks + decay_k[:, lanes]).astype(BF16)
        va_ref[:, lanes] = (vs + ones_v_ref[:, lanes]).astype(BF16)


def _decay_placement(heads):
    place_q = np.zeros((N_SPLIT * LANES, heads * LANES), np.float32)
    place_k = np.zeros_like(place_q)
    ones_q = np.zeros((1, heads * LANES), np.float32)
    ones_k = np.zeros_like(ones_q)
    ones_v = np.zeros_like(ones_q)
    for h in range(heads):
        base = h * LANES + DECAY_LANE
        for j in range(N_SPLIT):
            place_q[j * LANES + h, base + j] = 1.0
            place_k[j * LANES + h, base + N_SPLIT + j] = -1.0
        ones_q[0, base + N_SPLIT:base + 2 * N_SPLIT] = 1.0
        ones_k[0, base:base + N_SPLIT] = 1.0
        ones_v[0, base:(h + 1) * LANES] = 1.0
    return (jnp.asarray(place_q, BF16), jnp.asarray(place_k, BF16), jnp.asarray(ones_q), jnp.asarray(ones_k),
            jnp.asarray(ones_v))


def _even_proj(y3, wqkv, wf, bf, wu, *, heads, tm, prompt):
    batch, seq_len, d = y3.shape
    hw = wqkv.shape[1] // 3
    pw = wu.shape[1]
    row = lambda w: pl.BlockSpec((None, tm, w), lambda b, i: (b, i, 0))
    col = lambda h: pl.BlockSpec((None, h, tm), lambda b, i: (b, 0, i))
    nat = lambda w, dt: jax.ShapeDtypeStruct((batch, seq_len, w), dt)
    args = [y3, wqkv, wf, bf, wu]
    tail_shape = (jax.ShapeDtypeStruct((batch, heads, seq_len), F32), nat(pw, F32))
    tail_specs = (col(heads), row(pw))
    if prompt:
        args += list(_decay_placement(heads))
        slot_w = heads * LANES
        kv_t = jax.ShapeDtypeStruct((batch, hw, seq_len), F32)
        out_shape = (nat(slot_w, BF16),) * 3 + (kv_t, kv_t) + tail_shape
        out_specs = (row(slot_w),) * 3 + (col(hw), col(hw)) + tail_specs
        scratch = [pltpu.VMEM((SUBLANES, LANES), F32)]
    else:
        out_shape = (nat(hw, BF16), nat(hw, F32), nat(hw, F32)) + tail_shape
        out_specs = (row(hw),) * 3 + tail_specs
        scratch = []
    return pl.pallas_call(
        functools.partial(_even_proj_kernel, prompt=prompt),
        out_shape=out_shape,
        grid=(batch, seq_len // tm),
        in_specs=[row(d)] + [_const_spec(a.shape) for a in args[1:]],
        out_specs=out_specs,
        scratch_shapes=scratch,
        compiler_params=_params("parallel", "arbitrary"),
        name="even_proj",
    )(*args)


SOFTMAX_ROWS = 64
ATTN_BUFFERS = 2


def _attn_step(masked, q_ref, k_ref, v_ref, m_ref, acc_ref, s_ref, p_ref, scale_ref, heads):
    tq, tk = q_ref.shape[0], k_ref.shape[0]
    n_buf = s_ref.shape[0]

    def scores(h):
        lanes = slice(h * LANES, (h + 1) * LANES)
        s_ref[h % n_buf] = _dot_nt(q_ref[:, lanes], k_ref[:, lanes])

    scores(0)
    for h in range(heads):
        lanes = slice(h * LANES, (h + 1) * LANES)
        buf = h % n_buf
        if h + 1 < heads:
            scores(h + 1)
        for r0 in range(0, tq, SOFTMAX_ROWS):
            rows = slice(r0, r0 + SOFTMAX_ROWS)
            s = s_ref[buf, rows, :]
            if masked:
                causal = (lax.broadcasted_iota(jnp.int32, s.shape, 1)
                          <= r0 + lax.broadcasted_iota(jnp.int32, s.shape, 0))
                s = jnp.where(causal, s, NEG_INF)
            m_prev = m_ref[h, rows, :]
            m_new = jnp.maximum(m_prev, jnp.max(s, axis=1, keepdims=True))
            p_ref[buf, rows, :] = jnp.exp2(s - jnp.tile(m_new, (1, tk // LANES))).astype(BF16)
            m_ref[h, rows, :] = m_new
            scale_ref[buf, rows, :] = jnp.exp2(m_prev - m_new)
        acc_ref[h] = scale_ref[buf] * acc_ref[h] + _dot(p_ref[buf], v_ref[:, lanes])


def _attn_finish(o_ref, acc_ref, heads):
    low = lax.broadcasted_iota(jnp.int32, (o_ref.shape[0], LANES), 1) < D_HEAD
    for g in range(heads // HEADS_PER_LANE_GROUP):
        normed = []
        for sub in range(HEADS_PER_LANE_GROUP):
            acc = acc_ref[g * HEADS_PER_LANE_GROUP + sub]
            normed.append(acc / pltpu.roll(acc, D_HEAD, axis=1))
        out = jnp.where(low, normed[0], pltpu.roll(normed[1], D_HEAD, axis=1))
        o_ref[:, g * LANES:(g + 1) * LANES] = out.astype(o_ref.dtype)


def _fox_attn_kernel(qi_ref, ki_ref, q_ref, k_ref, v_ref, o_ref, m_ref, acc_ref, s_ref, p_ref, scale_ref,
                     *, heads):
    pair = pl.program_id(1)
    qi = qi_ref[pair]
    ki = ki_ref[pair]
    host = (q_ref, k_ref, v_ref, m_ref, acc_ref, s_ref, p_ref, scale_ref, heads)

    @pl.when(ki == 0)
    def _():
        m_ref[...] = jnp.full_like(m_ref, NEG_INF)
        acc_ref[...] = jnp.zeros_like(acc_ref)

    @pl.when(ki < qi)
    def _():
        _attn_step(False, *host)

    @pl.when(ki == qi)
    def _():
        _attn_step(True, *host)
        _attn_finish(o_ref, acc_ref, heads)


def _fox_attention(q, k, v, *, heads, tq):
    batch, seq_len, slot_w = q.shape
    nq = seq_len // tq
    qi_tab = np.concatenate([np.full(i + 1, i) for i in range(nq)]).astype(np.int32)
    ki_tab = np.concatenate([np.arange(i + 1) for i in range(nq)]).astype(np.int32)
    q_map = lambda b, p, qi, ki: (b, qi[p], 0)
    k_map = lambda b, p, qi, ki: (b, ki[p], 0)
    return pl.pallas_call(
        functools.partial(_fox_attn_kernel, heads=heads),
        out_shape=jax.ShapeDtypeStruct((batch, seq_len, heads * D_HEAD), BF16),
        grid_spec=pltpu.PrefetchScalarGridSpec(
            num_scalar_prefetch=2,
            grid=(batch, len(qi_tab)),
            in_specs=[pl.BlockSpec((None, tq, slot_w), q_map), pl.BlockSpec((None, tq, slot_w), k_map),
                      pl.BlockSpec((None, tq, slot_w), k_map)],
            out_specs=pl.BlockSpec((None, tq, heads * D_HEAD), q_map),
            scratch_shapes=[pltpu.VMEM((heads, tq, LANES), F32), pltpu.VMEM((heads, tq, LANES), F32),
                            pltpu.VMEM((ATTN_BUFFERS, tq, tq), F32), pltpu.VMEM((ATTN_BUFFERS, tq, tq), BF16),
                            pltpu.VMEM((ATTN_BUFFERS, tq, LANES), F32)],
        ),
        compiler_params=_params("parallel", "arbitrary"),
        name="fox_attention",
    )(jnp.asarray(qi_tab), jnp.asarray(ki_tab), q, k, v)


def _pool_groups(ctx_ref, n_avail, w_pool_ref, scale_ref, rows, halo):
    gw = w_pool_ref.shape[1]
    outs = []
    for g, w in enumerate(POOL_WINDOWS):
        lanes = slice(g * gw, (g + 1) * gw)
        cur = ctx_ref[pl.ds(halo, rows), lanes]
        win = cur
        for j in range(1, w):
            win = win + ctx_ref[pl.ds(halo - j, rows), lanes]
        mean = win / jnp.minimum(float(w), n_avail)
        outs.append(_dot((mean - cur).astype(BF16), w_pool_ref[g]))
    return jnp.concatenate(outs, axis=-1) * scale_ref[...]


def _even_mix_kernel(att_ref, u_ref, halo_ref, y_ref, wpool_ref, pscale_ref, wo_ref, g_ref, b_ref,
                     o_ref, ctx_ref, *, alpha, blocks_per_seq):
    tm = u_ref.shape[0]
    fw = att_ref.shape[1]
    blk = pl.program_id(0) % blocks_per_seq
    ctx_ref[0:HALO_ROWS, :] = jnp.where(blk == 0, 0.0, halo_ref[...])
    ctx_ref[HALO_ROWS:HALO_ROWS + tm, :] = u_ref[...]
    pos = blk * tm + lax.broadcasted_iota(jnp.int32, (tm, 1), 0)
    n_avail = (pos + 1).astype(F32)
    pool = _pool_groups(ctx_ref, n_avail, wpool_ref, pscale_ref, tm, HALO_ROWS)
    mix = _dot(att_ref[...], wo_ref[0:fw, :]) + _dot(pool.astype(BF16), wo_ref[fw:, :])
    o_ref[...] = _layer_norm(alpha * y_ref[...] + mix, g_ref[...], b_ref[...])


def _even_mix(att, u, y, wpool, pscale, wo, g, b, *, alpha, seq_len, tm):
    rows, d = y.shape
    fw, pw = att.shape[1], u.shape[1]
    halo_blocks = tm // HALO_ROWS
    row = lambda w: pl.BlockSpec((tm, w), lambda i: (i, 0))
    return pl.pallas_call(
        functools.partial(_even_mix_kernel, alpha=alpha, blocks_per_seq=seq_len // tm),
        out_shape=jax.ShapeDtypeStruct((rows, d), F32),
        grid=(rows // tm,),
        in_specs=[row(fw), row(pw),
                  pl.BlockSpec((HALO_ROWS, pw), lambda i: (jnp.maximum(i * halo_blocks - 1, 0), 0)),
                  row(d), _const_spec(wpool.shape), _const_spec(pscale.shape), _const_spec(wo.shape),
                  _const_spec(g.shape), _const_spec(b.shape)],
        out_specs=row(d),
        scratch_shapes=[pltpu.VMEM((HALO_ROWS + tm, pw), F32)],
        compiler_params=_params("parallel"),
        name="even_mix",
    )(att, u, u, y, wpool, pscale, wo, g, b)


def _pool_sample_kernel(ctx_ref, wpool_ref, pscale_ref, o_ref, *, pos0, n_new):
    gw = wpool_ref.shape[1]
    for i in range(n_new):
        n_avail = float(pos0 + i + 1)
        outs = []
        for g, w in enumerate(POOL_WINDOWS):
            lanes = slice(g * gw, (g + 1) * gw)
            cur = ctx_ref[POOL_BUF + i, :, lanes]
            win = cur
            for j in range(1, w):
                win = win + ctx_ref[POOL_BUF + i - j, :, lanes]
            mean = win / min(float(w), n_avail)
            outs.append(_dot((mean - cur).astype(BF16), wpool_ref[g]))
        o_ref[i] = jnp.concatenate(outs, axis=-1) * pscale_ref[...]


def _pool_sample(ctx_tm, wpool, pscale, *, pos0, n_new):
    steps, nb, c = ctx_tm.shape
    return pl.pallas_call(
        functools.partial(_pool_sample_kernel, pos0=pos0, n_new=n_new),
        out_shape=jax.ShapeDtypeStruct((n_new, nb, c), F32),
        grid=(1,),
        in_specs=[_const_spec(ctx_tm.shape), _const_spec(wpool.shape), _const_spec(pscale.shape)],
        out_specs=_const_spec((n_new, nb, c)),
        compiler_params=_params("arbitrary"),
        name="pool_sample",
    )(ctx_tm, wpool, pscale)


def _pair_mix_kernel(a_ref, p_ref, y_ref, wo_ref, g_ref, b_ref, o_ref, *, alpha):
    fw = a_ref.shape[1]
    mix = _dot(a_ref[...], wo_ref[0:fw, :]) + _dot(p_ref[...].astype(BF16), wo_ref[fw:, :])
    o_ref[...] = _layer_norm(alpha * y_ref[...] + mix, g_ref[...], b_ref[...])


def _pair_mix(a, p, y, wo, g, b, *, alpha, tm):
    rows, d = y.shape
    row = lambda w: pl.BlockSpec((tm, w), lambda i: (i, 0))
    return pl.pallas_call(
        functools.partial(_pair_mix_kernel, alpha=alpha),
        out_shape=jax.ShapeDtypeStruct((rows, d), F32),
        grid=(rows // tm,),
        in_specs=[row(a.shape[1]), row(p.shape[1]), row(d), _const_spec(wo.shape),
                  _const_spec(g.shape), _const_spec(b.shape)],
        out_specs=row(d),
        compiler_params=_params("parallel"),
        name="pair_mix",
    )(a, p, y, wo, g, b)


def _sample_new_tokens(qbd_ref, knew_ref, vnew_ref, lnew_ref, m_ref, l_ref, acc_ref, carry_ref, heads):
    qbd = qbd_ref[...]
    lf = lnew_ref[...]
    lane = lax.broadcasted_iota(jnp.int32, lf.shape, 1)
    tok = lax.broadcasted_iota(jnp.int32, lf.shape, 0) // heads
    upto = jnp.where(lax.broadcasted_iota(jnp.int32, (LANES, LANES), 0)
                     <= lax.broadcasted_iota(jnp.int32, (LANES, LANES), 1), 1.0, 0.0).astype(BF16)
    hi, mid, lo = _split3(lf)
    pre = _dot(hi, upto) + _dot(mid, upto) + _dot(lo, upto)
    n_col = jnp.sum(jnp.where(lane == tok, pre, 0.0), axis=1, keepdims=True)
    kn = knew_ref[...].astype(BF16)
    s = _dot_nt(qbd, kn) + n_col - pre[:, 0:SUBLANES]
    key = lax.broadcasted_iota(jnp.int32, s.shape, 1)
    qtok = lax.broadcasted_iota(jnp.int32, s.shape, 0) // heads
    s = jnp.where(key <= qtok, s, NEG_INF)
    m = jnp.max(s, axis=1, keepdims=True)
    p = jnp.exp(s - m)
    m_ref[...] = jnp.broadcast_to(m, m_ref.shape)
    l_ref[...] = jnp.broadcast_to(jnp.sum(p, axis=1, keepdims=True), l_ref.shape)
    acc_ref[...] = _dot(p.astype(BF16), vnew_ref[...].astype(BF16))
    carry_ref[...] = jnp.broadcast_to(n_col, carry_ref.shape)


def _sample_pages(qbd_ref, k_refs, v_refs, lf_refs, m_ref, l_ref, acc_ref, carry_ref, heads):
    pages_per_step = len(k_refs)
    n_rows = qbd_ref.shape[0]
    qbd = qbd_ref[...]
    src = lax.broadcasted_iota(jnp.int32, (PAGE_SIZE, PAGE_SIZE), 0)
    dst = lax.broadcasted_iota(jnp.int32, (PAGE_SIZE, PAGE_SIZE), 1)
    after = jnp.where(src > dst, 1.0, 0.0).astype(BF16)
    reps = n_rows // heads
    lf_all = jnp.concatenate([r[...] for r in lf_refs], axis=0)
    hi, mid, lo = _split3(lf_all)
    within_all = _dot(hi, after) + _dot(mid, after) + _dot(lo, after)
    total_all = jnp.sum(lf_all, axis=1, keepdims=True)
    carry = carry_ref[...]
    s_parts = []
    for t in range(pages_per_step):
        page = slice(t * heads, (t + 1) * heads)
        bias = carry + jnp.tile(within_all[page], (reps, 1))
        s_parts.append(_dot(qbd, k_refs[t][...].astype(BF16)) + bias)
        carry = carry + jnp.tile(total_all[page], (reps, 1))
    carry_ref[...] = carry
    s = jnp.concatenate(s_parts, axis=1)
    m_prev = m_ref[...]
    m_new = jnp.maximum(m_prev, jnp.max(s, axis=1, keepdims=True))
    p = jnp.exp(s - jnp.tile(m_new, (1, pages_per_step)))
    scale = jnp.exp(m_prev - m_new)
    m_ref[...] = m_new
    l_ref[...] = scale * l_ref[...] + jnp.sum(p, axis=1, keepdims=True)
    pb = p.astype(BF16)
    pv = _dot_nt(pb[:, 0:PAGE_SIZE], v_refs[0][...].astype(BF16))
    for t in range(1, pages_per_step):
        pv = pv + _dot_nt(pb[:, t * PAGE_SIZE:(t + 1) * PAGE_SIZE], v_refs[t][...].astype(BF16))
    acc_ref[...] = jnp.tile(scale, (1, acc_ref.shape[1] // LANES)) * acc_ref[...] + pv


def _sample_finish(o_ref, l_ref, acc_ref, heads):
    full = acc_ref[...] / jnp.tile(l_ref[...], (1, acc_ref.shape[1] // LANES))
    row_head = lax.broadcasted_iota(jnp.int32, full.shape, 0) % heads
    lane_head = lax.broadcasted_iota(jnp.int32, full.shape, 1) // D_HEAD
    own = jnp.where(row_head == lane_head, full, 0.0)
    o_ref[...] = jnp.sum(own.reshape(full.shape[0] // heads, heads, full.shape[1]), axis=1).astype(o_ref.dtype)


def _fox_sample_kernel(*refs, pages_per_step, heads):
    qbd_ref, knew_ref, vnew_ref, lnew_ref = refs[1:5]
    k_refs = refs[5:5 + pages_per_step]
    v_refs = refs[5 + pages_per_step:5 + 2 * pages_per_step]
    lf_refs = refs[5 + 2 * pages_per_step:5 + 3 * pages_per_step]
    o_ref, m_ref, l_ref, acc_ref, carry_ref = refs[5 + 3 * pages_per_step:]
    state = (m_ref, l_ref, acc_ref, carry_ref, heads)

    @pl.when(pl.program_id(1) == 0)
    def _():
        _sample_new_tokens(qbd_ref, knew_ref, vnew_ref, lnew_ref, *state)

    _sample_pages(qbd_ref, k_refs, v_refs, lf_refs, *state)

    @pl.when(pl.program_id(1) == pl.num_programs(1) - 1)
    def _():
        _sample_finish(o_ref, l_ref, acc_ref, heads)


def _fox_sample(qbd, knew, vnew, lnew, cache_k, cache_v, cache_lft, page_table, *, heads, n_new,
                pages_per_step):
    nb, n_rows, width = qbd.shape
    n_pages = page_table.shape[1]
    assert n_pages % pages_per_step == 0
    pt = page_table.reshape(-1)

    def page_map(t):
        def index(b, s, pt_ref):
            logical = n_pages - 1 - (s * pages_per_step + t)
            return (pt_ref[b * n_pages + logical], 0, 0)
        return index

    batch3 = lambda shape: pl.BlockSpec((None,) + shape, lambda b, s, pt_ref: (b, 0, 0))
    in_specs = [batch3((n_rows, width)), batch3((SUBLANES, width)), batch3((SUBLANES, width)),
                batch3((n_rows, LANES))]
    in_specs += [pl.BlockSpec((None, width, PAGE_SIZE), page_map(t)) for t in range(pages_per_step)]
    in_specs += [pl.BlockSpec((None, width, PAGE_SIZE), page_map(t)) for t in range(pages_per_step)]
    in_specs += [pl.BlockSpec((None, heads, PAGE_SIZE), page_map(t)) for t in range(pages_per_step)]
    return pl.pallas_call(
        functools.partial(_fox_sample_kernel, pages_per_step=pages_per_step, heads=heads),
        out_shape=jax.ShapeDtypeStruct((nb, n_new, width), BF16),
        grid_spec=pltpu.PrefetchScalarGridSpec(
            num_scalar_prefetch=1,
            grid=(nb, n_pages // pages_per_step),
            in_specs=in_specs,
            out_specs=batch3((n_new, width)),
            scratch_shapes=[pltpu.VMEM((n_rows, LANES), F32), pltpu.VMEM((n_rows, LANES), F32),
                            pltpu.VMEM((n_rows, width), F32), pltpu.VMEM((n_rows, LANES), F32)],
        ),
        compiler_params=_params("parallel", "arbitrary"),
        name="fox_sample",
    )(pt, qbd, knew, vnew, lnew, *([cache_k] * pages_per_step), *([cache_v] * pages_per_step),
      *([cache_lft] * pages_per_step))


def _odd_proj_kernel(*refs, tm):
    n_groups = len(DIL_CONFIGS)
    y_ref, w_ref = refs[0], refs[1]
    de_refs = refs[2:2 + 3 * n_groups]
    tail_refs = refs[2 + 3 * n_groups:2 + 5 * n_groups]
    stage_refs = refs[2 + 5 * n_groups:]
    yb = y_ref[...].astype(BF16)
    n_lane_groups = de_refs[0].shape[1]
    gw = n_lane_groups * LANES
    staged = []
    for g, (window, dil) in enumerate(DIL_CONFIGS):
        for part in range(3):
            col = (g * 3 + part) * gw
            res = _dot(yb, w_ref[:, col:col + gw])
            if part == 0:
                res = res * (D_HEAD ** -0.5 * LOG2_E)
            dst = de_refs[g * 3 + part]
            if dil == 1:
                for a in range(n_lane_groups):
                    dst[0, a] = res[:, a * LANES:(a + 1) * LANES].astype(BF16)
            else:
                s_ref = stage_refs[len(staged)]
                staged.append((s_ref, dst, dil))
                for a in range(n_lane_groups):
                    s_ref[a] = res[:, a * LANES:(a + 1) * LANES]
            if part > 0:
                keep = min(window, tm)
                tail_refs[g * 2 + part - 1][...] = res[tm - keep:, :].T
    for s_ref, dst, dil in staged:
        for r in range(dil):
            for a in range(n_lane_groups):
                dst[r, a] = s_ref[a, pl.ds(r, tm // dil, stride=dil), :].astype(BF16)


def _odd_proj(y3, w, *, tm):
    batch, seq_len, d = y3.shape
    gw = w.shape[1] // (3 * len(DIL_CONFIGS))
    n_lane_groups = gw // LANES
    n_tiles = seq_len // tm
    out_shape, out_specs, n_stage = [], [], 0
    for window, dil in DIL_CONFIGS:
        assert tm % (dil * 16) == 0 and (window % tm == 0 or tm % window == 0)
        for _ in range(3):
            out_shape.append(jax.ShapeDtypeStruct((batch, dil, n_lane_groups, seq_len // dil, LANES), BF16))
            out_specs.append(pl.BlockSpec((None, dil, n_lane_groups, tm // dil, LANES),
                                          lambda b, i: (b, 0, 0, i, 0)))
        n_stage += 3 if dil > 1 else 0
    for window, dil in DIL_CONFIGS:
        keep = min(window, tm)
        first = n_tiles - window // keep
        for _ in range(2):
            out_shape.append(jax.ShapeDtypeStruct((batch, gw, window), F32))
            out_specs.append(pl.BlockSpec((None, gw, keep),
                                          lambda b, i, first=first: (b, 0, jnp.maximum(i - first, 0))))
    return pl.pallas_call(
        functools.partial(_odd_proj_kernel, tm=tm),
        out_shape=tuple(out_shape),
        grid=(batch, n_tiles),
        in_specs=[pl.BlockSpec((None, tm, d), lambda b, i: (b, i, 0)),
                  pl.BlockSpec(w.shape, lambda b, i: (0, 0), pipeline_mode=pl.Buffered(1))],
        out_specs=tuple(out_specs),
        scratch_shapes=[pltpu.VMEM((n_lane_groups, tm, LANES), F32)] * n_stage,
        compiler_params=_params("parallel", "arbitrary"),
        name="odd_proj",
    )(y3, w)


def _bmm_nt(a, b):
    return lax.dot_general(a, b, (((2,), (2,)), ((0,), (0,))), preferred_element_type=F32)


def _bmm(a, b):
    return lax.dot_general(a, b, (((2,), (1,)), ((0,), (0,))), preferred_element_type=F32)


def _dil_attn_kernel(q_ref, kh_ref, kc_ref, vh_ref, vc_ref, bfirst_ref, bprev_ref, o_ref, lse_ref,
                     *, dil, heads, qb):
    nblk = q_ref.shape[2] // qb
    units = [(r, j) for r in range(dil) for j in range(nblk)]
    even_lane = _lane_is_even_head((1, qb, LANES))

    def cur_blocks(ref, g):
        return jnp.concatenate([ref[r, g].reshape(nblk, qb, LANES) for r in range(dil)], axis=0)

    def prev_blocks(cur, halo_ref, g):
        parts = []
        for r in range(dil):
            parts.append(halo_ref[r, g][None])
            if nblk > 1:
                parts.append(cur[r * nblk:(r + 1) * nblk - 1])
        return jnp.concatenate(parts, axis=0)

    for g in range(heads // HEADS_PER_LANE_GROUP):
        q2, kc, vc = cur_blocks(q_ref, g), cur_blocks(kc_ref, g), cur_blocks(vc_ref, g)
        k2 = jnp.concatenate([prev_blocks(kc, kh_ref, g), kc], axis=1)
        v2 = jnp.concatenate([prev_blocks(vc, vh_ref, g), vc], axis=1)
        outs, lses = [], []
        raw = [_bmm_nt(jnp.where(even_lane if sub == 0 else ~even_lane, q2, jnp.zeros_like(q2)), k2)
               for sub in range(HEADS_PER_LANE_GROUP)]
        for sub in range(HEADS_PER_LANE_GROUP):
            h = g * HEADS_PER_LANE_GROUP + sub
            bias = jnp.stack([bfirst_ref[h] if j == 0 else bprev_ref[h] for _, j in units])
            s = raw[sub] + bias
            m = jnp.max(s, axis=2, keepdims=True)
            p = jnp.exp2(s - m)
            den = jnp.sum(p, axis=2, keepdims=True)
            pv = _bmm(p.astype(BF16), v2)
            outs.append(pv / den)
            lses.append(jnp.broadcast_to(m * LN_2 + jnp.log(den), pv.shape))
        o2 = jnp.where(even_lane, outs[0], outs[1])
        l2 = jnp.where(even_lane, lses[0], lses[1])
        for u, (r, j) in enumerate(units):
            rows = pl.ds(j * qb, qb) if dil == 1 else pl.ds(r + dil * qb * j, qb, stride=dil)
            o_ref[g, rows, :] = o2[u]
            lse_ref[g, rows, :] = l2[u]


def _dil_attention(q, k, v, bias_first, bias, *, heads, qb, units):
    batch, dil, n_lane_groups, slots, _ = q.shape
    nblk = units // dil
    span = qb * nblk
    cur = pl.BlockSpec((None, dil, n_lane_groups, span, LANES), lambda b, i: (b, 0, 0, i, 0))
    halo = pl.BlockSpec((None, dil, n_lane_groups, qb, LANES),
                        lambda b, i: (b, 0, 0, jnp.maximum(i * nblk - 1, 0), 0))
    nat = pl.BlockSpec((None, n_lane_groups, span * dil, LANES), lambda b, i: (b, 0, i, 0))
    out_sds = jax.ShapeDtypeStruct((batch, n_lane_groups, slots * dil, LANES), F32)
    return pl.pallas_call(
        functools.partial(_dil_attn_kernel, dil=dil, heads=heads, qb=qb),
        out_shape=(out_sds, out_sds),
        grid=(batch, slots // span),
        in_specs=[cur, halo, cur, halo, cur,
                  pl.BlockSpec((None, heads, qb, 2 * qb), lambda b, i: (jnp.minimum(i, 1), 0, 0, 0)),
                  _const_spec(bias.shape)],
        out_specs=(nat, nat),
        compiler_params=_params("parallel", "arbitrary"),
        name=f"dil_attention_d{dil}",
    )(q, k, k, v, v, bias_first, bias)


def _odd_mix_kernel(*refs, alpha):
    n = len(DIL_CONFIGS)
    o_refs, lse_refs = refs[:n], refs[n:2 * n]
    y_ref, wo_ref, g_ref, b_ref, out_ref = refs[2 * n:]
    merged = []
    for a in range(o_refs[0].shape[0]):
        lses = [r[a] for r in lse_refs]
        top = functools.reduce(jnp.maximum, lses)
        wts = [jnp.exp(l - top) for l in lses]
        num = functools.reduce(jnp.add, [w * r[a] for w, r in zip(wts, o_refs)])
        merged.append((num / functools.reduce(jnp.add, wts)).astype(BF16))
    mix = _dot(jnp.concatenate(merged, axis=-1), wo_ref[...])
    out_ref[...] = _layer_norm(alpha * y_ref[...] + mix, g_ref[...], b_ref[...])


def _odd_mix(outs, lses, y3, wo, g, b, *, alpha, tm):
    batch, seq_len, d = y3.shape
    n_lane_groups = outs[0].shape[1]
    part = pl.BlockSpec((None, n_lane_groups, tm, LANES), lambda b, i: (b, 0, i, 0))
    row = pl.BlockSpec((None, tm, d), lambda b, i: (b, i, 0))
    return pl.pallas_call(
        functools.partial(_odd_mix_kernel, alpha=alpha),
        out_shape=jax.ShapeDtypeStruct((batch, seq_len, d), F32),
        grid=(batch, seq_len // tm),
        in_specs=[part] * (2 * len(outs)) + [row, _const_spec(wo.shape), _const_spec(g.shape),
                                             _const_spec(b.shape)],
        out_specs=row,
        compiler_params=_params("parallel", "parallel"),
        name="odd_mix",
    )(*outs, *lses, y3, wo, g, b)


def _proj_kernel(x_ref, w_ref, o_ref):
    o_ref[...] = _dot(x_ref[...].astype(BF16), w_ref[...])


def _proj(x, w, *, tn):
    rows, d = x.shape
    n = w.shape[1]
    return pl.pallas_call(
        _proj_kernel,
        out_shape=jax.ShapeDtypeStruct((rows, n), F32),
        grid=(n // tn,),
        in_specs=[_const_spec((rows, d)), pl.BlockSpec((d, tn), lambda j: (0, j))],
        out_specs=pl.BlockSpec((rows, tn), lambda j: (0, j)),
        compiler_params=_params("parallel"),
        name="proj",
    )(x, w)


def _dil_sample_kernel(qbd_ref, kc_ref, vc_ref, kn_ref, vn_ref, tabc_ref, tabn_ref,
                       o_ref, lse_ref, ko_ref, vo_ref, *, n_new, heads):
    width, buf_len = kc_ref.shape
    qbd = qbd_ref[...]
    chunk = min(buf_len, 512)
    starts = range(0, buf_len, chunk)
    sc = [_dot(qbd, kc_ref[:, c0:c0 + chunk].astype(BF16)) + tabc_ref[:, c0:c0 + chunk] for c0 in starts]
    sn = _dot_nt(qbd, kn_ref[...].astype(BF16)) + tabn_ref[...]
    m = functools.reduce(jnp.maximum, [jnp.max(s, axis=1, keepdims=True) for s in sc + [sn]])
    pn = jnp.exp(sn - m)
    den = jnp.sum(pn, axis=1, keepdims=True)
    pv = _dot(pn.astype(BF16), vn_ref[...].astype(BF16))
    for s, c0 in zip(sc, starts):
        p = jnp.exp(s - m)
        den = den + jnp.sum(p, axis=1, keepdims=True)
        pv = pv + _dot_nt(p.astype(BF16), vc_ref[:, c0:c0 + chunk].astype(BF16))
    full = pv / den
    lse = jnp.broadcast_to(m + jnp.log(den), full.shape)
    own = (lax.broadcasted_iota(jnp.int32, full.shape, 0) % heads
           == lax.broadcasted_iota(jnp.int32, full.shape, 1) // D_HEAD)
    o_ref[...] = jnp.sum(jnp.where(own, full, 0.0).reshape(n_new, heads, width), axis=1)
    lse_ref[...] = jnp.sum(jnp.where(own, lse, 0.0).reshape(n_new, heads, width), axis=1)
    rows = 64
    is_new = lax.broadcasted_iota(jnp.int32, (rows, LANES), 1) >= LANES - n_new
    zero_rows = jnp.zeros((LANES - SUBLANES, width), F32)
    for src_ref, new_ref, dst_ref in ((kc_ref, kn_ref, ko_ref), (vc_ref, vn_ref, vo_ref)):
        tail = jnp.concatenate([zero_rows, new_ref[...]], axis=0).T
        for r0 in range(0, width, rows):
            rolled = pltpu.roll(src_ref[r0:r0 + rows, :], buf_len - n_new, axis=1)
            if buf_len > LANES:
                dst_ref[r0:r0 + rows, 0:buf_len - LANES] = rolled[:, 0:buf_len - LANES]
            dst_ref[r0:r0 + rows, buf_len - LANES:] = jnp.where(is_new, tail[r0:r0 + rows, :],
                                                                rolled[:, buf_len - LANES:])


def _dil_sample(qbd, kc, vc, kn, vn, tab_c, tab_n, *, n_new, heads):
    nb, width, buf_len = kc.shape
    n_rows = qbd.shape[1]
    b3 = lambda *shape: pl.BlockSpec((None,) + shape, lambda b: (b,) + (0,) * len(shape))
    small = jax.ShapeDtypeStruct((nb, n_new, width), F32)
    big = jax.ShapeDtypeStruct((nb, width, buf_len), F32)
    return pl.pallas_call(
        functools.partial(_dil_sample_kernel, n_new=n_new, heads=heads),
        out_shape=(small, small, big, big),
        grid=(nb,),
        in_specs=[b3(n_rows, width), b3(width, buf_len), b3(width, buf_len),
                  b3(SUBLANES, width), b3(SUBLANES, width),
                  _const_spec(tab_c.shape), _const_spec(tab_n.shape)],
        out_specs=(b3(n_new, width), b3(n_new, width), b3(width, buf_len), b3(width, buf_len)),
        compiler_params=_params("parallel"),
        name=f"dil_sample_l{buf_len}",
    )(qbd, kc, vc, kn, vn, tab_c, tab_n)


def _rel_bucket(dist):
    exact = REL_BUCKETS // 2
    d = jnp.maximum(dist, 1).astype(F32)
    large = exact + (jnp.log(d / exact) / math.log(REL_MAX_DIST / exact) * (REL_BUCKETS - exact)).astype(jnp.int32)
    large = jnp.minimum(large, REL_BUCKETS - 1)
    return jnp.where(dist < exact, dist, large)


def _group_bias(rel_bias, g, window, dil, heads):
    dist = jnp.arange(window // dil + 1) * dil
    onehot = _rel_bucket(dist)[:, None] == jnp.arange(REL_BUCKETS)[None, :]
    cols = rel_bias[:, g * heads:(g + 1) * heads].astype(F32)
    return jnp.sum(jnp.where(onehot[:, :, None], cols[None], 0.0), axis=1).T


def _toeplitz(w, n):
    heads, period = w.shape
    flat = jnp.tile(w, (1, n))[:, :n * (period - 1)]
    return flat.reshape(heads, n, period - 1)[:, :, :n]


def _prompt_bias_tiles(bias, qb):
    heads, n_keys = bias.shape
    assert n_keys == qb + 1
    neg = jnp.full((heads, qb), NEG_INF, F32)
    cur = _toeplitz(jnp.concatenate([bias[:, 0:qb], neg], axis=1), qb)
    prev = _toeplitz(jnp.concatenate([bias[:, qb:qb + 1], neg, bias[:, 1:qb]], axis=1), qb)
    cur, prev = jnp.swapaxes(cur, 1, 2), jnp.swapaxes(prev, 1, 2)
    both = jnp.concatenate([prev, cur], axis=2)
    no_prev = jnp.concatenate([jnp.full_like(prev, NEG_INF), cur], axis=2)
    return jnp.stack([no_prev, both]), both


def _sample_bias_tables(bias, dil, n_new, buf_len):
    heads, n_keys = bias.shape
    n_back = n_keys - 1
    assert n_back * dil == buf_len
    oldest_first = bias[:, :0:-1]
    gaps = jnp.full((heads, n_back, dil - 1), NEG_INF, F32)
    spread = jnp.concatenate([oldest_first[:, :, None], gaps], axis=2).reshape(heads, buf_len)
    tab_c = [jnp.concatenate([jnp.full((heads, i), NEG_INF, F32), spread[:, :buf_len - i]], axis=1)
             for i in range(n_new)]
    pick = np.zeros((n_keys, n_new, SUBLANES), np.float32)
    for i in range(n_new):
        for j in range(i + 1):
            if (i - j) % dil == 0:
                pick[(i - j) // dil, i, SUBLANES - n_new + j] = 1.0
    picked = jnp.einsum("hk,kic->ihc", bias, jnp.asarray(pick), precision=lax.Precision.HIGHEST)
    tab_n = jnp.where(jnp.asarray(pick.sum(axis=0) > 0)[:, None, :], picked, NEG_INF)
    return jnp.concatenate(tab_c, axis=0), tab_n.reshape(n_new * heads, SUBLANES)


PROMPT_ROW_TILE = 512
ATTN_BLOCK = 512
DIL_SLOT_BLOCK = 128
DIL_UNITS_PER_STEP = 16
PAGES_PER_STEP = 32


def _pad_axis(x, axis, size):
    pad = [(0, 0)] * x.ndim
    pad[axis] = (0, size - x.shape[axis])
    return jnp.pad(x, pad)


def token_major(xt, lead):
    return jnp.transpose(xt.reshape(lead, -1, D_HEAD, xt.shape[-1]), (0, 3, 1, 2))


def kernel(x_prompt, x_sample, cache_fox_k, cache_fox_v, cache_fox_logf, state_pool, cache_dil0_k, cache_dil0_v, cache_dil1_k, cache_dil1_v, cache_dil2_k, cache_dil2_v, page_table, p_prompt, p_sample, w_in_even, b_fgate, pool_w, pool_scale, w_out_even, w_in_odd, w_out_odd, rel_bias, ffn1_wg, ffn1_wu, ffn1_wd, ffn2_wg, ffn2_wu, ffn2_wd, ln_g, ln_b, ple_wg, ple_bg, ple_wp):
    depth = ffn1_wg.shape[0]
    alpha = (2 * depth) ** 0.25
    batch, seq_len, d_model = x_prompt.shape
    nb, n_new, _ = x_sample.shape
    past_len = page_table.shape[1] * PAGE_SIZE
    fox_heads = cache_fox_k.shape[-2]
    fox_w = fox_heads * D_HEAD
    dil_heads = cache_dil0_k.shape[-2]
    dil_w = dil_heads * D_HEAD
    dil_caches_k = (cache_dil0_k, cache_dil1_k, cache_dil2_k)
    dil_caches_v = (cache_dil0_v, cache_dil1_v, cache_dil2_v)
    rows_p, rows_s = batch * seq_len, nb * n_new
    tm = PROMPT_ROW_TILE
    bf = lambda w: w.astype(BF16)
    vec = lambda a: a.reshape(1, -1)

    yp = x_prompt.reshape(rows_p, d_model)
    ys = x_sample.reshape(rows_s, d_model)
    outs = {name: [] for name in ("fkp", "fvp", "flp", "fks", "fvs", "fls", "plp", "pls")}
    dkp, dvp, dks, dvs = ([[] for _ in DIL_CONFIGS] for _ in range(4))

    for i in range(depth):
        w1 = (ffn1_wg, ffn1_wu, ffn1_wd, vec(ln_g[i, 0]), vec(ln_b[i, 0]))
        yp = _ffn(yp, *w1, layer=i, alpha=alpha, tm=tm)
        ys = _ffn(ys, *w1, layer=i, alpha=alpha, tm=rows_s)
        g1, b1 = vec(ln_g[i, 1]), vec(ln_b[i, 1])
        if i % 2 == 0:
            e = i // 2
            w_in = w_in_even[e]
            wqkv = bf(w_in[:, :3 * fox_w])
            wf = bf(_pad_axis(w_in[:, 3 * fox_w:3 * fox_w + fox_heads], 1, LANES))
            bfg = _pad_axis(vec(b_fgate[e]), 1, LANES)
            wu = bf(w_in[:, 3 * fox_w + fox_heads:])
            wpool, pscale, wo = bf(pool_w[e]), vec(pool_scale[e]), bf(w_out_even[e])
            qa, ka, va, kt, vt, lft, u = _even_proj(yp.reshape(batch, seq_len, d_model), wqkv, wf, bfg, wu,
                                                    heads=fox_heads, tm=tm, prompt=True)
            att = _fox_attention(qa, ka, va, heads=fox_heads, tq=ATTN_BLOCK).reshape(rows_p, fox_w)
            u = u.reshape(rows_p, -1)
            yp_next = _even_mix(att, u, yp, wpool, pscale, wo, g1, b1, alpha=alpha, seq_len=seq_len, tm=tm)
            outs["fkp"].append(token_major(kt, batch))
            outs["fvp"].append(token_major(vt, batch))
            outs["flp"].append(jnp.swapaxes(lft, 1, 2))
            outs["plp"].append(u.reshape(batch, seq_len, -1)[:, seq_len - POOL_BUF:])
            qs, ks, vs, lfts, us = _even_proj(ys[None], wqkv, wf, bfg, wu, heads=fox_heads, tm=rows_s,
                                              prompt=False)
            head_mask = (jnp.arange(fox_w)[None, :] // D_HEAD == jnp.arange(fox_heads)[:, None]).astype(BF16)
            qbd = (qs.reshape(nb, n_new, 1, fox_w) * head_mask).reshape(nb, n_new * fox_heads, fox_w)
            knew = _pad_axis(ks.reshape(nb, n_new, fox_w), 1, SUBLANES)
            vnew = _pad_axis(vs.reshape(nb, n_new, fox_w), 1, SUBLANES)
            lf3 = jnp.swapaxes(lfts.reshape(fox_heads, nb, n_new), 0, 1)
            lnew = jnp.broadcast_to(lf3[:, None], (nb, n_new, fox_heads, n_new)).reshape(nb, n_new * fox_heads, n_new)
            lnew = _pad_axis(lnew, 2, LANES)
            n_phys = cache_fox_k.shape[1]
            page_t = lambda c: jnp.transpose(c, (0, 2, 3, 1)).reshape(n_phys, fox_w, PAGE_SIZE)
            att_s = _fox_sample(qbd, knew, vnew, lnew, page_t(cache_fox_k[e]), page_t(cache_fox_v[e]),
                                jnp.swapaxes(cache_fox_logf[e], 1, 2), page_table,
                                heads=fox_heads, n_new=n_new, pages_per_step=PAGES_PER_STEP)
            ctx_s = jnp.concatenate([state_pool[e].astype(F32), us.reshape(nb, n_new, -1)], axis=1)
            pool_s = _pool_sample(jnp.swapaxes(ctx_s, 0, 1), wpool, pscale, pos0=past_len, n_new=n_new)
            pool_s = jnp.swapaxes(pool_s, 0, 1).reshape(rows_s, -1)
            ys_next = _pair_mix(att_s.reshape(rows_s, fox_w), pool_s, ys, wo, g1, b1, alpha=alpha, tm=rows_s)
            outs["fks"].append(ks.reshape(nb, n_new, fox_heads, D_HEAD))
            outs["fvs"].append(vs.reshape(nb, n_new, fox_heads, D_HEAD))
            outs["fls"].append(jnp.swapaxes(lf3, 1, 2))
            outs["pls"].append(ctx_s[:, -POOL_BUF:])
        else:
            o = i // 2
            w_in, wo = bf(w_in_odd[o]), bf(w_out_odd[o])
            biases = [_group_bias(rel_bias, g, window, dil, dil_heads) for g, (window, dil) in enumerate(DIL_CONFIGS)]
            proj = _odd_proj(yp.reshape(batch, seq_len, d_model), w_in, tm=tm)
            n_groups = len(DIL_CONFIGS)
            o_parts, lse_parts = [], []
            for g, (window, dil) in enumerate(DIL_CONFIGS):
                qd, kd, vd = proj[3 * g:3 * g + 3]
                bias_first, bias_both = _prompt_bias_tiles(biases[g] * LOG2_E, DIL_SLOT_BLOCK)
                og, lg = _dil_attention(qd, kd, vd, bias_first, bias_both, heads=dil_heads,
                                        qb=DIL_SLOT_BLOCK, units=DIL_UNITS_PER_STEP)
                o_parts.append(og)
                lse_parts.append(lg)
                dkp[g].append(token_major(proj[3 * n_groups + 2 * g], batch))
                dvp[g].append(token_major(proj[3 * n_groups + 2 * g + 1], batch))
            yp_next = _odd_mix(o_parts, lse_parts, yp.reshape(batch, seq_len, d_model), wo, g1, b1,
                               alpha=alpha, tm=tm).reshape(rows_p, d_model)
            lane_major = lambda a: jnp.swapaxes(a.reshape(rows_s, dil_w // LANES, LANES), 0, 1)[None]
            proj_s = _proj(ys, w_in, tn=3 * dil_w).reshape(nb, n_new, n_groups, 3, dil_w)
            head_mask = (jnp.arange(dil_w)[None, :] // D_HEAD == jnp.arange(dil_heads)[:, None]).astype(F32)
            o_parts, lse_parts = [], []
            for g, (window, dil) in enumerate(DIL_CONFIGS):
                qg = proj_s[:, :, g, 0] * (D_HEAD ** -0.5)
                qbd = (qg[:, :, None, :] * head_mask).astype(BF16).reshape(nb, n_new * dil_heads, dil_w)
                k_new, v_new = proj_s[:, :, g, 1], proj_s[:, :, g, 2]
                buf_len = dil_caches_k[g].shape[2]
                pos_minor = lambda c: jnp.transpose(c, (0, 2, 3, 1)).reshape(nb, dil_w, buf_len)
                rows_last = lambda x: jnp.pad(x, ((0, 0), (SUBLANES - n_new, 0), (0, 0)))
                tab_c, tab_n = _sample_bias_tables(biases[g], dil, n_new, buf_len)
                og, lg, k_roll, v_roll = _dil_sample(
                    qbd, pos_minor(dil_caches_k[g][o]), pos_minor(dil_caches_v[g][o]),
                    rows_last(k_new), rows_last(v_new), tab_c, tab_n, n_new=n_new, heads=dil_heads)
                o_parts.append(lane_major(og))
                lse_parts.append(lane_major(lg))
                dks[g].append(token_major(k_roll, nb))
                dvs[g].append(token_major(v_roll, nb))
            ys_next = _odd_mix(o_parts, lse_parts, ys[None], wo, g1, b1, alpha=alpha,
                               tm=rows_s).reshape(rows_s, d_model)
        yp, ys = yp_next, ys_next
        w2 = (ffn2_wg, ffn2_wu, ffn2_wd, vec(ln_g[i, 2]), vec(ln_b[i, 2]))
        ple_w = (ple_wg, vec(ple_bg[i]), ple_wp)
        yp = _ffn(yp, *w2, layer=i, alpha=alpha, tm=tm, ple=(p_prompt.reshape(depth, rows_p, -1),) + ple_w)
        ys = _ffn(ys, *w2, layer=i, alpha=alpha, tm=rows_s, ple=(p_sample.reshape(depth, rows_s, -1),) + ple_w)

    stack = lambda parts: jnp.stack(parts)
    result = [yp.reshape(batch, seq_len, d_model), ys.reshape(nb, n_new, d_model)]
    result += [stack(outs[name]) for name in ("fkp", "fvp", "flp", "fks", "fvs", "fls", "plp", "pls")]
    for g in range(len(DIL_CONFIGS)):
        result += [stack(dkp[g]), stack(dvp[g])]
    for g in range(len(DIL_CONFIGS)):
        result += [stack(dks[g]), stack(dvs[g])]
    return tuple(result)
```
